```python
import math
import jax, jax.numpy as jnp
from jax import lax
import numpy as np

D_MODEL = 1024
BATCH = 4
SEQ = 8192
DEPTH = 4

HEAD_DIM = 64
N_HEADS = D_MODEL // HEAD_DIM
H_FOX = N_HEADS // 4
H_DSA = N_HEADS // 4
H_DIL = N_HEADS // 4
H_MOBA = N_HEADS - H_FOX - H_DSA - H_DIL
IDX_HEADS = 8
IDX_DIM = 32
DSA_TOPK = 256
DIL_PATTERNS = ((128, 1), (512, 4), (2048, 16))
MOBA_BLOCK = 256
MOBA_TOPK = 3
Q_BLOCK = 128
MOBA_Q_BLOCK = 64
D_FF = 4 * D_MODEL
ROPE_THETA = 10000.0
LN_EPS = 1e-5
DEEPNORM_ALPHA = (2.0 * DEPTH) ** 0.25
DEEPNORM_BETA = (8.0 * DEPTH) ** -0.25
ATTN_SCALE = HEAD_DIM ** -0.5

SPLIT_SIZES = (
    H_FOX * HEAD_DIM, H_FOX * HEAD_DIM, H_FOX * HEAD_DIM, H_FOX,
    H_DSA * HEAD_DIM, H_DSA * HEAD_DIM, H_DSA * HEAD_DIM,
    IDX_HEADS * IDX_DIM, IDX_DIM, IDX_HEADS,
    H_DIL * HEAD_DIM, H_DIL * HEAD_DIM, H_DIL * HEAD_DIM,
    H_MOBA * HEAD_DIM, H_MOBA * HEAD_DIM, H_MOBA * HEAD_DIM,
)
D_IN = sum(SPLIT_SIZES)

kernel_name = "hymba_style_fox_dsa_dilated_moba_trunk"

F32 = jnp.float32


def layer_norm(x, g, b):
    xf = x.astype(F32)
    mu = jnp.mean(xf, axis=-1, keepdims=True)
    var = jnp.mean(jnp.square(xf - mu), axis=-1, keepdims=True)
    return ((xf - mu) * lax.rsqrt(var + LN_EPS) * g.astype(F32) + b.astype(F32)).astype(x.dtype)


def rope(x, pos):
    d = x.shape[-1]
    half = d // 2
    inv = ROPE_THETA ** (-jnp.arange(half, dtype=F32) / half)
    ang = pos.astype(F32)[:, None] * inv[None, :]
    cos = jnp.cos(ang)[None, :, None, :]
    sin = jnp.sin(ang)[None, :, None, :]
    xf = x.astype(F32)
    x1, x2 = xf[..., :half], xf[..., half:]
    return jnp.concatenate([x1 * cos - x2 * sin, x2 * cos + x1 * sin], axis=-1).astype(x.dtype)


def fox_attention(q, k, v, logf):
    B, T, H, d = q.shape
    c = jnp.cumsum(logf, axis=1)
    c_key = c.transpose(0, 2, 1)
    nb = T // Q_BLOCK
    qb = q.reshape(B, nb, Q_BLOCK, H, d).transpose(1, 0, 2, 3, 4)
    cb = c.reshape(B, nb, Q_BLOCK, H).transpose(1, 0, 3, 2)
    kpos = jnp.arange(T)

    def body(args):
        i, qi, ci = args
        s = jnp.einsum('bqhd,bkhd->bhqk', qi, k, preferred_element_type=F32) * ATTN_SCALE
        s = s + ci[..., None] - c_key[:, :, None, :]
        qpos = i * Q_BLOCK + jnp.arange(Q_BLOCK)
        s = jnp.where((kpos[None, :] <= qpos[:, None])[None, None], s, -jnp.inf)
        p = jax.nn.softmax(s, axis=-1)
        return jnp.einsum('bhqk,bkhd->bqhd', p.astype(v.dtype), v)

    out = lax.map(body, (jnp.arange(nb), qb, cb))
    return out.transpose(1, 0, 2, 3, 4).reshape(B, T, H, d)


def dsa_attention(q, k, v, qi, ki, wi):
    B, T, H, d = q.shape
    topk = min(DSA_TOPK, T // 4)
    nb = T // Q_BLOCK
    kpos = jnp.arange(T)
    bidx = jnp.arange(B)[:, None, None]
    qbs = q.reshape(B, nb, Q_BLOCK, H, d).transpose(1, 0, 2, 3, 4)
    qib = qi.reshape(B, nb, Q_BLOCK, IDX_HEADS, IDX_DIM).transpose(1, 0, 2, 3, 4)
    wib = wi.reshape(B, nb, Q_BLOCK, IDX_HEADS).transpose(1, 0, 2, 3)

    def body(args):
        i, qq, qx, wx = args
        qpos = i * Q_BLOCK + jnp.arange(Q_BLOCK)
        rel = jax.nn.relu(jnp.einsum('bqhe,bke->bqhk', qx, ki, preferred_element_type=F32))
        score = jnp.einsum('bqh,bqhk->bqk', wx.astype(F32), rel)
        score = jnp.where((kpos[None, :] <= qpos[:, None])[None], score, -jnp.inf)
        _, idx = lax.top_k(score, topk)
        valid = idx <= qpos[None, :, None]
        kg = k[bidx, idx]
        vg = v[bidx, idx]
        s = jnp.einsum('bqhd,bqkhd->bhqk', qq, kg, preferred_element_type=F32) * ATTN_SCALE
        s = jnp.where(valid[:, None], s, -jnp.inf)
        p = jax.nn.softmax(s, axis=-1)
        return jnp.einsum('bhqk,bqkhd->bqhd', p.astype(v.dtype), vg)

    out = lax.map(body, (jnp.arange(nb), qbs, qib, wib))
    return out.transpose(1, 0, 2, 3, 4).reshape(B, T, H, d)


def dilated_attention(q, k, v):
    B, T, H, d = q.shape
    unit = Q_BLOCK * max(r for _, r in DIL_PATTERNS)
    Tp = -(-T // unit) * unit
    padw = ((0, 0), (0, Tp - T), (0, 0), (0, 0))
    qp, kp, vp = jnp.pad(q, padw), jnp.pad(k, padw), jnp.pad(v, padw)
    outs, lses = [], []
    for (w, r) in DIL_PATTERNS:
        n_sub = Tp // r
        nblk = n_sub // Q_BLOCK
        qs = qp.reshape(B, nblk, Q_BLOCK, r, H, d)
        ks = kp.reshape(B, nblk, Q_BLOCK, r, H, d)
        vs = vp.reshape(B, nblk, Q_BLOCK, r, H, d)
        kband = jnp.concatenate([jnp.concatenate([jnp.zeros_like(ks[:, :1]), ks[:, :-1]], axis=1), ks], axis=2)
        vband = jnp.concatenate([jnp.concatenate([jnp.zeros_like(vs[:, :1]), vs[:, :-1]], axis=1), vs], axis=2)
        s = jnp.einsum('bnqrhd,bnkrhd->bnrhqk', qs, kband, preferred_element_type=F32) * ATTN_SCALE
        qi = jnp.arange(Q_BLOCK)[:, None]
        kj = jnp.arange(2 * Q_BLOCK)[None, :]
        dist = qi + Q_BLOCK - kj
        band = (dist >= 0) & (dist <= w // r)
        exists = (jnp.arange(nblk)[:, None] * Q_BLOCK + jnp.arange(2 * Q_BLOCK)[None, :] - Q_BLOCK) >= 0
        mask = band[None] & exists[:, None, :]
        s = jnp.where(mask[None, :, None, None], s, -jnp.inf)
        m = jnp.max(s, axis=-1, keepdims=True)
        e = jnp.exp(s - m)
        den = jnp.sum(e, axis=-1)
        o = jnp.einsum('bnrhqk,bnkrhd->bnqrhd', e, vband.astype(F32)) / den.transpose(0, 1, 4, 2, 3)[..., None]
        lse = (m[..., 0] + jnp.log(den)).transpose(0, 1, 4, 2, 3)
        outs.append(o.reshape(B, Tp, H, d))
        lses.append(lse.reshape(B, Tp, H))
    wts = jax.nn.softmax(jnp.stack(lses, axis=0), axis=0)
    out = jnp.sum(wts[..., None] * jnp.stack(outs, axis=0), axis=0)
    return out[:, :T].astype(v.dtype)


def moba_attention(q, k, v):
    B, T, H, d = q.shape
    Tp = -(-T // MOBA_BLOCK) * MOBA_BLOCK
    padw = ((0, 0), (0, Tp - T), (0, 0), (0, 0))
    qp, kp, vp = jnp.pad(q, padw), jnp.pad(k, padw), jnp.pad(v, padw)
    nkb = Tp // MOBA_BLOCK
    topn = min(MOBA_TOPK, nkb)
    kb = kp.reshape(B, nkb, MOBA_BLOCK, H, d)
    vb = vp.reshape(B, nkb, MOBA_BLOCK, H, d)
    kmean = jnp.mean(kb.astype(F32), axis=2)
    kbt = kb.transpose(0, 3, 1, 2, 4)
    vbt = vb.transpose(0, 3, 1, 2, 4)
    nqb = Tp // MOBA_Q_BLOCK
    per = MOBA_BLOCK // MOBA_Q_BLOCK
    qbs = qp.reshape(B, nqb, MOBA_Q_BLOCK, H, d).transpose(1, 0, 2, 3, 4)
    bidx = jnp.arange(B)[:, None, None, None]
    hidx = jnp.arange(H)[None, None, :, None]
    blk_ids = jnp.arange(nkb)

    def body(args):
        i, qi = args
        own = i // per
        gate = jnp.einsum('bqhd,bnhd->bqhn', qi.astype(F32), kmean)
        gate = jnp.where(blk_ids < own, gate, -jnp.inf)
        _, sel = lax.top_k(gate, topn)
        valid = sel < own
        kg = kbt[bidx, hidx, sel]
        vg = vbt[bidx, hidx, sel]
        s_sel = jnp.einsum('bqhd,bqhnkd->bqhnk', qi, kg, preferred_element_type=F32) * ATTN_SCALE
        s_sel = jnp.where(valid[..., None], s_sel, -jnp.inf).reshape(B, MOBA_Q_BLOCK, H, topn * MOBA_BLOCK)
        kown = lax.dynamic_index_in_dim(kb, own, axis=1, keepdims=False)
        vown = lax.dynamic_index_in_dim(vb, own, axis=1, keepdims=False)
        s_own = jnp.einsum('bqhd,bkhd->bqhk', qi, kown, preferred_element_type=F32) * ATTN_SCALE
        qpos = i * MOBA_Q_BLOCK + jnp.arange(MOBA_Q_BLOCK)
        kpos = own * MOBA_BLOCK + jnp.arange(MOBA_BLOCK)
        s_own = jnp.where((kpos[None, :] <= qpos[:, None])[None, :, None, :], s_own, -jnp.inf)
        p = jax.nn.softmax(jnp.concatenate([s_sel, s_own], axis=-1), axis=-1)
        p_sel = p[..., :topn * MOBA_BLOCK].reshape(B, MOBA_Q_BLOCK, H, topn, MOBA_BLOCK)
        p_own = p[..., topn * MOBA_BLOCK:]
        o = jnp.einsum('bqhnk,bqhnkd->bqhd', p_sel.astype(v.dtype), vg) + \
            jnp.einsum('bqhk,bkhd->bqhd', p_own.astype(v.dtype), vown)
        return o.astype(v.dtype)

    out = lax.map(body, (jnp.arange(nqb), qbs))
    return out.transpose(1, 0, 2, 3, 4).reshape(B, Tp, H, d)[:, :T]


def hybrid_mixer(x, w_in, b_f, w_o):
    B, T, _ = x.shape
    pos = jnp.arange(T)
    proj = jnp.einsum('btd,de->bte', x, w_in)
    split_points = [int(s) for s in np.cumsum(SPLIT_SIZES)[:-1]]
    (qa, ka, va, fa, qb, kb, vb, qi, ki, wi, qc, kc, vc, qd, kd, vd) = jnp.split(proj, split_points, axis=-1)
    heads = lambda t: t.reshape(B, T, -1, HEAD_DIM)
    logf = jax.nn.log_sigmoid(fa.astype(F32) + b_f.astype(F32))
    oa = fox_attention(heads(qa), heads(ka), heads(va), logf)
    ob = dsa_attention(rope(heads(qb), pos), rope(heads(kb), pos), heads(vb),
                       rope(qi.reshape(B, T, IDX_HEADS, IDX_DIM), pos),
                       rope(ki[:, :, None, :], pos)[:, :, 0], wi)
    oc = dilated_attention(rope(heads(qc), pos), rope(heads(kc), pos), heads(vc))
    od = moba_attention(rope(heads(qd), pos), rope(heads(kd), pos), heads(vd))
    o = jnp.concatenate([t.reshape(B, T, -1).astype(x.dtype) for t in (oa, ob, oc, od)], axis=-1)
    return jnp.einsum('bte,ed->btd', o, w_o)


def setup_inputs(seed: int = 0) -> dict:
    key = jax.random.key(seed)
    ks = jax.random.split(key, 11)
    nrm = jax.random.normal
    x = nrm(ks[0], (BATCH, SEQ, D_MODEL), F32)
    w_in = nrm(ks[1], (DEPTH, D_MODEL, D_IN), F32) * D_MODEL ** -0.5
    b_f = 3.0 + 0.5 * nrm(ks[2], (DEPTH, H_FOX), F32)
    w_o = nrm(ks[3], (DEPTH, D_MODEL, D_MODEL), F32) * (D_MODEL ** -0.5 * DEEPNORM_BETA)
    ln1_g = 1.0 + 0.02 * nrm(ks[4], (DEPTH, D_MODEL), F32)
    ln1_b = 0.02 * nrm(ks[5], (DEPTH, D_MODEL), F32)
    w_up = nrm(ks[6], (DEPTH, D_MODEL, D_FF), F32) * D_MODEL ** -0.5
    w_down = nrm(ks[7], (DEPTH, D_FF, D_MODEL), F32) * (D_FF ** -0.5 * DEEPNORM_BETA)
    ln2_g = 1.0 + 0.02 * nrm(ks[8], (DEPTH, D_MODEL), F32)
    ln2_b = 0.02 * nrm(ks[9], (DEPTH, D_MODEL), F32)
    return {"x": x, "w_in": w_in, "b_f": b_f, "w_o": w_o, "ln1_g": ln1_g, "ln1_b": ln1_b,
            "w_up": w_up, "w_down": w_down, "ln2_g": ln2_g, "ln2_b": ln2_b}


def reference(x, w_in, b_f, w_o, ln1_g, ln1_b, w_up, w_down, ln2_g, ln2_b):
    for l in range(DEPTH):
        x = layer_norm(DEEPNORM_ALPHA * x + hybrid_mixer(x, w_in[l], b_f[l], w_o[l]), ln1_g[l], ln1_b[l])
        h = jnp.square(jax.nn.relu(jnp.einsum('btd,df->btf', x, w_up[l])))
        x = layer_norm(DEEPNORM_ALPHA * x + jnp.einsum('btf,fd->btd', h, w_down[l]), ln2_g[l], ln2_b[l])
    return x
```

```python
import functools

import numpy as np
import jax
import jax.numpy as jnp
from jax import lax
from jax.experimental import pallas as pl
from jax.experimental.pallas import tpu as pltpu

F32 = jnp.float32
BF16 = jnp.bfloat16

HEAD_DIM = 64
GROUP_HEADS = 4
GROUP_W = GROUP_HEADS * HEAD_DIM
IDX_HEADS = 8
IDX_DIM = 32
DSA_TOPK = 256
DIL_PATTERNS = ((128, 1), (512, 4), (2048, 16))
MOBA_BLOCK = 256
MOBA_TOPK = 3
ROPE_THETA = 10000.0
LN_EPS = 1e-5
LANES = 128
TB = 256
NEG_INF = float("-inf")
INT_MIN = -2 ** 31
VMEM_LIMIT = 56 * 1024 * 1024

(CB_QA, CB_KA, CB_VA, CB_QB, CB_KB, CB_VB, CB_QC, CB_KC, CB_VC,
 CB_QD, CB_KD, CB_VD, CB_QI, CB_KI8) = range(14)
N_COLBLK = 14
ROPE64_BLOCKS = (CB_QB, CB_KB, CB_QC, CB_KC, CB_QD, CB_KD)
ROPE32_BLOCKS = (CB_QI, CB_KI8)


def _dot_nt(a, b):
    return lax.dot_general(a, b, (((1,), (1,)), ((), ())), preferred_element_type=F32)


def _layer_norm(y, g, b):
    mu = jnp.mean(y, axis=-1, keepdims=True)
    d = y - mu
    var = jnp.mean(d * d, axis=-1, keepdims=True)
    return d * lax.rsqrt(var + LN_EPS) * g + b


def _rope(acc, tab_ref, half):
    cos = tab_ref[:, 0:LANES]
    sin = tab_ref[:, LANES:2 * LANES]
    lane = lax.broadcasted_iota(jnp.int32, (1, LANES), 1)
    first = (lane & (2 * half - 1)) < half
    outs = []
    for c in range(GROUP_W // LANES):
        a = acc[:, c * LANES:(c + 1) * LANES]
        swapped = jnp.where(first, pltpu.roll(a, LANES - half, 1), pltpu.roll(a, half, 1))
        outs.append(a * cos + swapped * sin)
    return jnp.concatenate(outs, axis=1)


def _inproj_kernel(x_ref, w_ref, wm_ref, t64_ref, t32_ref, o_ref, misc_ref, xb_ref):
    j = pl.program_id(1)

    @pl.when(j == 0)
    def _():
        xb = x_ref[...].astype(BF16)
        xb_ref[...] = xb
        misc_ref[...] = jnp.dot(xb, wm_ref[...], preferred_element_type=F32)

    acc = jnp.dot(xb_ref[...], w_ref[...], preferred_element_type=F32)
    is64 = functools.reduce(jnp.logical_or, [j == c for c in ROPE64_BLOCKS])
    is32 = functools.reduce(jnp.logical_or, [j == c for c in ROPE32_BLOCKS])

    @pl.when(is64)
    def _():
        o_ref[...] = _rope(acc, t64_ref, HEAD_DIM // 2).astype(BF16)

    @pl.when(is32)
    def _():
        o_ref[...] = _rope(acc, t32_ref, IDX_DIM // 2).astype(BF16)

    @pl.when(jnp.logical_not(jnp.logical_or(is64, is32)))
    def _():
        o_ref[...] = acc.astype(BF16)


def _inproj(x2d, w_main, w_misc, tab64, tab32, seq):
    m, d = x2d.shape
    tm = min(512, seq)
    nt = seq // tm
    return pl.pallas_call(
        _inproj_kernel,
        grid=(m // tm, N_COLBLK),
        in_specs=[
            pl.BlockSpec((tm, d), lambda i, j: (i, 0)),
            pl.BlockSpec((d, GROUP_W), lambda i, j: (0, j)),
            pl.BlockSpec((d, LANES), lambda i, j: (0, 0)),
            pl.BlockSpec((tm, 2 * LANES), lambda i, j: (i % nt, 0)),
            pl.BlockSpec((tm, 2 * LANES), lambda i, j: (i % nt, 0)),
        ],
        out_specs=[
            pl.BlockSpec((tm, GROUP_W), lambda i, j: (i, j)),
            pl.BlockSpec((tm, LANES), lambda i, j: (i, 0)),
        ],
        out_shape=[
            jax.ShapeDtypeStruct((m, N_COLBLK * GROUP_W), BF16),
            jax.ShapeDtypeStruct((m, LANES), F32),
        ],
        scratch_shapes=[pltpu.VMEM((tm, d), BF16)],
        compiler_params=pltpu.CompilerParams(
            dimension_semantics=("parallel", "arbitrary"), vmem_limit_bytes=VMEM_LIMIT),
    )(x2d, w_main, w_misc, tab64, tab32)


def _gate_kernel(z_ref, b_ref, c_ref, *, chunks):
    z = z_ref[...] + b_ref[...]
    w = jnp.minimum(z, 0.0) - jnp.log1p(jnp.exp(-jnp.abs(z)))
    rows = w.shape[0]
    lane = lax.broadcasted_iota(jnp.int32, (rows, LANES), 1)
    s = 1
    while s < LANES:
        w = w + jnp.where(lane >= s, pltpu.roll(w, s, 1), 0.0)
        s *= 2
    tot = jnp.broadcast_to(w[:, LANES - 1:LANES], (rows, LANES))
    pos = lax.broadcasted_iota(jnp.int32, (rows, LANES), 0) & (chunks - 1)
    run = tot
    s = 1
    while s < chunks:
        run = run + jnp.where(pos >= s, pltpu.roll(run, s, 0), 0.0)
        s *= 2
    c_ref[...] = w + (run - tot)


def _gate_cumsum(z, bias_rows, chunks):
    b, rows, _ = z.shape
    return pl.pallas_call(
        functools.partial(_gate_kernel, chunks=chunks),
        grid=(b,),
        in_specs=[pl.BlockSpec((None, rows, LANES), lambda i: (i, 0, 0)),
                  pl.BlockSpec((rows, 1), lambda i: (0, 0))],
        out_specs=pl.BlockSpec((None, rows, LANES), lambda i: (i, 0, 0)),
        out_shape=jax.ShapeDtypeStruct(z.shape, F32),
        compiler_params=pltpu.CompilerParams(dimension_semantics=("parallel",)),
    )(z, bias_rows)


def _split_heads(q_ref):
    lo = lax.broadcasted_iota(jnp.int32, (1, LANES), 1) < HEAD_DIM
    qs = []
    for pr in range(GROUP_W // LANES):
        qp = q_ref[:, pr * LANES:(pr + 1) * LANES].astype(F32)
        qs.append(jnp.where(lo, qp, 0.0).astype(BF16))
        qs.append(jnp.where(lo, 0.0, qp).astype(BF16))
    return qs


def _online_softmax(st, m, l):
    m_new = jnp.maximum(m, jnp.max(st, axis=0, keepdims=True))
    m_safe = jnp.where(m_new == NEG_INF, 0.0, m_new)
    alpha = jnp.exp(m - m_safe)
    p = jnp.exp(st - m_safe)
    return p, m_new, alpha


def _init_state():
    m = tuple(jnp.full((1, TB), NEG_INF, F32) for _ in range(GROUP_HEADS))
    l = tuple(jnp.zeros((1, TB), F32) for _ in range(GROUP_HEADS))
    acc = tuple(jnp.zeros((HEAD_DIM, TB), F32) for _ in range(GROUP_HEADS))
    return m, l, acc


def _attend_block(qs, kblk, vt, state, mask_fn, weight=None):
    ms, ls, accs = state
    nm, nl, na = [], [], []
    for h in range(GROUP_HEADS):
        pr = h // 2
        st = _dot_nt(kblk[:, pr * LANES:(pr + 1) * LANES], qs[h])
        st = mask_fn(h, st)
        p, m_new, alpha = _online_softmax(st, ms[h], ls[h])
        if weight is not None:
            p = p * weight
        l_new = alpha * ls[h] + jnp.sum(p, axis=0, keepdims=True)
        pv = jnp.dot(vt[h * HEAD_DIM:(h + 1) * HEAD_DIM, :], p.astype(BF16),
                     preferred_element_type=F32)
        nm.append(m_new)
        nl.append(l_new)
        na.append(alpha * accs[h] + pv)
    return tuple(nm), tuple(nl), tuple(na)


def _finish(state, o_ref):
    _, ls, accs = state
    ot = jnp.concatenate([accs[h] / ls[h] for h in range(GROUP_HEADS)], axis=0)
    o_ref[...] = ot.T.astype(o_ref.dtype)


def _causal_mask():
    krow = lax.broadcasted_iota(jnp.int32, (TB, TB), 0)
    qcol = lax.broadcasted_iota(jnp.int32, (TB, TB), 1)
    return krow <= qcol


def _kblock(k_ref, j):
    return k_ref[pl.ds(pl.multiple_of(j * TB, TB), TB), :]


def _attn_specs(seq, colblks):
    nq = seq // TB
    specs = []
    for kind, cb in colblks:
        if kind == "q":
            specs.append(pl.BlockSpec((TB, GROUP_W), lambda b, i, cb=cb: (b * nq + i, cb)))
        else:
            specs.append(pl.BlockSpec((seq, GROUP_W), lambda b, i, cb=cb: (b, cb)))
    return specs


def _vt_spec(nkb):
    return pl.BlockSpec((None, nkb, GROUP_W, TB), lambda b, i: (b, 0, 0, 0))


def _o_spec(seq):
    nq = seq // TB
    return pl.BlockSpec((TB, GROUP_W), lambda b, i: (b * nq + i, 0))


_ATTN_PARAMS = dict(
    compiler_params=pltpu.CompilerParams(
        dimension_semantics=("parallel", "arbitrary"), vmem_limit_bytes=VMEM_LIMIT))


def _fox_kernel(q_ref, k_ref, vt_ref, c_ref, o_ref):
    i = pl.program_id(1)
    qs = _split_heads(q_ref)

    def step(j, state, diag):
        cb = c_ref[pl.ds(pl.multiple_of(j * TB, TB), TB), :]
        causal = _causal_mask() if diag else None

        def mask_fn(h, st):
            st = st - cb[:, h:h + 1]
            return jnp.where(causal, st, NEG_INF) if diag else st

        return _attend_block(qs, _kblock(k_ref, j), vt_ref[j], state, mask_fn)

    state = lax.fori_loop(0, i, lambda j, s: step(j, s, False), _init_state())
    _finish(step(i, state, True), o_ref)


def _fox(qkv, vt, ccol, batch, seq):
    nkb = seq // TB
    return pl.pallas_call(
        _fox_kernel,
        grid=(batch, seq // TB),
        in_specs=_attn_specs(seq, (("q", CB_QA), ("k", CB_KA))) + [
            _vt_spec(nkb),
            pl.BlockSpec((None, seq, GROUP_HEADS), lambda b, i: (b, 0, 0)),
        ],
        out_specs=_o_spec(seq),
        out_shape=jax.ShapeDtypeStruct((batch * seq, GROUP_W), BF16),
        **_ATTN_PARAMS,
    )(qkv, qkv, vt, ccol)


def _key_to_float(key_u):
    key = key_u ^ jnp.int32(INT_MIN)
    bits = jnp.where(key < 0, key ^ jnp.int32(0x7FFFFFFF), key)
    below_neg_inf = jnp.logical_and(key_u >= 0, key_u < jnp.int32(0x007FFFFF))
    return jnp.where(below_neg_inf, NEG_INF, lax.bitcast_convert_type(bits, F32))


def _dsa_kernel(q_ref, k_ref, vt_ref, qi_ref, k8_ref, w_ref, o_ref, strip_ref, qm_ref, *, topk, idx_bits):
    i = pl.program_id(1)
    nblk = i + 1

    qi = qi_ref[...].astype(F32)
    head_of_lane = lax.broadcasted_iota(jnp.int32, (1, GROUP_W), 1) >> 5
    for h in range(IDX_HEADS):
        qm_ref[h * TB:(h + 1) * TB, :] = jnp.where(head_of_lane == h, qi, 0.0).astype(BF16)

    wrow = w_ref[...]
    causal = _causal_mask()

    def score_block(j, diag):
        zt = _dot_nt(_kblock(k8_ref, j), qm_ref[...])
        sc = jnp.zeros((TB, TB), F32)
        for h in range(IDX_HEADS):
            sc = sc + wrow[h:h + 1, :] * jnp.maximum(zt[:, h * TB:(h + 1) * TB], 0.0)
        if diag:
            sc = jnp.where(causal, sc, NEG_INF)
        strip_ref[j] = sc

    def score_body(j, carry):
        score_block(j, False)
        return carry

    lax.fori_loop(0, i, score_body, 0)
    score_block(i, True)

    def count(pred):
        def body(j, c):
            return c + jnp.sum(jnp.where(pred(j, strip_ref[j]), 1.0, 0.0), axis=0, keepdims=True)
        return lax.fori_loop(0, nblk, body, jnp.zeros((1, TB), F32))

    def bit_body(it, prefix):
        cand = prefix | jnp.left_shift(jnp.int32(1), 31 - it)
        t = _key_to_float(cand)
        cnt = count(lambda j, blk: blk >= t)
        return jnp.where(cnt >= topk, cand, prefix)

    prefix = lax.fori_loop(0, 32, bit_body, jnp.zeros((1, TB), jnp.int32))
    thr = _key_to_float(prefix)
    cnt_ge = count(lambda j, blk: blk >= thr)

    @pl.when(jnp.max(cnt_ge) > topk)
    def _():
        need = topk - count(lambda j, blk: blk > thr)
        krow = lax.broadcasted_iota(jnp.int32, (TB, TB), 0)

        def idx_body(it, lim):
            cand = lim | jnp.left_shift(jnp.int32(1), idx_bits - 1 - it)
            cnt = count(lambda j, blk: jnp.logical_and(blk == thr, krow + j * TB < cand))
            return jnp.where(cnt <= need, cand, lim)

        lim = lax.fori_loop(0, idx_bits, idx_body, jnp.zeros((1, TB), jnp.int32))

        def drop_body(j, carry):
            blk = strip_ref[j]
            drop = jnp.logical_and(blk == thr, krow + j * TB >= lim)
            strip_ref[j] = jnp.where(drop, NEG_INF, blk)
            return carry

        lax.fori_loop(0, nblk, drop_body, 0)

    qs = _split_heads(q_ref)

    def step(j, state, diag):
        blk = strip_ref[j]
        sel = blk >= thr
        if diag:
            sel = jnp.logical_and(sel, blk > NEG_INF)
        return _attend_block(qs, _kblock(k_ref, j), vt_ref[j], state,
                             lambda h, st: jnp.where(sel, st, NEG_INF))

    state = lax.fori_loop(0, i, lambda j, s: step(j, s, False), _init_state())
    _finish(step(i, state, True), o_ref)


def _dsa(qkv, vt, wrow, batch, seq):
    nkb = seq // TB
    topk = min(DSA_TOPK, seq // 4)
    kern = functools.partial(_dsa_kernel, topk=float(topk), idx_bits=int(seq).bit_length())
    return pl.pallas_call(
        kern,
        grid=(batch, seq // TB),
        in_specs=_attn_specs(seq, (("q", CB_QB), ("k", CB_KB))) + [_vt_spec(nkb)]
        + _attn_specs(seq, (("q", CB_QI), ("k", CB_KI8)))
        + [pl.BlockSpec((None, IDX_HEADS, TB), lambda b, i: (b, 0, i))],
        out_specs=_o_spec(seq),
        out_shape=jax.ShapeDtypeStruct((batch * seq, GROUP_W), BF16),
        scratch_shapes=[pltpu.VMEM((nkb, TB, TB), F32), pltpu.VMEM((IDX_HEADS * TB, GROUP_W), BF16)],
        **_ATTN_PARAMS,
    )(qkv, qkv, vt, qkv, qkv, wrow)


DIL_SPAN = max(w for w, _ in DIL_PATTERNS) // TB + 1


def _dilated_multiplicity():
    k = np.arange(TB)[:, None]
    q = np.arange(TB)[None, :]
    out = np.zeros((DIL_SPAN, TB, TB), np.float32)
    for delta in range(DIL_SPAN):
        d = q - k + TB * delta
        for w, r in DIL_PATTERNS:
            out[delta] += ((d >= 0) & (d % r == 0) & (d <= w)).astype(np.float32)
    return out


def _dil_kernel(q_ref, k_ref, vt_ref, mult_ref, o_ref):
    i = pl.program_id(1)
    qs = _split_heads(q_ref)

    def step(delta, state):
        j = i - delta
        mult = mult_ref[delta]
        keep = mult > 0.0
        return _attend_block(qs, _kblock(k_ref, j), vt_ref[j], state,
                             lambda h, st: jnp.where(keep, st, NEG_INF), weight=mult)

    state = lax.fori_loop(0, jnp.minimum(i + 1, DIL_SPAN), step, _init_state())
    _finish(state, o_ref)


def _dilated(qkv, vt, mult, batch, seq):
    nkb = seq // TB
    return pl.pallas_call(
        _dil_kernel,
        grid=(batch, seq // TB),
        in_specs=_attn_specs(seq, (("q", CB_QC), ("k", CB_KC))) + [
            _vt_spec(nkb),
            pl.BlockSpec((DIL_SPAN, TB, TB), lambda b, i: (0, 0, 0)),
        ],
        out_specs=_o_spec(seq),
        out_shape=jax.ShapeDtypeStruct((batch * seq, GROUP_W), BF16),
        **_ATTN_PARAMS,
    )(qkv, qkv, vt, mult)


def _moba_kernel(q_ref, k_ref, vt_ref, o_ref, kmean_ref, sel_ref, *, nkb, nkb_pad, topn):
    i = pl.program_id(1)

    @pl.when(i == 0)
    def _():
        kmean_ref[...] = jnp.zeros_like(kmean_ref)
        for n in range(nkb):
            kb = k_ref[n * TB:(n + 1) * TB, :].astype(F32)
            kmean_ref[n:n + 1, :] = jnp.mean(kb, axis=0, keepdims=True)

    qs = _split_heads(q_ref)
    blk_id = lax.broadcasted_iota(jnp.int32, (nkb_pad, TB), 0)
    blk_f = blk_id.astype(F32)
    km = kmean_ref[...].astype(BF16)
    for h in range(GROUP_HEADS):
        pr = h // 2
        g = _dot_nt(km[:, pr * LANES:(pr + 1) * LANES], qs[h])
        g = jnp.where(blk_id < i, g, NEG_INF)
        sel = jnp.zeros((nkb_pad, TB), F32)
        for _ in range(topn):
            mx = jnp.max(g, axis=0, keepdims=True)
            is_max = jnp.logical_and(g == mx, mx > NEG_INF)
            first = jnp.min(jnp.where(is_max, blk_f, float(nkb_pad)), axis=0, keepdims=True)
            pick = blk_f == first
            sel = jnp.where(pick, 1.0, sel)
            g = jnp.where(pick, NEG_INF, g)
        sel_ref[h] = sel

    def step(j, state):
        def mask_fn(h, st):
            return jnp.where(sel_ref[h, pl.ds(j, 1), :] > 0.5, st, NEG_INF)
        return _attend_block(qs, _kblock(k_ref, j), vt_ref[j], state, mask_fn)

    state = lax.fori_loop(0, i, step, _init_state())
    causal = _causal_mask()
    state = _attend_block(qs, _kblock(k_ref, i), vt_ref[i], state,
                          lambda h, st: jnp.where(causal, st, NEG_INF))
    _finish(state, o_ref)


def _moba(qkv, vt, batch, seq):
    nkb = seq // TB
    nkb_pad = -(-nkb // 8) * 8
    kern = functools.partial(_moba_kernel, nkb=nkb, nkb_pad=nkb_pad, topn=min(MOBA_TOPK, nkb))
    return pl.pallas_call(
        kern,
        grid=(batch, seq // TB),
        in_specs=_attn_specs(seq, (("q", CB_QD), ("k", CB_KD))) + [_vt_spec(nkb)],
        out_specs=_o_spec(seq),
        out_shape=jax.ShapeDtypeStruct((batch * seq, GROUP_W), BF16),
        scratch_shapes=[pltpu.VMEM((nkb_pad, GROUP_W), F32),
                        pltpu.VMEM((GROUP_HEADS, nkb_pad, TB), F32)],
        compiler_params=pltpu.CompilerParams(
            dimension_semantics=("arbitrary", "arbitrary"), vmem_limit_bytes=VMEM_LIMIT),
    )(qkv, qkv, vt)


def _outproj_kernel(oa_ref, ob_ref, oc_ref, od_ref, w_ref, x_ref, g_ref, b_ref, y_ref, *, alpha):
    acc = alpha * x_ref[...]
    for gi, o_ref in enumerate((oa_ref, ob_ref, oc_ref, od_ref)):
        acc = acc + jnp.dot(o_ref[...], w_ref[gi * GROUP_W:(gi + 1) * GROUP_W, :],
                            preferred_element_type=F32)
    y_ref[...] = _layer_norm(acc, g_ref[...], b_ref[...])


def _outproj(os4, w_o, x2d, g, b, alpha):
    m, d = x2d.shape
    tm = min(512, m)
    row = lambda i: (i, 0)
    fixed = lambda i: (0, 0)
    return pl.pallas_call(
        functools.partial(_outproj_kernel, alpha=alpha),
        grid=(m // tm,),
        in_specs=[pl.BlockSpec((tm, GROUP_W), row)] * 4 + [
            pl.BlockSpec((d, d), fixed), pl.BlockSpec((tm, d), row),
            pl.BlockSpec((1, d), fixed), pl.BlockSpec((1, d), fixed)],
        out_specs=pl.BlockSpec((tm, d), row),
        out_shape=jax.ShapeDtypeStruct((m, d), F32),
        compiler_params=pltpu.CompilerParams(
            dimension_semantics=("parallel",), vmem_limit_bytes=VMEM_LIMIT),
    )(*os4, w_o, x2d, g, b)


def _mlp_kernel(x_ref, wu_ref, wd_ref, g_ref, b_ref, y_ref, xb_ref, acc_ref, *, alpha):
    f = pl.program_id(1)

    @pl.when(f == 0)
    def _():
        x = x_ref[...]
        xb_ref[...] = x.astype(BF16)
        acc_ref[...] = alpha * x

    h = jnp.dot(xb_ref[...], wu_ref[...], preferred_element_type=F32)
    h = jnp.square(jnp.maximum(h, 0.0))
    acc_ref[...] += jnp.dot(h.astype(BF16), wd_ref[...], preferred_element_type=F32)

    @pl.when(f == pl.num_programs(1) - 1)
    def _():
        y_ref[...] = _layer_norm(acc_ref[...], g_ref[...], b_ref[...])


def _mlp(x2d, w_up, w_down, g, b, alpha):
    m, d = x2d.shape
    dff = w_up.shape[1]
    tm = min(1024, m)
    tf = min(512, dff)
    return pl.pallas_call(
        functools.partial(_mlp_kernel, alpha=alpha),
        grid=(m // tm, dff // tf),
        in_specs=[
            pl.BlockSpec((tm, d), lambda i, f: (i, 0)),
            pl.BlockSpec((d, tf), lambda i, f: (0, f)),
            pl.BlockSpec((tf, d), lambda i, f: (f, 0)),
            pl.BlockSpec((1, d), lambda i, f: (0, 0)),
            pl.BlockSpec((1, d), lambda i, f: (0, 0)),
        ],
        out_specs=pl.BlockSpec((tm, d), lambda i, f: (i, 0)),
        out_shape=jax.ShapeDtypeStruct((m, d), F32),
        scratch_shapes=[pltpu.VMEM((tm, d), BF16), pltpu.VMEM((tm, d), F32)],
        compiler_params=pltpu.CompilerParams(
            dimension_semantics=("parallel", "arbitrary"), vmem_limit_bytes=VMEM_LIMIT),
    )(x2d, w_up, w_down, g, b)


def _column_plan():
    hq = GROUP_W
    sizes = (hq, hq, hq, GROUP_HEADS,
             hq, hq, hq, IDX_HEADS * IDX_DIM, IDX_DIM, IDX_HEADS,
             hq, hq, hq, hq, hq, hq)
    offs = np.concatenate([[0], np.cumsum(sizes)])
    (qa, ka, va, fa, qb, kb, vb, qi, ki, wi, qc, kc, vc, qd, kd, vd) = [
        np.arange(offs[n], offs[n + 1]) for n in range(len(sizes))]
    main = [qa, ka, va, qb, kb, vb, qc, kc, vc, qd, kd, vd, qi] + [ki] * IDX_HEADS
    scale = np.ones(N_COLBLK * GROUP_W, np.float32)
    for cb in (CB_QA, CB_QB, CB_QC, CB_QD):
        scale[cb * GROUP_W:(cb + 1) * GROUP_W] = HEAD_DIM ** -0.5
    return np.concatenate(main), scale, np.concatenate([fa, wi])


def _rope_table(seq, dim):
    half = dim // 2
    lane = np.arange(LANES)
    inv = ROPE_THETA ** (-(jnp.arange(half, dtype=F32)) / half)
    ang = jnp.arange(seq, dtype=F32)[:, None] * inv[None, :]
    idx = lane % half
    sign = np.where((lane % dim) < half, -1.0, 1.0).astype(np.float32)
    return jnp.concatenate([jnp.cos(ang)[:, idx], jnp.sin(ang)[:, idx] * sign[None, :]], axis=1)


def _vt_blocks(qkv, cb, batch, seq):
    v = qkv[:, cb * GROUP_W:(cb + 1) * GROUP_W].reshape(batch, seq // TB, TB, GROUP_W)
    return v.transpose(0, 1, 3, 2)


def kernel(x, w_in, b_f, w_o, ln1_g, ln1_b, w_up, w_down, ln2_g, ln2_b):
    batch, seq, d = x.shape
    depth = w_in.shape[0]
    assert seq % TB == 0 and d == 4 * GROUP_W
    chunks = seq // LANES
    assert chunks & (chunks - 1) == 0
    alpha = (2.0 * depth) ** 0.25

    main_cols, main_scale, misc_cols = _column_plan()
    w_main = (w_in[:, :, main_cols] * main_scale).astype(BF16)
    w_misc = jnp.pad(w_in[:, :, misc_cols], ((0, 0), (0, 0), (0, LANES - len(misc_cols)))).astype(BF16)
    w_o16, w_up16, w_down16 = w_o.astype(BF16), w_up.astype(BF16), w_down.astype(BF16)
    tab64 = _rope_table(seq, HEAD_DIM)
    tab32 = _rope_table(seq, IDX_DIM)
    mult = jnp.asarray(_dilated_multiplicity())

    x2d = x.reshape(batch * seq, d)
    for l in range(depth):
        qkv, misc = _inproj(x2d, w_main[l], w_misc[l], tab64, tab32, seq)
        misc3 = misc.reshape(batch, seq, LANES)
        z = misc3[:, :, 0:GROUP_HEADS].transpose(0, 2, 1).reshape(batch, GROUP_HEADS * chunks, LANES)
        bias_rows = jnp.repeat(b_f[l].astype(F32), chunks)[:, None]
        c = _gate_cumsum(z, bias_rows, chunks)
        ccol = c.reshape(batch, GROUP_HEADS, seq).transpose(0, 2, 1)
        wrow = misc3[:, :, GROUP_HEADS:GROUP_HEADS + IDX_HEADS].transpose(0, 2, 1)

        oa = _fox(qkv, _vt_blocks(qkv, CB_VA, batch, seq), ccol, batch, seq)
        ob = _dsa(qkv, _vt_blocks(qkv, CB_VB, batch, seq), wrow, batch, seq)
        oc = _dilated(qkv, _vt_blocks(qkv, CB_VC, batch, seq), mult, batch, seq)
        od = _moba(qkv, _vt_blocks(qkv, CB_VD, batch, seq), batch, seq)

        x2d = _outproj((oa, ob, oc, od), w_o16[l], x2d, ln1_g[l][None, :], ln1_b[l][None, :], alpha)
        x2d = _mlp(x2d, w_up16[l], w_down16[l], ln2_g[l][None, :], ln2_b[l][None, :], alpha)
    return x2d.reshape(batch, seq, d)
```

```python
import functools

import numpy as np
import jax
import jax.numpy as jnp
from jax import lax
from jax.experimental import pallas as pl
from jax.experimental.pallas import tpu as pltpu

F32 = jnp.float32
BF16 = jnp.bfloat16

HEAD_DIM = 64
GROUP_HEADS = 4
GROUP_W = GROUP_HEADS * HEAD_DIM
IDX_HEADS = 8
IDX_DIM = 32
DSA_TOPK = 256
DIL_PATTERNS = ((128, 1), (512, 4), (2048, 16))
MOBA_BLOCK = 256
MOBA_TOPK = 3
ROPE_THETA = 10000.0
LN_EPS = 1e-5
LANES = 128
TB = 256
NEG_INF = float("-inf")
VMEM_LIMIT = 56 * 1024 * 1024

(CB_QA, CB_KA, CB_VA, CB_QB, CB_KB, CB_VB, CB_QC, CB_KC, CB_VC,
 CB_QD, CB_KD, CB_VD, CB_QI, CB_KI8) = range(14)
N_COLBLK = 14
ROPE64_BLOCKS = (CB_QB, CB_KB, CB_QC, CB_KC, CB_QD, CB_KD)
ROPE32_BLOCKS = (CB_QI, CB_KI8)


def _dot_nt(a, b):
    return lax.dot_general(a, b, (((1,), (1,)), ((), ())), preferred_element_type=F32)


def _layer_norm(y, g, b):
    mu = jnp.mean(y, axis=-1, keepdims=True)
    d = y - mu
    var = jnp.mean(d * d, axis=-1, keepdims=True)
    return d * lax.rsqrt(var + LN_EPS) * g + b


def _rope(acc, tab_ref, half):
    cos = tab_ref[:, 0:LANES]
    sin = tab_ref[:, LANES:2 * LANES]
    lane = lax.broadcasted_iota(jnp.int32, (1, LANES), 1)
    first = (lane & (2 * half - 1)) < half
    outs = []
    for c in range(GROUP_W // LANES):
        a = acc[:, c * LANES:(c + 1) * LANES]
        swapped = jnp.where(first, pltpu.roll(a, LANES - half, 1), pltpu.roll(a, half, 1))
        outs.append(a * cos + swapped * sin)
    return jnp.concatenate(outs, axis=1)


def _inproj_kernel(x_ref, w_ref, wm_ref, t64_ref, t32_ref, o_ref, misc_ref, xb_ref):
    j = pl.program_id(1)

    @pl.when(j == 0)
    def _():
        xb = x_ref[...].astype(BF16)
        xb_ref[...] = xb
        misc_ref[...] = jnp.dot(xb, wm_ref[...], preferred_element_type=F32)

    acc = jnp.dot(xb_ref[...], w_ref[...], preferred_element_type=F32)
    is64 = functools.reduce(jnp.logical_or, [j == c for c in ROPE64_BLOCKS])
    is32 = functools.reduce(jnp.logical_or, [j == c for c in ROPE32_BLOCKS])

    @pl.when(is64)
    def _():
        o_ref[...] = _rope(acc, t64_ref, HEAD_DIM // 2).astype(BF16)

    @pl.when(is32)
    def _():
        o_ref[...] = _rope(acc, t32_ref, IDX_DIM // 2).astype(BF16)

    @pl.when(jnp.logical_not(jnp.logical_or(is64, is32)))
    def _():
        o_ref[...] = acc.astype(BF16)


def _inproj(x2d, w_main, w_misc, tab64, tab32, seq):
    m, d = x2d.shape
    tm = min(512, seq)
    nt = seq // tm
    return pl.pallas_call(
        _inproj_kernel,
        grid=(m // tm, N_COLBLK),
        in_specs=[
            pl.BlockSpec((tm, d), lambda i, j: (i, 0)),
            pl.BlockSpec((d, GROUP_W), lambda i, j: (0, j)),
            pl.BlockSpec((d, LANES), lambda i, j: (0, 0)),
            pl.BlockSpec((tm, 2 * LANES), lambda i, j: (i % nt, 0)),
            pl.BlockSpec((tm, 2 * LANES), lambda i, j: (i % nt, 0)),
        ],
        out_specs=[
            pl.BlockSpec((tm, GROUP_W), lambda i, j: (i, j)),
            pl.BlockSpec((tm, LANES), lambda i, j: (i, 0)),
        ],
        out_shape=[
            jax.ShapeDtypeStruct((m, N_COLBLK * GROUP_W), BF16),
            jax.ShapeDtypeStruct((m, LANES), F32),
        ],
        scratch_shapes=[pltpu.VMEM((tm, d), BF16)],
        compiler_params=pltpu.CompilerParams(
            dimension_semantics=("parallel", "arbitrary"), vmem_limit_bytes=VMEM_LIMIT),
    )(x2d, w_main, w_misc, tab64, tab32)


def _gate_kernel(z_ref, b_ref, c_ref, *, chunks):
    z = z_ref[...] + b_ref[...]
    w = jnp.minimum(z, 0.0) - jnp.log1p(jnp.exp(-jnp.abs(z)))
    rows = w.shape[0]
    lane = lax.broadcasted_iota(jnp.int32, (rows, LANES), 1)
    s = 1
    while s < LANES:
        w = w + jnp.where(lane >= s, pltpu.roll(w, s, 1), 0.0)
        s *= 2
    tot = jnp.broadcast_to(w[:, LANES - 1:LANES], (rows, LANES))
    pos = lax.broadcasted_iota(jnp.int32, (rows, LANES), 0) & (chunks - 1)
    run = tot
    s = 1
    while s < chunks:
        run = run + jnp.where(pos >= s, pltpu.roll(run, s, 0), 0.0)
        s *= 2
    c_ref[...] = w + (run - tot)


def _gate_cumsum(z, bias_rows, chunks):
    b, rows, _ = z.shape
    return pl.pallas_call(
        functools.partial(_gate_kernel, chunks=chunks),
        grid=(b,),
        in_specs=[pl.BlockSpec((None, rows, LANES), lambda i: (i, 0, 0)),
                  pl.BlockSpec((rows, 1), lambda i: (0, 0))],
        out_specs=pl.BlockSpec((None, rows, LANES), lambda i: (i, 0, 0)),
        out_shape=jax.ShapeDtypeStruct(z.shape, F32),
        compiler_params=pltpu.CompilerParams(dimension_semantics=("parallel",)),
    )(z, bias_rows)


QS_SHAPE = (GROUP_HEADS * TB, GROUP_W)


def _stack_heads(q_ref, qs_ref):
    q = q_ref[...].astype(F32)
    head_of_lane = lax.broadcasted_iota(jnp.int32, (1, GROUP_W), 1) >> 6
    for h in range(GROUP_HEADS):
        qs_ref[h * TB:(h + 1) * TB, :] = jnp.where(head_of_lane == h, q, 0.0).astype(BF16)


def _head(x, h):
    return x[:, h * TB:(h + 1) * TB]


def _tile_heads(x):
    return jnp.concatenate([x] * GROUP_HEADS, axis=1)


def _init_state():
    m = jnp.full((1, GROUP_HEADS * TB), NEG_INF, F32)
    l = jnp.zeros((1, GROUP_HEADS * TB), F32)
    acc = tuple(jnp.zeros((HEAD_DIM, TB), F32) for _ in range(GROUP_HEADS))
    return m, l, acc


def _scores(k_ref, qs_ref, j):
    return _dot_nt(k_ref[pl.ds(pl.multiple_of(j * TB, TB), TB), :], qs_ref[...])


def _consume(st, vt, state, mask_fn, weight=None):
    m, l, accs = state
    st = mask_fn(st)
    m_new = jnp.maximum(m, jnp.max(st, axis=0, keepdims=True))
    m_safe = jnp.where(m_new == NEG_INF, 0.0, m_new)
    alpha = jnp.exp(m - m_safe)
    p = jnp.exp(st - m_safe)
    if weight is not None:
        p = p * weight
    l_new = alpha * l + jnp.sum(p, axis=0, keepdims=True)
    pb = p.astype(BF16)
    new_accs = tuple(
        _head(alpha, h) * accs[h]
        + jnp.dot(vt[h * HEAD_DIM:(h + 1) * HEAD_DIM, :], _head(pb, h), preferred_element_type=F32)
        for h in range(GROUP_HEADS))
    return m_new, l_new, new_accs


def _finish(state, o_ref):
    _, l, accs = state
    ot = jnp.concatenate([accs[h] / _head(l, h) for h in range(GROUP_HEADS)], axis=0)
    o_ref[...] = ot.T.astype(o_ref.dtype)


def _attn_scratch():
    return [
        pltpu.VMEM(QS_SHAPE, BF16),
        pltpu.VMEM((2, TB, GROUP_HEADS * TB), F32),
        pltpu.VMEM((2, GROUP_HEADS * TB), F32),
        pltpu.VMEM((GROUP_W, TB), F32),
    ]


def _attend_sequence(k_ref, vt_ref, o_ref, scratch, n_steps, block_of, mask_of, weight_of=None):
    qs_ref, st_ref, ml_ref, acc_ref = scratch

    def issue(s, slot):
        st_ref[slot] = _scores(k_ref, qs_ref, block_of(s))

    def consume(s, slot, last):
        state = (ml_ref[0:1, :], ml_ref[1:2, :],
                 tuple(acc_ref[h * HEAD_DIM:(h + 1) * HEAD_DIM, :] for h in range(GROUP_HEADS)))
        weight = None if weight_of is None else weight_of(s)
        m, l, accs = _consume(st_ref[slot], vt_ref[block_of(s)], state, mask_of(s, last), weight)
        ml_ref[0:1, :] = m
        ml_ref[1:2, :] = l
        for h in range(GROUP_HEADS):
            acc_ref[h * HEAD_DIM:(h + 1) * HEAD_DIM, :] = accs[h]

    ml_ref[0:1, :] = jnp.full((1, GROUP_HEADS * TB), NEG_INF, F32)
    ml_ref[1:2, :] = jnp.zeros((1, GROUP_HEADS * TB), F32)
    acc_ref[...] = jnp.zeros_like(acc_ref)
    issue(0, 0)

    def pair(pp, carry):
        s = 2 * pp
        issue(s + 1, 1)
        consume(s, 0, False)
        issue(s + 2, 0)
        consume(s + 1, 1, False)
        return carry

    lax.fori_loop(0, n_steps >> 1, pair, 0)

    @pl.when((n_steps & 1) == 1)
    def _():
        issue(n_steps, 1)
        consume(n_steps - 1, 0, False)
        consume(n_steps, 1, True)

    @pl.when((n_steps & 1) == 0)
    def _():
        consume(n_steps, 0, True)

    ot = jnp.concatenate(
        [acc_ref[h * HEAD_DIM:(h + 1) * HEAD_DIM, :] / _head(ml_ref[1:2, :], h)
         for h in range(GROUP_HEADS)], axis=0)
    o_ref[...] = ot.T.astype(o_ref.dtype)


def _causal_mask(heads=GROUP_HEADS):
    krow = lax.broadcasted_iota(jnp.int32, (TB, heads * TB), 0)
    qcol = lax.broadcasted_iota(jnp.int32, (TB, heads * TB), 1) & (TB - 1)
    return krow <= qcol


def _kblock(k_ref, j):
    return k_ref[pl.ds(pl.multiple_of(j * TB, TB), TB), :]


def _attn_specs(seq, colblks):
    nq = seq // TB
    specs = []
    for kind, cb in colblks:
        if kind == "q":
            specs.append(pl.BlockSpec((TB, GROUP_W), lambda b, i, cb=cb: (b * nq + i, cb)))
        else:
            specs.append(pl.BlockSpec((seq, GROUP_W), lambda b, i, cb=cb: (b, cb)))
    return specs


def _vt_spec(nkb):
    return pl.BlockSpec((None, nkb, GROUP_W, TB), lambda b, i: (b, 0, 0, 0))


def _o_spec(seq):
    nq = seq // TB
    return pl.BlockSpec((TB, GROUP_W), lambda b, i: (b * nq + i, 0))


_ATTN_PARAMS = dict(
    compiler_params=pltpu.CompilerParams(
        dimension_semantics=("parallel", "arbitrary"), vmem_limit_bytes=VMEM_LIMIT))


def _fox_kernel(q_ref, k_ref, vt_ref, c_ref, o_ref, *scratch):
    i = pl.program_id(1)
    _stack_heads(q_ref, scratch[0])

    def mask_of(j, diag):
        def mask_fn(st):
            cb = c_ref[pl.ds(pl.multiple_of(j * TB, TB), TB), :]
            st = jnp.concatenate(
                [_head(st, h) - cb[:, h:h + 1] for h in range(GROUP_HEADS)], axis=1)
            return jnp.where(_causal_mask(), st, NEG_INF) if diag else st
        return mask_fn

    _attend_sequence(k_ref, vt_ref, o_ref, scratch, i, lambda s: s, mask_of)


def _fox(qkv, vt, ccol, batch, seq):
    nkb = seq // TB
    return pl.pallas_call(
        _fox_kernel,
        grid=(batch, seq // TB),
        in_specs=_attn_specs(seq, (("q", CB_QA), ("k", CB_KA))) + [
            _vt_spec(nkb),
            pl.BlockSpec((None, seq, GROUP_HEADS), lambda b, i: (b, 0, 0)),
        ],
        out_specs=_o_spec(seq),
        out_shape=jax.ShapeDtypeStruct((batch * seq, GROUP_W), BF16),
        scratch_shapes=_attn_scratch(),
        **_ATTN_PARAMS,
    )(qkv, qkv, vt, ccol)


def _float_to_key(f):
    bits = lax.bitcast_convert_type(f, jnp.int32)
    return jnp.where(bits < 0, bits ^ jnp.int32(0x7FFFFFFF), bits)


def _key_to_float(key):
    return lax.bitcast_convert_type(jnp.where(key < 0, key ^ jnp.int32(0x7FFFFFFF), key), F32)


def _dsa_kernel(q_ref, k_ref, vt_ref, qi_ref, k8_ref, w_ref, o_ref, strip_ref, qm_ref, *scratch,
                topk, idx_bits):
    i = pl.program_id(1)
    nblk = i + 1

    qi = qi_ref[...].astype(F32)
    head_of_lane = lax.broadcasted_iota(jnp.int32, (1, GROUP_W), 1) >> 5
    for h in range(IDX_HEADS):
        qm_ref[h * TB:(h + 1) * TB, :] = jnp.where(head_of_lane == h, qi, 0.0).astype(BF16)

    wrow = w_ref[...]

    def score_block(j, diag, carry):
        rmax, rmin = carry
        zt = _dot_nt(_kblock(k8_ref, j), qm_ref[...])
        sc = jnp.zeros((TB, TB), F32)
        for h in range(IDX_HEADS):
            sc = sc + wrow[h:h + 1, :] * jnp.maximum(zt[:, h * TB:(h + 1) * TB], 0.0)
        sc_lo = jnp.where(_causal_mask(1), sc, NEG_INF) if diag else sc
        sc_hi = jnp.where(_causal_mask(1), sc, -NEG_INF) if diag else sc
        strip_ref[j] = sc_lo
        return (jnp.maximum(rmax, jnp.max(sc_lo, axis=0, keepdims=True)),
                jnp.minimum(rmin, jnp.min(sc_hi, axis=0, keepdims=True)))

    carry = (jnp.full((1, TB), NEG_INF, F32), jnp.full((1, TB), -NEG_INF, F32))
    carry = lax.fori_loop(0, i, lambda j, c: score_block(j, False, c), carry)
    rmax, rmin = score_block(i, True, carry)

    def count(pred):
        def body(j, c):
            return c + jnp.sum(jnp.where(pred(j, strip_ref[j]), 1.0, 0.0), axis=0, keepdims=True)
        return lax.fori_loop(0, nblk, body, jnp.zeros((1, TB), F32))

    n_causal = (i * TB + 1 + lax.broadcasted_iota(jnp.int32, (1, TB), 1)).astype(F32)
    trivial = n_causal <= topk
    log_target = float(np.log(topk + 0.5))

    def is_done(klo, khi, c_lo):
        return jnp.logical_or(trivial, jnp.logical_or(c_lo == topk, klo + 1 >= khi))

    def n_active(klo, khi, c_lo):
        return jnp.max(jnp.where(is_done(klo, khi, c_lo), 0.0, 1.0))

    def search_body(c):
        _, phase, klo, khi, c_lo, c_hi = c
        done = is_done(klo, khi, c_lo)
        flo, fhi = _key_to_float(klo), _key_to_float(khi)
        frac = (jnp.log(c_lo) - log_target) / (jnp.log(c_lo) - jnp.log(jnp.maximum(c_hi, 0.5)))
        probe = _float_to_key(flo + (fhi - flo) * frac)
        mid = (klo >> 1) + (khi >> 1) + (klo & khi & 1)
        probe = jnp.where(phase == 2, mid, probe)
        probe = jnp.minimum(jnp.maximum(probe, klo + 1), khi - 1)
        t = _key_to_float(probe)
        cnt = count(lambda j, blk: blk >= t)
        up = jnp.logical_and(cnt >= topk, jnp.logical_not(done))
        dn = jnp.logical_and(cnt < topk, jnp.logical_not(done))
        klo, c_lo = jnp.where(up, probe, klo), jnp.where(up, cnt, c_lo)
        khi, c_hi = jnp.where(dn, probe, khi), jnp.where(dn, cnt, c_hi)
        return (n_active(klo, khi, c_lo), jnp.where(phase == 2, 0, phase + 1),
                klo, khi, c_lo, c_hi)

    klo0 = _float_to_key(rmin)
    khi0 = _float_to_key(rmax) + 1
    init = (n_active(klo0, khi0, n_causal), jnp.int32(0), klo0, khi0,
            n_causal, jnp.zeros((1, TB), F32))
    _, _, klo, _, c_lo, _ = lax.while_loop(lambda c: c[0] > 0.0, search_body, init)
    thr = jnp.where(trivial, NEG_INF, _key_to_float(klo))
    cnt_ge = jnp.where(trivial, (nblk * TB).astype(F32), c_lo)

    @pl.when(jnp.max(cnt_ge) > topk)
    def _():
        need = topk - count(lambda j, blk: blk > thr)
        krow = lax.broadcasted_iota(jnp.int32, (TB, TB), 0)

        def idx_body(it, lim):
            cand = lim | jnp.left_shift(jnp.int32(1), idx_bits - 1 - it)
            cnt = count(lambda j, blk: jnp.logical_and(blk == thr, krow + j * TB < cand))
            return jnp.where(cnt <= need, cand, lim)

        lim = lax.fori_loop(0, idx_bits, idx_body, jnp.zeros((1, TB), jnp.int32))

        def drop_body(j, carry):
            blk = strip_ref[j]
            drop = jnp.logical_and(blk == thr, krow + j * TB >= lim)
            strip_ref[j] = jnp.where(drop, NEG_INF, blk)
            return carry

        lax.fori_loop(0, nblk, drop_body, 0)

    _stack_heads(q_ref, scratch[0])

    def mask_of(j, diag):
        def mask_fn(st):
            blk = strip_ref[j]
            sel = blk >= thr
            if diag:
                sel = jnp.logical_and(sel, blk > NEG_INF)
            return jnp.concatenate(
                [jnp.where(sel, _head(st, h), NEG_INF) for h in range(GROUP_HEADS)], axis=1)
        return mask_fn

    _attend_sequence(k_ref, vt_ref, o_ref, scratch, i, lambda s: s, mask_of)


def _dsa(qkv, vt, wrow, batch, seq):
    nkb = seq // TB
    topk = min(DSA_TOPK, seq // 4)
    kern = functools.partial(_dsa_kernel, topk=float(topk), idx_bits=int(seq).bit_length())
    return pl.pallas_call(
        kern,
        grid=(batch, seq // TB),
        in_specs=_attn_specs(seq, (("q", CB_QB), ("k", CB_KB))) + [_vt_spec(nkb)]
        + _attn_specs(seq, (("q", CB_QI), ("k", CB_KI8)))
        + [pl.BlockSpec((None, IDX_HEADS, TB), lambda b, i: (b, 0, i))],
        out_specs=_o_spec(seq),
        out_shape=jax.ShapeDtypeStruct((batch * seq, GROUP_W), BF16),
        scratch_shapes=[pltpu.VMEM((nkb, TB, TB), F32),
                        pltpu.VMEM((IDX_HEADS * TB, GROUP_W), BF16)] + _attn_scratch(),
        **_ATTN_PARAMS,
    )(qkv, qkv, vt, qkv, qkv, wrow)


DIL_SPAN = max(w for w, _ in DIL_PATTERNS) // TB + 1


def _dilated_multiplicity():
    k = np.arange(TB)[:, None]
    q = np.arange(TB)[None, :]
    out = np.zeros((DIL_SPAN, TB, TB), np.float32)
    for delta in range(DIL_SPAN):
        d = q - k + TB * delta
        for w, r in DIL_PATTERNS:
            out[delta] += ((d >= 0) & (d % r == 0) & (d <= w)).astype(np.float32)
    return out


def _dil_kernel(q_ref, k_ref, vt_ref, mult_ref, o_ref, *scratch):
    i = pl.program_id(1)
    _stack_heads(q_ref, scratch[0])

    n = jnp.minimum(i, DIL_SPAN - 1)

    def mult_of(s):
        return _tile_heads(mult_ref[n - s])

    def mask_of(s, last):
        return lambda st: jnp.where(mult_of(s) > 0.0, st, NEG_INF)

    _attend_sequence(k_ref, vt_ref, o_ref, scratch, n, lambda s: i - n + s, mask_of, mult_of)


def _dilated(qkv, vt, mult, batch, seq):
    nkb = seq // TB
    return pl.pallas_call(
        _dil_kernel,
        grid=(batch, seq // TB),
        in_specs=_attn_specs(seq, (("q", CB_QC), ("k", CB_KC))) + [
            _vt_spec(nkb),
            pl.BlockSpec((DIL_SPAN, TB, TB), lambda b, i: (0, 0, 0)),
        ],
        out_specs=_o_spec(seq),
        out_shape=jax.ShapeDtypeStruct((batch * seq, GROUP_W), BF16),
        scratch_shapes=_attn_scratch(),
        **_ATTN_PARAMS,
    )(qkv, qkv, vt, mult)


def _moba_kernel(q_ref, k_ref, vt_ref, o_ref, kmean_ref, sel_ref, *scratch, nkb, nkb_pad, topn):
    i = pl.program_id(1)

    @pl.when(i == 0)
    def _():
        kmean_ref[...] = jnp.zeros_like(kmean_ref)
        for n in range(nkb):
            kb = k_ref[n * TB:(n + 1) * TB, :].astype(F32)
            kmean_ref[n:n + 1, :] = jnp.mean(kb, axis=0, keepdims=True)

    _stack_heads(q_ref, scratch[0])
    blk_id = lax.broadcasted_iota(jnp.int32, (nkb_pad, GROUP_HEADS * TB), 0)
    blk_f = blk_id.astype(F32)
    g = _dot_nt(kmean_ref[...].astype(BF16), scratch[0][...])
    g = jnp.where(blk_id < i, g, NEG_INF)
    sel = jnp.zeros_like(g)
    for _ in range(topn):
        mx = jnp.max(g, axis=0, keepdims=True)
        is_max = jnp.logical_and(g == mx, mx > NEG_INF)
        first = jnp.min(jnp.where(is_max, blk_f, float(nkb_pad)), axis=0, keepdims=True)
        pick = blk_f == first
        sel = jnp.where(pick, 1.0, sel)
        g = jnp.where(pick, NEG_INF, g)
    sel_ref[...] = sel

    def mask_of(j, diag):
        if diag:
            return lambda st: jnp.where(_causal_mask(), st, NEG_INF)
        return lambda st: jnp.where(sel_ref[pl.ds(j, 1), :] > 0.5, st, NEG_INF)

    _attend_sequence(k_ref, vt_ref, o_ref, scratch, i, lambda s: s, mask_of)


def _moba(qkv, vt, batch, seq):
    nkb = seq // TB
    nkb_pad = -(-nkb // 8) * 8
    kern = functools.partial(_moba_kernel, nkb=nkb, nkb_pad=nkb_pad, topn=min(MOBA_TOPK, nkb))
    return pl.pallas_call(
        kern,
        grid=(batch, seq // TB),
        in_specs=_attn_specs(seq, (("q", CB_QD), ("k", CB_KD))) + [_vt_spec(nkb)],
        out_specs=_o_spec(seq),
        out_shape=jax.ShapeDtypeStruct((batch * seq, GROUP_W), BF16),
        scratch_shapes=[pltpu.VMEM((nkb_pad, GROUP_W), F32),
                        pltpu.VMEM((nkb_pad, GROUP_HEADS * TB), F32)] + _attn_scratch(),
        compiler_params=pltpu.CompilerParams(
            dimension_semantics=("arbitrary", "arbitrary"), vmem_limit_bytes=VMEM_LIMIT),
    )(qkv, qkv, vt)


def _outproj_kernel(oa_ref, ob_ref, oc_ref, od_ref, w_ref, x_ref, g_ref, b_ref, y_ref, *, alpha):
    acc = alpha * x_ref[...]
    for gi, o_ref in enumerate((oa_ref, ob_ref, oc_ref, od_ref)):
        acc = acc + jnp.dot(o_ref[...], w_ref[gi * GROUP_W:(gi + 1) * GROUP_W, :],
                            preferred_element_type=F32)
    y_ref[...] = _layer_norm(acc, g_ref[...], b_ref[...])


def _outproj(os4, w_o, x2d, g, b, alpha):
    m, d = x2d.shape
    tm = min(512, m)
    row = lambda i: (i, 0)
    fixed = lambda i: (0, 0)
    return pl.pallas_call(
        functools.partial(_outproj_kernel, alpha=alpha),
        grid=(m // tm,),
        in_specs=[pl.BlockSpec((tm, GROUP_W), row)] * 4 + [
            pl.BlockSpec((d, d), fixed), pl.BlockSpec((tm, d), row),
            pl.BlockSpec((1, d), fixed), pl.BlockSpec((1, d), fixed)],
        out_specs=pl.BlockSpec((tm, d), row),
        out_shape=jax.ShapeDtypeStruct((m, d), F32),
        compiler_params=pltpu.CompilerParams(
            dimension_semantics=("parallel",), vmem_limit_bytes=VMEM_LIMIT),
    )(*os4, w_o, x2d, g, b)


def _mlp_kernel(x_ref, wu_ref, wd_ref, g_ref, b_ref, y_ref, xb_ref, acc_ref, *, alpha):
    f = pl.program_id(1)

    @pl.when(f == 0)
    def _():
        x = x_ref[...]
        xb_ref[...] = x.astype(BF16)
        acc_ref[...] = alpha * x

    h = jnp.dot(xb_ref[...], wu_ref[...], preferred_element_type=F32)
    h = jnp.square(jnp.maximum(h, 0.0))
    acc_ref[...] += jnp.dot(h.astype(BF16), wd_ref[...], preferred_element_type=F32)

    @pl.when(f == pl.num_programs(1) - 1)
    def _():
        y_ref[...] = _layer_norm(acc_ref[...], g_ref[...], b_ref[...])


def _mlp(x2d, w_up, w_down, g, b, alpha):
    m, d = x2d.shape
    dff = w_up.shape[1]
    tm = min(1024, m)
    tf = min(512, dff)
    return pl.pallas_call(
        functools.partial(_mlp_kernel, alpha=alpha),
        grid=(m // tm, dff // tf),
        in_specs=[
            pl.BlockSpec((tm, d), lambda i, f: (i, 0)),
            pl.BlockSpec((d, tf), lambda i, f: (0, f)),
            pl.BlockSpec((tf, d), lambda i, f: (f, 0)),
            pl.BlockSpec((1, d), lambda i, f: (0, 0)),
            pl.BlockSpec((1, d), lambda i, f: (0, 0)),
        ],
        out_specs=pl.BlockSpec((tm, d), lambda i, f: (i, 0)),
        out_shape=jax.ShapeDtypeStruct((m, d), F32),
        scratch_shapes=[pltpu.VMEM((tm, d), BF16), pltpu.VMEM((tm, d), F32)],
        compiler_params=pltpu.CompilerParams(
            dimension_semantics=("parallel", "arbitrary"), vmem_limit_bytes=VMEM_LIMIT),
    )(x2d, w_up, w_down, g, b)


def _column_plan():
    hq = GROUP_W
    sizes = (hq, hq, hq, GROUP_HEADS,
             hq, hq, hq, IDX_HEADS * IDX_DIM, IDX_DIM, IDX_HEADS,
             hq, hq, hq, hq, hq, hq)
    offs = np.concatenate([[0], np.cumsum(sizes)])
    (qa, ka, va, fa, qb, kb, vb, qi, ki, wi, qc, kc, vc, qd, kd, vd) = [
        np.arange(offs[n], offs[n + 1]) for n in range(len(sizes))]
    main = [qa, ka, va, qb, kb, vb, qc, kc, vc, qd, kd, vd, qi] + [ki] * IDX_HEADS
    scale = np.ones(N_COLBLK * GROUP_W, np.float32)
    for cb in (CB_QA, CB_QB, CB_QC, CB_QD):
        scale[cb * GROUP_W:(cb + 1) * GROUP_W] = HEAD_DIM ** -0.5
    return np.concatenate(main), scale, np.concatenate([fa, wi])


def _rope_table(seq, dim):
    half = dim // 2
    lane = np.arange(LANES)
    inv = ROPE_THETA ** (-(jnp.arange(half, dtype=F32)) / half)
    ang = jnp.arange(seq, dtype=F32)[:, None] * inv[None, :]
    idx = lane % half
    sign = np.where((lane % dim) < half, -1.0, 1.0).astype(np.float32)
    return jnp.concatenate([jnp.cos(ang)[:, idx], jnp.sin(ang)[:, idx] * sign[None, :]], axis=1)


def _vt_blocks(qkv, cb, batch, seq):
    v = qkv[:, cb * GROUP_W:(cb + 1) * GROUP_W].reshape(batch, seq // TB, TB, GROUP_W)
    return v.transpose(0, 1, 3, 2)


def kernel(x, w_in, b_f, w_o, ln1_g, ln1_b, w_up, w_down, ln2_g, ln2_b):
    batch, seq, d = x.shape
    depth = w_in.shape[0]
    assert seq % TB == 0 and d == 4 * GROUP_W
    chunks = seq // LANES
    assert chunks & (chunks - 1) == 0
    alpha = (2.0 * depth) ** 0.25

    main_cols, main_scale, misc_cols = _column_plan()
    w_main = (w_in[:, :, main_cols] * main_scale).astype(BF16)
    w_misc = jnp.pad(w_in[:, :, misc_cols], ((0, 0), (0, 0), (0, LANES - len(misc_cols)))).astype(BF16)
    w_o16, w_up16, w_down16 = w_o.astype(BF16), w_up.astype(BF16), w_down.astype(BF16)
    tab64 = _rope_table(seq, HEAD_DIM)
    tab32 = _rope_table(seq, IDX_DIM)
    mult = jnp.asarray(_dilated_multiplicity())

    x2d = x.reshape(batch * seq, d)
    for l in range(depth):
        qkv, misc = _inproj(x2d, w_main[l], w_misc[l], tab64, tab32, seq)
        misc3 = misc.reshape(batch, seq, LANES)
        z = misc3[:, :, 0:GROUP_HEADS].transpose(0, 2, 1).reshape(batch, GROUP_HEADS * chunks, LANES)
        bias_rows = jnp.repeat(b_f[l].astype(F32), chunks)[:, None]
        c = _gate_cumsum(z, bias_rows, chunks)
        ccol = c.reshape(batch, GROUP_HEADS, seq).transpose(0, 2, 1)
        wrow = misc3[:, :, GROUP_HEADS:GROUP_HEADS + IDX_HEADS].transpose(0, 2, 1)

        oa = _fox(qkv, _vt_blocks(qkv, CB_VA, batch, seq), ccol, batch, seq)
        ob = _dsa(qkv, _vt_blocks(qkv, CB_VB, batch, seq), wrow, batch, seq)
        oc = _dilated(qkv, _vt_blocks(qkv, CB_VC, batch, seq), mult, batch, seq)
        od = _moba(qkv, _vt_blocks(qkv, CB_VD, batch, seq), batch, seq)

        x2d = _outproj((oa, ob, oc, od), w_o16[l], x2d, ln1_g[l][None, :], ln1_b[l][None, :], alpha)
        x2d = _mlp(x2d, w_up16[l], w_down16[l], ln2_g[l][None, :], ln2_b[l][None, :], alpha)
    return x2d.reshape(batch, seq, d)
```

```python
import functools
import math

import numpy as np
import jax
import jax.numpy as jnp
from jax import lax
from jax.experimental import pallas as pl
from jax.experimental.pallas import tpu as pltpu

F32 = jnp.float32
BF16 = jnp.bfloat16

HEAD_DIM = 64
GROUP_HEADS = 4
GROUP_W = GROUP_HEADS * HEAD_DIM
IDX_HEADS = 8
IDX_DIM = 32
DSA_TOPK = 256
DIL_PATTERNS = ((128, 1), (512, 4), (2048, 16))
MOBA_BLOCK = 256
MOBA_TOPK = 3
ROPE_THETA = 10000.0
LN_EPS = 1e-5
LOG2E = math.log2(math.e)
LANES = 128
BF16_ROWS = 16
TB = 256
VROWS = HEAD_DIM + BF16_ROWS
NEG_INF = float("-inf")
VMEM_LIMIT = 56 * 1024 * 1024

(CB_QA, CB_KA, CB_VA, CB_VB, CB_VC, CB_VD,
 CB_QB, CB_KB, CB_QC, CB_KC, CB_QD, CB_KD,
 CB_QI, CB_KI8) = range(14)
N_COLBLK = 14
PROJ_TN = 2 * GROUP_W
PLAIN_STEPS, ROPE64_STEPS = 3, 3


def _dot_nt(a, b):
    return lax.dot_general(a, b, (((1,), (1,)), ((), ())), preferred_element_type=F32)


def _layer_norm(y, g, b):
    mu = jnp.mean(y, axis=-1, keepdims=True)
    d = y - mu
    var = jnp.mean(d * d, axis=-1, keepdims=True)
    return d * lax.rsqrt(var + LN_EPS) * g + b


def _tree_sum(parts):
    while len(parts) > 1:
        parts = [parts[n] + parts[n + 1] for n in range(0, len(parts), 2)]
    return parts[0]


def _rope(acc, tab_ref, half):
    cos = tab_ref[:, 0:LANES]
    sin = tab_ref[:, LANES:2 * LANES]
    lane = lax.broadcasted_iota(jnp.int32, (1, LANES), 1)
    first = (lane & (2 * half - 1)) < half
    outs = []
    for c in range(acc.shape[1] // LANES):
        a = acc[:, c * LANES:(c + 1) * LANES]
        swapped = jnp.where(first, pltpu.roll(a, LANES - half, 1), pltpu.roll(a, half, 1))
        outs.append(a * cos + swapped * sin)
    return jnp.concatenate(outs, axis=1)


def _inproj_kernel(x_ref, w_ref, wm_ref, t64_ref, t32_ref, o_ref, misc_ref, xb_ref):
    j = pl.program_id(1)

    @pl.when(j == 0)
    def _():
        xb = x_ref[...].astype(BF16)
        xb_ref[...] = xb
        misc_ref[...] = jnp.dot(xb, wm_ref[...], preferred_element_type=F32)

    acc = jnp.dot(xb_ref[...], w_ref[...], preferred_element_type=F32)

    @pl.when(j < PLAIN_STEPS)
    def _():
        o_ref[...] = acc.astype(BF16)

    @pl.when(jnp.logical_and(j >= PLAIN_STEPS, j < PLAIN_STEPS + ROPE64_STEPS))
    def _():
        o_ref[...] = _rope(acc, t64_ref, HEAD_DIM // 2).astype(BF16)

    @pl.when(j == PLAIN_STEPS + ROPE64_STEPS)
    def _():
        o_ref[...] = _rope(acc, t32_ref, IDX_DIM // 2).astype(BF16)


def _inproj(x2d, w_main, w_misc, tab64, tab32, seq):
    m, d = x2d.shape
    tm = min(1024, seq)
    nt = seq // tm
    return pl.pallas_call(
        _inproj_kernel,
        grid=(m // tm, N_COLBLK * GROUP_W // PROJ_TN),
        in_specs=[
            pl.BlockSpec((tm, d), lambda i, j: (i, 0)),
            pl.BlockSpec((d, PROJ_TN), lambda i, j: (0, j)),
            pl.BlockSpec((d, LANES), lambda i, j: (0, 0)),
            pl.BlockSpec((tm, 2 * LANES), lambda i, j: (i % nt, 0)),
            pl.BlockSpec((tm, 2 * LANES), lambda i, j: (i % nt, 0)),
        ],
        out_specs=[
            pl.BlockSpec((tm, PROJ_TN), lambda i, j: (i, j)),
            pl.BlockSpec((tm, LANES), lambda i, j: (i, 0)),
        ],
        out_shape=[
            jax.ShapeDtypeStruct((m, N_COLBLK * GROUP_W), BF16),
            jax.ShapeDtypeStruct((m, LANES), F32),
        ],
        scratch_shapes=[pltpu.VMEM((tm, d), BF16)],
        compiler_params=pltpu.CompilerParams(
            dimension_semantics=("parallel", "arbitrary"), vmem_limit_bytes=VMEM_LIMIT),
    )(x2d, w_main, w_misc, tab64, tab32)


def _gate_kernel(z_ref, b_ref, o_ref, *, chunks):
    z = z_ref[...] + b_ref[...]
    w = jnp.minimum(z, 0.0) - jnp.log1p(jnp.exp(-jnp.abs(z)))
    rows = w.shape[0]
    lane = lax.broadcasted_iota(jnp.int32, (rows, LANES), 1)
    s = 1
    while s < LANES:
        w = w + jnp.where(lane >= s, pltpu.roll(w, s, 1), 0.0)
        s *= 2
    tot = jnp.broadcast_to(w[:, LANES - 1:LANES], (rows, LANES))
    pos = lax.broadcasted_iota(jnp.int32, (rows, LANES), 0) & (chunks - 1)
    run = tot
    s = 1
    while s < chunks:
        run = run + jnp.where(pos >= s, pltpu.roll(run, s, 0), 0.0)
        s *= 2
    bias = -LOG2E * (w + (run - tot))
    hi = bias.astype(BF16).astype(F32)
    mid = (bias - hi).astype(BF16).astype(F32)
    lo = (bias - hi - mid).astype(BF16).astype(F32)
    o_ref[0] = hi
    o_ref[1] = mid
    o_ref[2] = lo


def _gate_bias(z, bias_rows, chunks):
    b, rows, _ = z.shape
    return pl.pallas_call(
        functools.partial(_gate_kernel, chunks=chunks),
        grid=(b,),
        in_specs=[pl.BlockSpec((None, rows, LANES), lambda i: (i, 0, 0)),
                  pl.BlockSpec((rows, 1), lambda i: (0, 0))],
        out_specs=pl.BlockSpec((None, 3, rows, LANES), lambda i: (i, 0, 0, 0)),
        out_shape=jax.ShapeDtypeStruct((b, 3, rows, LANES), F32),
        compiler_params=pltpu.CompilerParams(dimension_semantics=("parallel",)),
    )(z, bias_rows)


HEADS_W = GROUP_HEADS * TB


def _stack_heads(q_ref, qs_ref):
    q = q_ref[...].astype(F32)
    head_of_lane = lax.broadcasted_iota(jnp.int32, (1, GROUP_W), 1) >> 6
    for h in range(GROUP_HEADS):
        qs_ref[h * TB:(h + 1) * TB, 0:GROUP_W] = jnp.where(head_of_lane == h, q, 0.0).astype(BF16)


def _head(x, h):
    return x[:, h * TB:(h + 1) * TB]


def _tile_heads(x):
    return jnp.concatenate([x] * GROUP_HEADS, axis=1)


def _kblock(k_ref, j):
    return k_ref[pl.ds(pl.multiple_of(j * TB, TB), TB), :]


def _causal_mask(heads=GROUP_HEADS):
    krow = lax.broadcasted_iota(jnp.int32, (TB, heads * TB), 0)
    qcol = lax.broadcasted_iota(jnp.int32, (TB, heads * TB), 1) & (TB - 1)
    return krow <= qcol


def _consume(st, vt, m, accs, mask_fn=None, weight=None, keep=None):
    if mask_fn is not None:
        st = mask_fn(st)
    blk_max = jnp.max(st, axis=0, keepdims=True)
    m_new = jnp.maximum(m, blk_max)
    if keep is not None:
        m_new = jnp.where(keep > 0.5, m_new, m)
    m_safe = jnp.where(m_new == NEG_INF, 0.0, m_new)
    alpha = jnp.exp2(m - m_safe)
    p = jnp.exp2(st - m_safe)
    if weight is not None:
        p = p * weight
    pb = p.astype(BF16)
    new_accs = []
    for h in range(GROUP_HEADS):
        pv = jnp.dot(vt[h * VROWS:(h + 1) * VROWS, :], _head(pb, h), preferred_element_type=F32)
        upd = _head(alpha, h) * accs[h] + pv
        new_accs.append(upd if keep is None else jnp.where(_head(keep, h) > 0.5, upd, accs[h]))
    return m_new, new_accs


def _attn_scratch(qs_width=GROUP_W):
    return [
        pltpu.VMEM((HEADS_W, qs_width), BF16),
        pltpu.VMEM((2, TB, HEADS_W), F32),
        pltpu.VMEM((1, HEADS_W), F32),
        pltpu.VMEM((GROUP_HEADS * VROWS, TB), F32),
    ]


def _attend_sequence(scores_of, vt_ref, o_ref, scratch, n_steps, block_of, consume_args):
    _, st_ref, m_ref, acc_ref = scratch

    def issue(s, slot):
        st_ref[slot] = scores_of(block_of(s))

    def consume(s, slot, last):
        accs = [acc_ref[h * VROWS:(h + 1) * VROWS, :] for h in range(GROUP_HEADS)]
        m, accs = _consume(st_ref[slot], vt_ref[block_of(s)], m_ref[...], accs,
                           **consume_args(s, last))
        m_ref[...] = m
        for h in range(GROUP_HEADS):
            acc_ref[h * VROWS:(h + 1) * VROWS, :] = accs[h]

    m_ref[...] = jnp.full((1, HEADS_W), NEG_INF, F32)
    acc_ref[...] = jnp.zeros_like(acc_ref)
    issue(0, 0)

    def pair(pp, carry):
        s = 2 * pp
        issue(s + 1, 1)
        consume(s, 0, False)
        issue(s + 2, 0)
        consume(s + 1, 1, False)
        return carry

    lax.fori_loop(0, n_steps >> 1, pair, 0)

    @pl.when((n_steps & 1) == 1)
    def _():
        issue(n_steps, 1)
        consume(n_steps - 1, 0, False)
        consume(n_steps, 1, True)

    @pl.when((n_steps & 1) == 0)
    def _():
        consume(n_steps, 0, True)

    ot = jnp.concatenate(
        [acc_ref[h * VROWS:h * VROWS + HEAD_DIM, :] / acc_ref[h * VROWS + HEAD_DIM:h * VROWS + HEAD_DIM + 1, :]
         for h in range(GROUP_HEADS)], axis=0)
    o_ref[...] = ot.T.astype(o_ref.dtype)


def _attn_specs(seq, colblks):
    nq = seq // TB
    specs = []
    for kind, cb in colblks:
        if kind == "q":
            specs.append(pl.BlockSpec((TB, GROUP_W), lambda b, i, cb=cb: (b * nq + i, cb)))
        else:
            specs.append(pl.BlockSpec((seq, GROUP_W), lambda b, i, cb=cb: (b, cb)))
    return specs


def _vt_spec(nkb):
    return pl.BlockSpec((None, nkb, GROUP_HEADS * VROWS, TB), lambda b, i: (b, 0, 0, 0))


def _o_spec(seq):
    nq = seq // TB
    return pl.BlockSpec((TB, GROUP_W), lambda b, i: (b * nq + i, 0))


_ATTN_PARAMS = dict(
    compiler_params=pltpu.CompilerParams(
        dimension_semantics=("parallel", "arbitrary"), vmem_limit_bytes=VMEM_LIMIT))


BIAS_TERMS = 3


def _fox_kernel(q_ref, k_ref, cb_ref, vt_ref, o_ref, *scratch):
    i = pl.program_id(1)
    qs_ref = scratch[0]
    _stack_heads(q_ref, qs_ref)
    lane = lax.broadcasted_iota(jnp.int32, (TB, LANES), 1)
    for h in range(GROUP_HEADS):
        pick = jnp.logical_and(lane >= BIAS_TERMS * h, lane < BIAS_TERMS * (h + 1))
        qs_ref[h * TB:(h + 1) * TB, GROUP_W:GROUP_W + LANES] = jnp.where(pick, 1.0, 0.0).astype(BF16)

    def scores_of(j):
        return _dot_nt(jnp.concatenate([_kblock(k_ref, j), _kblock(cb_ref, j)], axis=1), qs_ref[...])

    def consume_args(j, diag):
        if diag:
            return dict(mask_fn=lambda st: jnp.where(_causal_mask(), st, NEG_INF))
        return {}

    _attend_sequence(scores_of, vt_ref, o_ref, scratch, i, lambda s: s, consume_args)


def _fox(qkv, cbias, vt, batch, seq):
    nkb = seq // TB
    return pl.pallas_call(
        _fox_kernel,
        grid=(batch, seq // TB),
        in_specs=_attn_specs(seq, (("q", CB_QA), ("k", CB_KA))) + [
            pl.BlockSpec((seq, LANES), lambda b, i: (b, 0)),
            _vt_spec(nkb),
        ],
        out_specs=_o_spec(seq),
        out_shape=jax.ShapeDtypeStruct((batch * seq, GROUP_W), BF16),
        scratch_shapes=_attn_scratch(GROUP_W + LANES),
        **_ATTN_PARAMS,
    )(qkv, qkv, cbias, vt)


def _key_to_float(key_u):
    key = key_u ^ jnp.int32(-2 ** 31)
    bits = jnp.where(key < 0, key ^ jnp.int32(0x7FFFFFFF), key)
    below_neg_inf = jnp.logical_and(key_u >= 0, key_u < jnp.int32(0x007FFFFF))
    return jnp.where(below_neg_inf, NEG_INF, lax.bitcast_convert_type(bits, F32))


def _dsa_kernel(q_ref, k_ref, vt_ref, qi_ref, k8_ref, w_ref, o_ref, strip_ref, qm_ref, *scratch,
                topk, idx_bits):
    i = pl.program_id(1)
    nblk = i + 1

    qi = qi_ref[...].astype(F32)
    head_of_lane = lax.broadcasted_iota(jnp.int32, (1, GROUP_W), 1) >> 5
    for h in range(IDX_HEADS):
        qm_ref[h * TB:(h + 1) * TB, :] = jnp.where(head_of_lane == h, qi, 0.0).astype(BF16)

    wrow = w_ref[...]

    def score_block(j, diag):
        zt = _dot_nt(_kblock(k8_ref, j), qm_ref[...])
        sc = _tree_sum([wrow[h:h + 1, :] * jnp.maximum(zt[:, h * TB:(h + 1) * TB], 0.0)
                        for h in range(IDX_HEADS)])
        strip_ref[j] = jnp.where(_causal_mask(1), sc, NEG_INF) if diag else sc

    def score_body(j, carry):
        score_block(j, False)
        return carry

    lax.fori_loop(0, i, score_body, 0)
    score_block(i, True)

    def count(pred):
        def body(j, acc):
            hit = jnp.where(pred(j, strip_ref[j]), 1.0, 0.0)
            return acc + _tree_sum([hit[r:r + 8, :] for r in range(0, TB, 8)])
        acc = lax.fori_loop(0, nblk, body, jnp.zeros((8, TB), F32))
        return jnp.sum(acc, axis=0, keepdims=True)

    def unsettled(cnt_at_prefix):
        return jnp.max(jnp.where(cnt_at_prefix == topk, 0.0, 1.0))

    def bit_body(c):
        it, _, prefix, cnt_at_prefix = c
        cand = prefix | jnp.left_shift(jnp.int32(1), 31 - it)
        t = _key_to_float(cand)
        cnt = count(lambda j, blk: blk >= t)
        take = cnt >= topk
        prefix = jnp.where(take, cand, prefix)
        cnt_at_prefix = jnp.where(take, cnt, cnt_at_prefix)
        return it + 1, unsettled(cnt_at_prefix), prefix, cnt_at_prefix

    n_all = jnp.full((1, TB), 1.0, F32) * (nblk * TB).astype(F32)
    init = (jnp.int32(0), unsettled(n_all), jnp.zeros((1, TB), jnp.int32), n_all)
    _, _, prefix, cnt_ge = lax.while_loop(
        lambda c: jnp.logical_and(c[0] < 32, c[1] > 0.0), bit_body, init)
    thr = _key_to_float(prefix)

    @pl.when(jnp.max(cnt_ge) > topk)
    def _():
        need = topk - count(lambda j, blk: blk > thr)
        krow = lax.broadcasted_iota(jnp.int32, (TB, TB), 0)

        def idx_body(it, lim):
            cand = lim | jnp.left_shift(jnp.int32(1), idx_bits - 1 - it)
            cnt = count(lambda j, blk: jnp.logical_and(blk == thr, krow + j * TB < cand))
            return jnp.where(cnt <= need, cand, lim)

        lim = lax.fori_loop(0, idx_bits, idx_body, jnp.zeros((1, TB), jnp.int32))

        def drop_body(j, carry):
            blk = strip_ref[j]
            drop = jnp.logical_and(blk == thr, krow + j * TB >= lim)
            strip_ref[j] = jnp.where(drop, NEG_INF, blk)
            return carry

        lax.fori_loop(0, nblk, drop_body, 0)

    qs_ref = scratch[0]
    _stack_heads(q_ref, qs_ref)

    def consume_args(j, diag):
        def mask_fn(st):
            blk = strip_ref[j]
            sel = blk >= thr
            if diag:
                sel = jnp.logical_and(sel, blk > NEG_INF)
            return jnp.concatenate(
                [jnp.where(sel, _head(st, h), NEG_INF) for h in range(GROUP_HEADS)], axis=1)
        return dict(mask_fn=mask_fn)

    _attend_sequence(lambda j: _dot_nt(_kblock(k_ref, j), qs_ref[...]), vt_ref, o_ref, scratch,
                     i, lambda s: s, consume_args)


def _dsa(qkv, vt, wrow, batch, seq):
    nkb = seq // TB
    topk = min(DSA_TOPK, seq // 4)
    kern = functools.partial(_dsa_kernel, topk=float(topk), idx_bits=int(seq).bit_length())
    return pl.pallas_call(
        kern,
        grid=(batch, seq // TB),
        in_specs=_attn_specs(seq, (("q", CB_QB), ("k", CB_KB))) + [_vt_spec(nkb)]
        + _attn_specs(seq, (("q", CB_QI), ("k", CB_KI8)))
        + [pl.BlockSpec((None, IDX_HEADS, TB), lambda b, i: (b, 0, i))],
        out_specs=_o_spec(seq),
        out_shape=jax.ShapeDtypeStruct((batch * seq, GROUP_W), BF16),
        scratch_shapes=[pltpu.VMEM((nkb, TB, TB), F32),
                        pltpu.VMEM((IDX_HEADS * TB, GROUP_W), BF16)] + _attn_scratch(),
        **_ATTN_PARAMS,
    )(qkv, qkv, vt, qkv, qkv, wrow)


DIL_SPAN = max(w for w, _ in DIL_PATTERNS) // TB + 1


def _dilated_multiplicity():
    k = np.arange(TB)[:, None]
    q = np.arange(TB)[None, :]
    out = np.zeros((DIL_SPAN, TB, TB), np.float32)
    for delta in range(DIL_SPAN):
        d = q - k + TB * delta
        for w, r in DIL_PATTERNS:
            out[delta] += ((d >= 0) & (d % r == 0) & (d <= w)).astype(np.float32)
    return out


def _dil_kernel(q_ref, k_ref, vt_ref, mult_ref, o_ref, *scratch):
    i = pl.program_id(1)
    qs_ref = scratch[0]
    _stack_heads(q_ref, qs_ref)
    n = jnp.minimum(i, DIL_SPAN - 1)

    def consume_args(s, last):
        mult = _tile_heads(mult_ref[n - s])
        return dict(mask_fn=lambda st: jnp.where(mult > 0.0, st, NEG_INF), weight=mult)

    _attend_sequence(lambda j: _dot_nt(_kblock(k_ref, j), qs_ref[...]), vt_ref, o_ref, scratch,
                     n, lambda s: i - n + s, consume_args)


def _dilated(qkv, vt, mult, batch, seq):
    nkb = seq // TB
    return pl.pallas_call(
        _dil_kernel,
        grid=(batch, seq // TB),
        in_specs=_attn_specs(seq, (("q", CB_QC), ("k", CB_KC))) + [
            _vt_spec(nkb),
            pl.BlockSpec((DIL_SPAN, TB, TB), lambda b, i: (0, 0, 0)),
        ],
        out_specs=_o_spec(seq),
        out_shape=jax.ShapeDtypeStruct((batch * seq, GROUP_W), BF16),
        scratch_shapes=_attn_scratch(),
        **_ATTN_PARAMS,
    )(qkv, qkv, vt, mult)


def _moba_kernel(q_ref, k_ref, vt_ref, o_ref, kmean_ref, sel_ref, *scratch, nkb, nkb_pad, topn):
    i = pl.program_id(1)

    @pl.when(i == 0)
    def _():
        kmean_ref[...] = jnp.zeros_like(kmean_ref)
        for n in range(nkb):
            kb = k_ref[n * TB:(n + 1) * TB, :].astype(F32)
            kmean_ref[n:n + 1, :] = jnp.mean(kb, axis=0, keepdims=True)

    qs_ref = scratch[0]
    _stack_heads(q_ref, qs_ref)
    blk_id = lax.broadcasted_iota(jnp.int32, (nkb_pad, HEADS_W), 0)
    blk_f = blk_id.astype(F32)
    g = _dot_nt(kmean_ref[...].astype(BF16), qs_ref[...])
    g = jnp.where(blk_id < i, g, NEG_INF)
    sel = jnp.zeros_like(g)
    for _ in range(topn):
        mx = jnp.max(g, axis=0, keepdims=True)
        is_max = jnp.logical_and(g == mx, mx > NEG_INF)
        first = jnp.min(jnp.where(is_max, blk_f, float(nkb_pad)), axis=0, keepdims=True)
        pick = blk_f == first
        sel = jnp.where(pick, 1.0, sel)
        g = jnp.where(pick, NEG_INF, g)
    sel_ref[...] = sel

    def consume_args(j, diag):
        if diag:
            return dict(mask_fn=lambda st: jnp.where(_causal_mask(), st, NEG_INF))
        return dict(keep=sel_ref[pl.ds(j, 1), :])

    _attend_sequence(lambda j: _dot_nt(_kblock(k_ref, j), qs_ref[...]), vt_ref, o_ref, scratch,
                     i, lambda s: s, consume_args)


def _moba(qkv, vt, batch, seq):
    nkb = seq // TB
    nkb_pad = -(-nkb // 8) * 8
    kern = functools.partial(_moba_kernel, nkb=nkb, nkb_pad=nkb_pad, topn=min(MOBA_TOPK, nkb))
    return pl.pallas_call(
        kern,
        grid=(batch, seq // TB),
        in_specs=_attn_specs(seq, (("q", CB_QD), ("k", CB_KD))) + [_vt_spec(nkb)],
        out_specs=_o_spec(seq),
        out_shape=jax.ShapeDtypeStruct((batch * seq, GROUP_W), BF16),
        scratch_shapes=[pltpu.VMEM((nkb_pad, GROUP_W), F32),
                        pltpu.VMEM((nkb_pad, HEADS_W), F32)] + _attn_scratch(),
        compiler_params=pltpu.CompilerParams(
            dimension_semantics=("arbitrary", "arbitrary"), vmem_limit_bytes=VMEM_LIMIT),
    )(qkv, qkv, vt)


def _outproj_kernel(oa_ref, ob_ref, oc_ref, od_ref, w_ref, x_ref, g_ref, b_ref, y_ref, *, alpha):
    acc = alpha * x_ref[...]
    for gi, o_ref in enumerate((oa_ref, ob_ref, oc_ref, od_ref)):
        acc = acc + jnp.dot(o_ref[...], w_ref[gi * GROUP_W:(gi + 1) * GROUP_W, :],
                            preferred_element_type=F32)
    y_ref[...] = _layer_norm(acc, g_ref[...], b_ref[...])


def _outproj(os4, w_o, x2d, g, b, alpha):
    m, d = x2d.shape
    tm = min(512, m)
    row = lambda i: (i, 0)
    fixed = lambda i: (0, 0)
    return pl.pallas_call(
        functools.partial(_outproj_kernel, alpha=alpha),
        grid=(m // tm,),
        in_specs=[pl.BlockSpec((tm, GROUP_W), row)] * 4 + [
            pl.BlockSpec((d, d), fixed), pl.BlockSpec((tm, d), row),
            pl.BlockSpec((1, d), fixed), pl.BlockSpec((1, d), fixed)],
        out_specs=pl.BlockSpec((tm, d), row),
        out_shape=jax.ShapeDtypeStruct((m, d), F32),
        compiler_params=pltpu.CompilerParams(
            dimension_semantics=("parallel",), vmem_limit_bytes=VMEM_LIMIT),
    )(*os4, w_o, x2d, g, b)


def _mlp_kernel(x_ref, wu_ref, wd_ref, g_ref, b_ref, y_ref, xb_ref, acc_ref, *, alpha):
    f = pl.program_id(1)

    @pl.when(f == 0)
    def _():
        x = x_ref[...]
        xb_ref[...] = x.astype(BF16)
        acc_ref[...] = alpha * x

    h = jnp.dot(xb_ref[...], wu_ref[...], preferred_element_type=F32)
    h = jnp.square(jnp.maximum(h, 0.0))
    acc_ref[...] += jnp.dot(h.astype(BF16), wd_ref[...], preferred_element_type=F32)

    @pl.when(f == pl.num_programs(1) - 1)
    def _():
        y_ref[...] = _layer_norm(acc_ref[...], g_ref[...], b_ref[...])


def _mlp(x2d, w_up, w_down, g, b, alpha):
    m, d = x2d.shape
    dff = w_up.shape[1]
    tm = min(1024, m)
    tf = min(512, dff)
    return pl.pallas_call(
        functools.partial(_mlp_kernel, alpha=alpha),
        grid=(m // tm, dff // tf),
        in_specs=[
            pl.BlockSpec((tm, d), lambda i, f: (i, 0)),
            pl.BlockSpec((d, tf), lambda i, f: (0, f)),
            pl.BlockSpec((tf, d), lambda i, f: (f, 0)),
            pl.BlockSpec((1, d), lambda i, f: (0, 0)),
            pl.BlockSpec((1, d), lambda i, f: (0, 0)),
        ],
        out_specs=pl.BlockSpec((tm, d), lambda i, f: (i, 0)),
        out_shape=jax.ShapeDtypeStruct((m, d), F32),
        scratch_shapes=[pltpu.VMEM((tm, d), BF16), pltpu.VMEM((tm, d), F32)],
        compiler_params=pltpu.CompilerParams(
            dimension_semantics=("parallel", "arbitrary"), vmem_limit_bytes=VMEM_LIMIT),
    )(x2d, w_up, w_down, g, b)


def _column_plan():
    hq = GROUP_W
    sizes = (hq, hq, hq, GROUP_HEADS,
             hq, hq, hq, IDX_HEADS * IDX_DIM, IDX_DIM, IDX_HEADS,
             hq, hq, hq, hq, hq, hq)
    offs = np.concatenate([[0], np.cumsum(sizes)])
    (qa, ka, va, fa, qb, kb, vb, qi, ki, wi, qc, kc, vc, qd, kd, vd) = [
        np.arange(offs[n], offs[n + 1]) for n in range(len(sizes))]
    blocks = {CB_QA: qa, CB_KA: ka, CB_VA: va, CB_VB: vb, CB_VC: vc, CB_VD: vd,
              CB_QB: qb, CB_KB: kb, CB_QC: qc, CB_KC: kc, CB_QD: qd, CB_KD: kd,
              CB_QI: qi, CB_KI8: np.tile(ki, IDX_HEADS)}
    scale = np.ones(N_COLBLK * GROUP_W, np.float32)
    for cb in (CB_QA, CB_QB, CB_QC, CB_QD):
        scale[cb * GROUP_W:(cb + 1) * GROUP_W] = LOG2E * HEAD_DIM ** -0.5
    return np.concatenate([blocks[cb] for cb in range(N_COLBLK)]), scale, np.concatenate([fa, wi])


def _rope_table(seq, dim):
    half = dim // 2
    lane = np.arange(LANES)
    inv = ROPE_THETA ** (-(jnp.arange(half, dtype=F32)) / half)
    ang = jnp.arange(seq, dtype=F32)[:, None] * inv[None, :]
    idx = lane % half
    sign = np.where((lane % dim) < half, -1.0, 1.0).astype(np.float32)
    return jnp.concatenate([jnp.cos(ang)[:, idx], jnp.sin(ang)[:, idx] * sign[None, :]], axis=1)


def _vt_blocks(qkv, cb, batch, seq):
    nkb = seq // TB
    v = qkv[:, cb * GROUP_W:(cb + 1) * GROUP_W].reshape(batch, nkb, TB, GROUP_HEADS, HEAD_DIM)
    pad = jnp.zeros((batch, nkb, TB, GROUP_HEADS, VROWS - HEAD_DIM), v.dtype).at[..., 0].set(1)
    v = jnp.concatenate([v, pad], axis=-1).reshape(batch, nkb, TB, GROUP_HEADS * VROWS)
    return v.transpose(0, 1, 3, 2)


def kernel(x, w_in, b_f, w_o, ln1_g, ln1_b, w_up, w_down, ln2_g, ln2_b):
    batch, seq, d = x.shape
    depth = w_in.shape[0]
    assert seq % TB == 0 and d == 4 * GROUP_W
    chunks = seq // LANES
    assert chunks & (chunks - 1) == 0
    alpha = (2.0 * depth) ** 0.25

    main_cols, main_scale, misc_cols = _column_plan()
    w_main = (w_in[:, :, main_cols] * main_scale).astype(BF16)
    w_misc = jnp.pad(w_in[:, :, misc_cols], ((0, 0), (0, 0), (0, LANES - len(misc_cols)))).astype(BF16)
    w_o16, w_up16, w_down16 = w_o.astype(BF16), w_up.astype(BF16), w_down.astype(BF16)
    tab64 = _rope_table(seq, HEAD_DIM)
    tab32 = _rope_table(seq, IDX_DIM)
    mult = jnp.asarray(_dilated_multiplicity())

    x2d = x.reshape(batch * seq, d)
    for l in range(depth):
        qkv, misc = _inproj(x2d, w_main[l], w_misc[l], tab64, tab32, seq)
        misc3 = misc.reshape(batch, seq, LANES)
        z = misc3[:, :, 0:GROUP_HEADS].transpose(0, 2, 1).reshape(batch, GROUP_HEADS * chunks, LANES)
        bias_rows = jnp.repeat(b_f[l].astype(F32), chunks)[:, None]
        gb = _gate_bias(z, bias_rows, chunks)
        gb = gb.reshape(batch, BIAS_TERMS, GROUP_HEADS, seq).transpose(0, 3, 2, 1)
        cbias = jnp.pad(gb.reshape(batch * seq, GROUP_HEADS * BIAS_TERMS),
                        ((0, 0), (0, LANES - GROUP_HEADS * BIAS_TERMS))).astype(BF16)
        wrow = misc3[:, :, GROUP_HEADS:GROUP_HEADS + IDX_HEADS].transpose(0, 2, 1)

        oa = _fox(qkv, cbias, _vt_blocks(qkv, CB_VA, batch, seq), batch, seq)
        ob = _dsa(qkv, _vt_blocks(qkv, CB_VB, batch, seq), wrow, batch, seq)
        oc = _dilated(qkv, _vt_blocks(qkv, CB_VC, batch, seq), mult, batch, seq)
        od = _moba(qkv, _vt_blocks(qkv, CB_VD, batch, seq), batch, seq)

        x2d = _outproj((oa, ob, oc, od), w_o16[l], x2d, ln1_g[l][None, :], ln1_b[l][None, :], alpha)
        x2d = _mlp(x2d, w_up16[l], w_down16[l], ln2_g[l][None, :], ln2_b[l][None, :], alpha)
    return x2d.reshape(batch, seq, d)
```

```python
import functools
import math

import numpy as np
import jax
import jax.numpy as jnp
from jax import lax
from jax.experimental import pallas as pl
from jax.experimental.pallas import tpu as pltpu

F32 = jnp.float32
BF16 = jnp.bfloat16

HEAD_DIM = 64
GROUP_HEADS = 4
GROUP_W = GROUP_HEADS * HEAD_DIM
IDX_HEADS = 8
IDX_DIM = 32
DSA_TOPK = 256
DIL_PATTERNS = ((128, 1), (512, 4), (2048, 16))
MOBA_BLOCK = 256
MOBA_TOPK = 3
ROPE_THETA = 10000.0
LN_EPS = 1e-5
LOG2E = math.log2(math.e)
LANES = 128
BF16_ROWS = 16
TB = 256
VROWS = HEAD_DIM + BF16_ROWS
NEG_INF = float("-inf")
VMEM_LIMIT = 56 * 1024 * 1024

(CB_QA, CB_KA, CB_VA, CB_VB, CB_VC, CB_VD,
 CB_QB, CB_KB, CB_QC, CB_KC, CB_QD, CB_KD,
 CB_QI, CB_KI8) = range(14)
N_COLBLK = 14
PROJ_TN = 2 * GROUP_W
PLAIN_STEPS, ROPE64_STEPS = 3, 3


def _dot_nt(a, b):
    return lax.dot_general(a, b, (((1,), (1,)), ((), ())), preferred_element_type=F32)


def _layer_norm(y, g, b):
    mu = jnp.mean(y, axis=-1, keepdims=True)
    d = y - mu
    var = jnp.mean(d * d, axis=-1, keepdims=True)
    return d * lax.rsqrt(var + LN_EPS) * g + b


def _tree_sum(parts):
    while len(parts) > 1:
        parts = [parts[n] + parts[n + 1] for n in range(0, len(parts), 2)]
    return parts[0]


def _rope(acc, tab_ref, half):
    cos = tab_ref[:, 0:LANES]
    sin = tab_ref[:, LANES:2 * LANES]
    lane = lax.broadcasted_iota(jnp.int32, (1, LANES), 1)
    first = (lane & (2 * half - 1)) < half
    outs = []
    for c in range(acc.shape[1] // LANES):
        a = acc[:, c * LANES:(c + 1) * LANES]
        swapped = jnp.where(first, pltpu.roll(a, LANES - half, 1), pltpu.roll(a, half, 1))
        outs.append(a * cos + swapped * sin)
    return jnp.concatenate(outs, axis=1)


def _inproj_kernel(x_ref, w_ref, wm_ref, t64_ref, t32_ref, o_ref, misc_ref, xb_ref):
    j = pl.program_id(1)

    @pl.when(j == 0)
    def _():
        xb = x_ref[...].astype(BF16)
        xb_ref[...] = xb
        misc_ref[...] = jnp.dot(xb, wm_ref[...], preferred_element_type=F32)

    acc = jnp.dot(xb_ref[...], w_ref[...], preferred_element_type=F32)

    @pl.when(j < PLAIN_STEPS)
    def _():
        o_ref[...] = acc.astype(BF16)

    @pl.when(jnp.logical_and(j >= PLAIN_STEPS, j < PLAIN_STEPS + ROPE64_STEPS))
    def _():
        o_ref[...] = _rope(acc, t64_ref, HEAD_DIM // 2).astype(BF16)

    @pl.when(j == PLAIN_STEPS + ROPE64_STEPS)
    def _():
        o_ref[...] = _rope(acc, t32_ref, IDX_DIM // 2).astype(BF16)


def _inproj(x2d, w_main, w_misc, tab64, tab32, seq):
    m, d = x2d.shape
    tm = min(1024, seq)
    nt = seq // tm
    return pl.pallas_call(
        _inproj_kernel,
        grid=(m // tm, N_COLBLK * GROUP_W // PROJ_TN),
        in_specs=[
            pl.BlockSpec((tm, d), lambda i, j: (i, 0)),
            pl.BlockSpec((d, PROJ_TN), lambda i, j: (0, j)),
            pl.BlockSpec((d, LANES), lambda i, j: (0, 0)),
            pl.BlockSpec((tm, 2 * LANES), lambda i, j: (i % nt, 0)),
            pl.BlockSpec((tm, 2 * LANES), lambda i, j: (i % nt, 0)),
        ],
        out_specs=[
            pl.BlockSpec((tm, PROJ_TN), lambda i, j: (i, j)),
            pl.BlockSpec((tm, LANES), lambda i, j: (i, 0)),
        ],
        out_shape=[
            jax.ShapeDtypeStruct((m, N_COLBLK * GROUP_W), BF16),
            jax.ShapeDtypeStruct((m, LANES), F32),
        ],
        scratch_shapes=[pltpu.VMEM((tm, d), BF16)],
        compiler_params=pltpu.CompilerParams(
            dimension_semantics=("parallel", "arbitrary"), vmem_limit_bytes=VMEM_LIMIT),
    )(x2d, w_main, w_misc, tab64, tab32)


def _gate_kernel(z_ref, b_ref, o_ref, *, chunks):
    z = z_ref[...] + b_ref[...]
    w = jnp.minimum(z, 0.0) - jnp.log1p(jnp.exp(-jnp.abs(z)))
    rows = w.shape[0]
    lane = lax.broadcasted_iota(jnp.int32, (rows, LANES), 1)
    s = 1
    while s < LANES:
        w = w + jnp.where(lane >= s, pltpu.roll(w, s, 1), 0.0)
        s *= 2
    tot = jnp.broadcast_to(w[:, LANES - 1:LANES], (rows, LANES))
    pos = lax.broadcasted_iota(jnp.int32, (rows, LANES), 0) & (chunks - 1)
    run = tot
    s = 1
    while s < chunks:
        run = run + jnp.where(pos >= s, pltpu.roll(run, s, 0), 0.0)
        s *= 2
    bias = -LOG2E * (w + (run - tot))
    hi = bias.astype(BF16).astype(F32)
    mid = (bias - hi).astype(BF16).astype(F32)
    lo = (bias - hi - mid).astype(BF16).astype(F32)
    o_ref[0] = hi
    o_ref[1] = mid
    o_ref[2] = lo


def _gate_bias(z, bias_rows, chunks):
    b, rows, _ = z.shape
    return pl.pallas_call(
        functools.partial(_gate_kernel, chunks=chunks),
        grid=(b,),
        in_specs=[pl.BlockSpec((None, rows, LANES), lambda i: (i, 0, 0)),
                  pl.BlockSpec((rows, 1), lambda i: (0, 0))],
        out_specs=pl.BlockSpec((None, 3, rows, LANES), lambda i: (i, 0, 0, 0)),
        out_shape=jax.ShapeDtypeStruct((b, 3, rows, LANES), F32),
        compiler_params=pltpu.CompilerParams(dimension_semantics=("parallel",)),
    )(z, bias_rows)


HEADS_W = GROUP_HEADS * TB


def _stack_heads(q_ref, qs_ref):
    q = q_ref[...].astype(F32)
    head_of_lane = lax.broadcasted_iota(jnp.int32, (1, GROUP_W), 1) >> 6
    for h in range(GROUP_HEADS):
        qs_ref[h * TB:(h + 1) * TB, 0:GROUP_W] = jnp.where(head_of_lane == h, q, 0.0).astype(BF16)


def _head(x, h):
    return x[:, h * TB:(h + 1) * TB]


def _tile_heads(x):
    return jnp.concatenate([x] * GROUP_HEADS, axis=1)


def _kblock(k_ref, j):
    return k_ref[pl.ds(pl.multiple_of(j * TB, TB), TB), :]


def _pair_scores(k_ref, qs_ref, j, extra_ref=None):
    kblk = _kblock(k_ref, j)
    outs = []
    for pr in range(GROUP_W // LANES):
        rows = slice(2 * pr * TB, 2 * (pr + 1) * TB)
        kp = kblk[:, pr * LANES:(pr + 1) * LANES]
        qp = qs_ref[rows, pr * LANES:(pr + 1) * LANES]
        if extra_ref is not None:
            kp = jnp.concatenate([kp, _kblock(extra_ref, j)], axis=1)
            qp = jnp.concatenate([qp, qs_ref[rows, GROUP_W:GROUP_W + LANES]], axis=1)
        outs.append(_dot_nt(kp, qp))
    return jnp.concatenate(outs, axis=1)


def _causal_mask(heads=GROUP_HEADS):
    krow = lax.broadcasted_iota(jnp.int32, (TB, heads * TB), 0)
    qcol = lax.broadcasted_iota(jnp.int32, (TB, heads * TB), 1) & (TB - 1)
    return krow <= qcol


def _consume(st, vt, m, accs, mask_fn=None, weight=None, keep=None):
    if mask_fn is not None:
        st = mask_fn(st)
    blk_max = jnp.max(st, axis=0, keepdims=True)
    m_new = jnp.maximum(m, blk_max)
    if keep is not None:
        m_new = jnp.where(keep > 0.5, m_new, m)
    m_safe = jnp.where(m_new == NEG_INF, 0.0, m_new)
    alpha = jnp.exp2(m - m_safe)
    p = jnp.exp2(st - m_safe)
    if weight is not None:
        p = p * weight
    pb = p.astype(BF16)
    new_accs = []
    for h in range(GROUP_HEADS):
        pv = jnp.dot(vt[h * VROWS:(h + 1) * VROWS, :], _head(pb, h), preferred_element_type=F32)
        upd = _head(alpha, h) * accs[h] + pv
        new_accs.append(upd if keep is None else jnp.where(_head(keep, h) > 0.5, upd, accs[h]))
    return m_new, new_accs


def _attn_scratch(qs_width=GROUP_W):
    return [
        pltpu.VMEM((HEADS_W, qs_width), BF16),
        pltpu.VMEM((2, TB, HEADS_W), F32),
        pltpu.VMEM((1, HEADS_W), F32),
        pltpu.VMEM((GROUP_HEADS * VROWS, TB), F32),
    ]


def _attend_sequence(scores_of, vt_ref, o_ref, scratch, n_steps, block_of, consume_args):
    _, st_ref, m_ref, acc_ref = scratch

    def issue(s, slot):
        st_ref[slot] = scores_of(block_of(s))

    def consume(s, slot, last):
        accs = [acc_ref[h * VROWS:(h + 1) * VROWS, :] for h in range(GROUP_HEADS)]
        m, accs = _consume(st_ref[slot], vt_ref[block_of(s)], m_ref[...], accs,
                           **consume_args(s, last))
        m_ref[...] = m
        for h in range(GROUP_HEADS):
            acc_ref[h * VROWS:(h + 1) * VROWS, :] = accs[h]

    m_ref[...] = jnp.full((1, HEADS_W), NEG_INF, F32)
    acc_ref[...] = jnp.zeros_like(acc_ref)
    issue(0, 0)

    def pair(pp, carry):
        s = 2 * pp
        issue(s + 1, 1)
        consume(s, 0, False)
        issue(s + 2, 0)
        consume(s + 1, 1, False)
        return carry

    lax.fori_loop(0, n_steps >> 1, pair, 0)

    @pl.when((n_steps & 1) == 1)
    def _():
        issue(n_steps, 1)
        consume(n_steps - 1, 0, False)
        consume(n_steps, 1, True)

    @pl.when((n_steps & 1) == 0)
    def _():
        consume(n_steps, 0, True)

    ot = jnp.concatenate(
        [acc_ref[h * VROWS:h * VROWS + HEAD_DIM, :] / acc_ref[h * VROWS + HEAD_DIM:h * VROWS + HEAD_DIM + 1, :]
         for h in range(GROUP_HEADS)], axis=0)
    o_ref[...] = ot.T.astype(o_ref.dtype)


def _attn_specs(seq, colblks):
    nq = seq // TB
    specs = []
    for kind, cb in colblks:
        if kind == "q":
            specs.append(pl.BlockSpec((TB, GROUP_W), lambda b, i, cb=cb: (b * nq + i, cb)))
        else:
            specs.append(pl.BlockSpec((seq, GROUP_W), lambda b, i, cb=cb: (b, cb)))
    return specs


def _vt_spec(nkb):
    return pl.BlockSpec((None, nkb, GROUP_HEADS * VROWS, TB), lambda b, i: (b, 0, 0, 0))


def _o_spec(seq):
    nq = seq // TB
    return pl.BlockSpec((TB, GROUP_W), lambda b, i: (b * nq + i, 0))


_ATTN_PARAMS = dict(
    compiler_params=pltpu.CompilerParams(
        dimension_semantics=("parallel", "arbitrary"), vmem_limit_bytes=VMEM_LIMIT))


BIAS_TERMS = 3


def _fox_kernel(q_ref, k_ref, cb_ref, vt_ref, o_ref, *scratch):
    i = pl.program_id(1)
    qs_ref = scratch[0]
    _stack_heads(q_ref, qs_ref)
    lane = lax.broadcasted_iota(jnp.int32, (TB, LANES), 1)
    for h in range(GROUP_HEADS):
        pick = jnp.logical_and(lane >= BIAS_TERMS * h, lane < BIAS_TERMS * (h + 1))
        qs_ref[h * TB:(h + 1) * TB, GROUP_W:GROUP_W + LANES] = jnp.where(pick, 1.0, 0.0).astype(BF16)

    def scores_of(j):
        return _pair_scores(k_ref, qs_ref, j, cb_ref)

    def consume_args(j, diag):
        if diag:
            return dict(mask_fn=lambda st: jnp.where(_causal_mask(), st, NEG_INF))
        return {}

    _attend_sequence(scores_of, vt_ref, o_ref, scratch, i, lambda s: s, consume_args)


def _fox(qkv, cbias, vt, batch, seq):
    nkb = seq // TB
    return pl.pallas_call(
        _fox_kernel,
        grid=(batch, seq // TB),
        in_specs=_attn_specs(seq, (("q", CB_QA), ("k", CB_KA))) + [
            pl.BlockSpec((seq, LANES), lambda b, i: (b, 0)),
            _vt_spec(nkb),
        ],
        out_specs=_o_spec(seq),
        out_shape=jax.ShapeDtypeStruct((batch * seq, GROUP_W), BF16),
        scratch_shapes=_attn_scratch(GROUP_W + LANES),
        **_ATTN_PARAMS,
    )(qkv, qkv, cbias, vt)


MANT_BITS = 23
EXP_PROBES = 4


def _key_to_float(key_u):
    key = key_u ^ jnp.int32(-2 ** 31)
    bits = jnp.where(key < 0, key ^ jnp.int32(0x7FFFFFFF), key)
    below_neg_inf = jnp.logical_and(key_u >= 0, key_u < jnp.int32(0x007FFFFF))
    return jnp.where(below_neg_inf, NEG_INF, lax.bitcast_convert_type(bits, F32))


def _dsa_kernel(q_ref, k_ref, vt_ref, qi_ref, k8_ref, w_ref, o_ref, strip_ref, qm_ref, *scratch,
                topk, idx_bits):
    i = pl.program_id(1)
    nblk = i + 1

    qi = qi_ref[...].astype(F32)
    low_lanes = lax.broadcasted_iota(jnp.int32, (1, LANES), 1) < IDX_DIM
    per_half = LANES // IDX_DIM
    for h in range(IDX_HEADS):
        half = qi[:, (h // per_half) * LANES:(h // per_half + 1) * LANES]
        if h % per_half:
            half = pltpu.roll(half, LANES - IDX_DIM * (h % per_half), 1)
        qm_ref[h] = jnp.where(low_lanes, half, 0.0).astype(BF16)

    wrow = w_ref[...]

    def score_block(j, diag, rmax):
        kblk = _kblock(k8_ref, j)[:, 0:LANES]
        sc = _tree_sum([wrow[h:h + 1, :] * jnp.maximum(_dot_nt(kblk, qm_ref[h]), 0.0)
                        for h in range(IDX_HEADS)])
        if diag:
            sc = jnp.where(_causal_mask(1), sc, NEG_INF)
        strip_ref[j] = sc
        return jnp.maximum(rmax, jnp.max(sc, axis=0, keepdims=True))

    rmax = lax.fori_loop(0, i, lambda j, c: score_block(j, False, c), jnp.full((1, TB), NEG_INF, F32))
    rmax = score_block(i, True, rmax)

    def count(pred):
        def body(j, acc):
            hit = jnp.where(pred(j, strip_ref[j]), 1.0, 0.0)
            return acc + _tree_sum([hit[r:r + 8, :] for r in range(0, TB, 8)])
        acc = lax.fori_loop(0, nblk, body, jnp.zeros((8, TB), F32))
        return jnp.sum(acc, axis=0, keepdims=True)

    def unsettled(cnt_at_prefix):
        return jnp.max(jnp.where(cnt_at_prefix == topk, 0.0, 1.0))

    def bit_body(c):
        it, _, prefix, cnt_at_prefix = c
        cand = prefix | jnp.left_shift(jnp.int32(1), 31 - it)
        t = _key_to_float(cand)
        cnt = count(lambda j, blk: blk >= t)
        take = cnt >= topk
        prefix = jnp.where(take, cand, prefix)
        cnt_at_prefix = jnp.where(take, cnt, cnt_at_prefix)
        return it + 1, unsettled(cnt_at_prefix), prefix, cnt_at_prefix

    n_all = jnp.full((1, TB), 1.0, F32) * (nblk * TB).astype(F32)

    top = jnp.broadcast_to(jnp.max(rmax, axis=1, keepdims=True), (1, TB))
    e_top = (lax.bitcast_convert_type(top, jnp.int32) >> MANT_BITS) & 0xFF
    usable = jnp.logical_and(top > 0.0, top < -NEG_INF)
    prefix = jnp.zeros((1, TB), jnp.int32)
    cnt_at_prefix = n_all
    for probe in range(EXP_PROBES):
        e = e_top - probe
        pow2 = jnp.left_shift(jnp.maximum(e, 1), MANT_BITS)
        t = lax.bitcast_convert_type(pow2, F32)
        cnt = count(lambda j, blk: blk >= t)
        take = jnp.logical_and(jnp.logical_and(cnt >= topk, prefix == 0),
                               jnp.logical_and(usable, e >= 1))
        prefix = jnp.where(take, pow2 ^ jnp.int32(-2 ** 31), prefix)
        cnt_at_prefix = jnp.where(take, cnt, cnt_at_prefix)
    resolved = jnp.min(jnp.where(prefix == 0, 0.0, 1.0)) > 0.5
    prefix = jnp.where(resolved, prefix, 0)
    cnt_at_prefix = jnp.where(resolved, cnt_at_prefix, n_all)
    first_bit = jnp.where(resolved, 31 - MANT_BITS + 1, 0)

    init = (first_bit, unsettled(cnt_at_prefix), prefix, cnt_at_prefix)
    _, _, prefix, cnt_ge = lax.while_loop(
        lambda c: jnp.logical_and(c[0] < 32, c[1] > 0.0), bit_body, init)
    thr = _key_to_float(prefix)

    @pl.when(jnp.max(cnt_ge) > topk)
    def _():
        need = topk - count(lambda j, blk: blk > thr)
        krow = lax.broadcasted_iota(jnp.int32, (TB, TB), 0)

        def idx_body(it, lim):
            cand = lim | jnp.left_shift(jnp.int32(1), idx_bits - 1 - it)
            cnt = count(lambda j, blk: jnp.logical_and(blk == thr, krow + j * TB < cand))
            return jnp.where(cnt <= need, cand, lim)

        lim = lax.fori_loop(0, idx_bits, idx_body, jnp.zeros((1, TB), jnp.int32))

        def drop_body(j, carry):
            blk = strip_ref[j]
            drop = jnp.logical_and(blk == thr, krow + j * TB >= lim)
            strip_ref[j] = jnp.where(drop, NEG_INF, blk)
            return carry

        lax.fori_loop(0, nblk, drop_body, 0)

    qs_ref = scratch[0]
    _stack_heads(q_ref, qs_ref)

    def consume_args(j, diag):
        def mask_fn(st):
            blk = strip_ref[j]
            sel = blk >= thr
            if diag:
                sel = jnp.logical_and(sel, blk > NEG_INF)
            return jnp.concatenate(
                [jnp.where(sel, _head(st, h), NEG_INF) for h in range(GROUP_HEADS)], axis=1)
        return dict(mask_fn=mask_fn)

    _attend_sequence(lambda j: _pair_scores(k_ref, qs_ref, j), vt_ref, o_ref, scratch,
                     i, lambda s: s, consume_args)


def _dsa(qkv, vt, wrow, batch, seq):
    nkb = seq // TB
    topk = min(DSA_TOPK, seq // 4)
    kern = functools.partial(_dsa_kernel, topk=float(topk), idx_bits=int(seq).bit_length())
    return pl.pallas_call(
        kern,
        grid=(batch, seq // TB),
        in_specs=_attn_specs(seq, (("q", CB_QB), ("k", CB_KB))) + [_vt_spec(nkb)]
        + _attn_specs(seq, (("q", CB_QI), ("k", CB_KI8)))
        + [pl.BlockSpec((None, IDX_HEADS, TB), lambda b, i: (b, 0, i))],
        out_specs=_o_spec(seq),
        out_shape=jax.ShapeDtypeStruct((batch * seq, GROUP_W), BF16),
        scratch_shapes=[pltpu.VMEM((nkb, TB, TB), F32),
                        pltpu.VMEM((IDX_HEADS, TB, LANES), BF16)] + _attn_scratch(),
        **_ATTN_PARAMS,
    )(qkv, qkv, vt, qkv, qkv, wrow)


DIL_SPAN = max(w for w, _ in DIL_PATTERNS) // TB + 1


def _dilated_multiplicity():
    k = np.arange(TB)[:, None]
    q = np.arange(TB)[None, :]
    out = np.zeros((DIL_SPAN, TB, TB), np.float32)
    for delta in range(DIL_SPAN):
        d = q - k + TB * delta
        for w, r in DIL_PATTERNS:
            out[delta] += ((d >= 0) & (d % r == 0) & (d <= w)).astype(np.float32)
    return out


def _dil_kernel(q_ref, k_ref, vt_ref, mult_ref, o_ref, *scratch):
    i = pl.program_id(1)
    qs_ref = scratch[0]
    _stack_heads(q_ref, qs_ref)
    n = jnp.minimum(i, DIL_SPAN - 1)

    def consume_args(s, last):
        mult = _tile_heads(mult_ref[n - s])
        return dict(mask_fn=lambda st: jnp.where(mult > 0.0, st, NEG_INF), weight=mult)

    _attend_sequence(lambda j: _pair_scores(k_ref, qs_ref, j), vt_ref, o_ref, scratch,
                     n, lambda s: i - n + s, consume_args)


def _dilated(qkv, vt, mult, batch, seq):
    nkb = seq // TB
    return pl.pallas_call(
        _dil_kernel,
        grid=(batch, seq // TB),
        in_specs=_attn_specs(seq, (("q", CB_QC), ("k", CB_KC))) + [
            _vt_spec(nkb),
            pl.BlockSpec((DIL_SPAN, TB, TB), lambda b, i: (0, 0, 0)),
        ],
        out_specs=_o_spec(seq),
        out_shape=jax.ShapeDtypeStruct((batch * seq, GROUP_W), BF16),
        scratch_shapes=_attn_scratch(),
        **_ATTN_PARAMS,
    )(qkv, qkv, vt, mult)


def _moba_kernel(q_ref, k_ref, vt_ref, o_ref, kmean_ref, sel_ref, *scratch, nkb, nkb_pad, topn):
    i = pl.program_id(1)

    @pl.when(i == 0)
    def _():
        kmean_ref[...] = jnp.zeros_like(kmean_ref)
        for n in range(nkb):
            kb = k_ref[n * TB:(n + 1) * TB, :].astype(F32)
            kmean_ref[n:n + 1, :] = jnp.mean(kb, axis=0, keepdims=True)

    qs_ref = scratch[0]
    _stack_heads(q_ref, qs_ref)
    blk_id = lax.broadcasted_iota(jnp.int32, (nkb_pad, HEADS_W), 0)
    blk_f = blk_id.astype(F32)
    g = _dot_nt(kmean_ref[...].astype(BF16), qs_ref[...])
    g = jnp.where(blk_id < i, g, NEG_INF)
    sel = jnp.zeros_like(g)
    for _ in range(topn):
        mx = jnp.max(g, axis=0, keepdims=True)
        is_max = jnp.logical_and(g == mx, mx > NEG_INF)
        first = jnp.min(jnp.where(is_max, blk_f, float(nkb_pad)), axis=0, keepdims=True)
        pick = blk_f == first
        sel = jnp.where(pick, 1.0, sel)
        g = jnp.where(pick, NEG_INF, g)
    sel_ref[...] = sel

    def consume_args(j, diag):
        if diag:
            return dict(mask_fn=lambda st: jnp.where(_causal_mask(), st, NEG_INF))
        return dict(keep=sel_ref[pl.ds(j, 1), :])

    _attend_sequence(lambda j: _pair_scores(k_ref, qs_ref, j), vt_ref, o_ref, scratch,
                     i, lambda s: s, consume_args)


def _moba(qkv, vt, batch, seq):
    nkb = seq // TB
    nkb_pad = -(-nkb // 8) * 8
    kern = functools.partial(_moba_kernel, nkb=nkb, nkb_pad=nkb_pad, topn=min(MOBA_TOPK, nkb))
    return pl.pallas_call(
        kern,
        grid=(batch, seq // TB),
        in_specs=_attn_specs(seq, (("q", CB_QD), ("k", CB_KD))) + [_vt_spec(nkb)],
        out_specs=_o_spec(seq),
        out_shape=jax.ShapeDtypeStruct((batch * seq, GROUP_W), BF16),
        scratch_shapes=[pltpu.VMEM((nkb_pad, GROUP_W), F32),
                        pltpu.VMEM((nkb_pad, HEADS_W), F32)] + _attn_scratch(),
        compiler_params=pltpu.CompilerParams(
            dimension_semantics=("arbitrary", "arbitrary"), vmem_limit_bytes=VMEM_LIMIT),
    )(qkv, qkv, vt)


def _outproj_kernel(oa_ref, ob_ref, oc_ref, od_ref, w_ref, x_ref, g_ref, b_ref, y_ref, *, alpha):
    acc = alpha * x_ref[...]
    for gi, o_ref in enumerate((oa_ref, ob_ref, oc_ref, od_ref)):
        acc = acc + jnp.dot(o_ref[...], w_ref[gi * GROUP_W:(gi + 1) * GROUP_W, :],
                            preferred_element_type=F32)
    y_ref[...] = _layer_norm(acc, g_ref[...], b_ref[...])


def _outproj(os4, w_o, x2d, g, b, alpha):
    m, d = x2d.shape
    tm = min(512, m)
    row = lambda i: (i, 0)
    fixed = lambda i: (0, 0)
    return pl.pallas_call(
        functools.partial(_outproj_kernel, alpha=alpha),
        grid=(m // tm,),
        in_specs=[pl.BlockSpec((tm, GROUP_W), row)] * 4 + [
            pl.BlockSpec((d, d), fixed), pl.BlockSpec((tm, d), row),
            pl.BlockSpec((1, d), fixed), pl.BlockSpec((1, d), fixed)],
        out_specs=pl.BlockSpec((tm, d), row),
        out_shape=jax.ShapeDtypeStruct((m, d), F32),
        compiler_params=pltpu.CompilerParams(
            dimension_semantics=("parallel",), vmem_limit_bytes=VMEM_LIMIT),
    )(*os4, w_o, x2d, g, b)


def _mlp_kernel(x_ref, wu_ref, wd_ref, g_ref, b_ref, y_ref, xb_ref, acc_ref, *, alpha):
    f = pl.program_id(1)

    @pl.when(f == 0)
    def _():
        x = x_ref[...]
        xb_ref[...] = x.astype(BF16)
        acc_ref[...] = alpha * x

    h = jnp.dot(xb_ref[...], wu_ref[...], preferred_element_type=F32)
    h = jnp.square(jnp.maximum(h, 0.0))
    acc_ref[...] += jnp.dot(h.astype(BF16), wd_ref[...], preferred_element_type=F32)

    @pl.when(f == pl.num_programs(1) - 1)
    def _():
        y_ref[...] = _layer_norm(acc_ref[...], g_ref[...], b_ref[...])


def _mlp(x2d, w_up, w_down, g, b, alpha):
    m, d = x2d.shape
    dff = w_up.shape[1]
    tm = min(1024, m)
    tf = min(512, dff)
    return pl.pallas_call(
        functools.partial(_mlp_kernel, alpha=alpha),
        grid=(m // tm, dff // tf),
        in_specs=[
            pl.BlockSpec((tm, d), lambda i, f: (i, 0)),
            pl.BlockSpec((d, tf), lambda i, f: (0, f)),
            pl.BlockSpec((tf, d), lambda i, f: (f, 0)),
            pl.BlockSpec((1, d), lambda i, f: (0, 0)),
            pl.BlockSpec((1, d), lambda i, f: (0, 0)),
        ],
        out_specs=pl.BlockSpec((tm, d), lambda i, f: (i, 0)),
        out_shape=jax.ShapeDtypeStruct((m, d), F32),
        scratch_shapes=[pltpu.VMEM((tm, d), BF16), pltpu.VMEM((tm, d), F32)],
        compiler_params=pltpu.CompilerParams(
            dimension_semantics=("parallel", "arbitrary"), vmem_limit_bytes=VMEM_LIMIT),
    )(x2d, w_up, w_down, g, b)


def _column_plan():
    hq = GROUP_W
    sizes = (hq, hq, hq, GROUP_HEADS,
             hq, hq, hq, IDX_HEADS * IDX_DIM, IDX_DIM, IDX_HEADS,
             hq, hq, hq, hq, hq, hq)
    offs = np.concatenate([[0], np.cumsum(sizes)])
    (qa, ka, va, fa, qb, kb, vb, qi, ki, wi, qc, kc, vc, qd, kd, vd) = [
        np.arange(offs[n], offs[n + 1]) for n in range(len(sizes))]
    blocks = {CB_QA: qa, CB_KA: ka, CB_VA: va, CB_VB: vb, CB_VC: vc, CB_VD: vd,
              CB_QB: qb, CB_KB: kb, CB_QC: qc, CB_KC: kc, CB_QD: qd, CB_KD: kd,
              CB_QI: qi, CB_KI8: np.tile(ki, IDX_HEADS)}
    scale = np.ones(N_COLBLK * GROUP_W, np.float32)
    for cb in (CB_QA, CB_QB, CB_QC, CB_QD):
        scale[cb * GROUP_W:(cb + 1) * GROUP_W] = LOG2E * HEAD_DIM ** -0.5
    return np.concatenate([blocks[cb] for cb in range(N_COLBLK)]), scale, np.concatenate([fa, wi])


def _rope_table(seq, dim):
    half = dim // 2
    lane = np.arange(LANES)
    inv = ROPE_THETA ** (-(jnp.arange(half, dtype=F32)) / half)
    ang = jnp.arange(seq, dtype=F32)[:, None] * inv[None, :]
    idx = lane % half
    sign = np.where((lane % dim) < half, -1.0, 1.0).astype(np.float32)
    return jnp.concatenate([jnp.cos(ang)[:, idx], jnp.sin(ang)[:, idx] * sign[None, :]], axis=1)


def _vt_blocks(qkv, cb, batch, seq):
    nkb = seq // TB
    v = qkv[:, cb * GROUP_W:(cb + 1) * GROUP_W].reshape(batch, nkb, TB, GROUP_HEADS, HEAD_DIM)
    pad = jnp.zeros((batch, nkb, TB, GROUP_HEADS, VROWS - HEAD_DIM), v.dtype).at[..., 0].set(1)
    v = jnp.concatenate([v, pad], axis=-1).reshape(batch, nkb, TB, GROUP_HEADS * VROWS)
    return v.transpose(0, 1, 3, 2)


def kernel(x, w_in, b_f, w_o, ln1_g, ln1_b, w_up, w_down, ln2_g, ln2_b):
    batch, seq, d = x.shape
    depth = w_in.shape[0]
    assert seq % TB == 0 and d == 4 * GROUP_W
    chunks = seq // LANES
    assert chunks & (chunks - 1) == 0
    alpha = (2.0 * depth) ** 0.25

    main_cols, main_scale, misc_cols = _column_plan()
    w_main = (w_in[:, :, main_cols] * main_scale).astype(BF16)
    w_misc = jnp.pad(w_in[:, :, misc_cols], ((0, 0), (0, 0), (0, LANES - len(misc_cols)))).astype(BF16)
    w_o16, w_up16, w_down16 = w_o.astype(BF16), w_up.astype(BF16), w_down.astype(BF16)
    tab64 = _rope_table(seq, HEAD_DIM)
    tab32 = _rope_table(seq, IDX_DIM)
    mult = jnp.asarray(_dilated_multiplicity())

    x2d = x.reshape(batch * seq, d)
    for l in range(depth):
        qkv, misc = _inproj(x2d, w_main[l], w_misc[l], tab64, tab32, seq)
        misc3 = misc.reshape(batch, seq, LANES)
        z = misc3[:, :, 0:GROUP_HEADS].transpose(0, 2, 1).reshape(batch, GROUP_HEADS * chunks, LANES)
        bias_rows = jnp.repeat(b_f[l].astype(F32), chunks)[:, None]
        gb = _gate_bias(z, bias_rows, chunks)
        gb = gb.reshape(batch, BIAS_TERMS, GROUP_HEADS, seq).transpose(0, 3, 2, 1)
        cbias = jnp.pad(gb.reshape(batch * seq, GROUP_HEADS * BIAS_TERMS),
                        ((0, 0), (0, LANES - GROUP_HEADS * BIAS_TERMS))).astype(BF16)
        wrow = misc3[:, :, GROUP_HEADS:GROUP_HEADS + IDX_HEADS].transpose(0, 2, 1)

        oa = _fox(qkv, cbias, _vt_blocks(qkv, CB_VA, batch, seq), batch, seq)
        ob = _dsa(qkv, _vt_blocks(qkv, CB_VB, batch, seq), wrow, batch, seq)
        oc = _dilated(qkv, _vt_blocks(qkv, CB_VC, batch, seq), mult, batch, seq)
        od = _moba(qkv, _vt_blocks(qkv, CB_VD, batch, seq), batch, seq)

        x2d = _outproj((oa, ob, oc, od), w_o16[l], x2d, ln1_g[l][None, :], ln1_b[l][None, :], alpha)
        x2d = _mlp(x2d, w_up16[l], w_down16[l], ln2_g[l][None, :], ln2_b[l][None, :], alpha)
    return x2d.reshape(batch, seq, d)
```

```python
import functools
import math

import numpy as np
import jax
import jax.numpy as jnp
from jax import lax
from jax.experimental import pallas as pl
from jax.experimental.pallas import tpu as pltpu

F32 = jnp.float32
BF16 = jnp.bfloat16

HEAD_DIM = 64
GROUP_HEADS = 4
GROUP_W = GROUP_HEADS * HEAD_DIM
IDX_HEADS = 8
IDX_DIM = 32
DSA_TOPK = 256
DIL_PATTERNS = ((128, 1), (512, 4), (2048, 16))
MOBA_BLOCK = 256
MOBA_TOPK = 3
ROPE_THETA = 10000.0
LN_EPS = 1e-5
LOG2E = math.log2(math.e)
LANES = 128
BF16_ROWS = 16
TB = 256
VROWS = HEAD_DIM + BF16_ROWS
NEG_INF = float("-inf")
VMEM_LIMIT = 56 * 1024 * 1024

(CB_QA, CB_KA, CB_VA, CB_VB, CB_VC, CB_VD,
 CB_QB, CB_KB, CB_QC, CB_KC, CB_QD, CB_KD,
 CB_QI, CB_KI8) = range(14)
N_COLBLK = 14
PROJ_TN = 2 * GROUP_W
PLAIN_STEPS, ROPE64_STEPS = 3, 3


def _dot_nt(a, b):
    return lax.dot_general(a, b, (((1,), (1,)), ((), ())), preferred_element_type=F32)


def _layer_norm(y, g, b):
    mu = jnp.mean(y, axis=-1, keepdims=True)
    d = y - mu
    var = jnp.mean(d * d, axis=-1, keepdims=True)
    return d * lax.rsqrt(var + LN_EPS) * g + b


def _tree_sum(parts):
    while len(parts) > 1:
        parts = [parts[n] + parts[n + 1] for n in range(0, len(parts), 2)]
    return parts[0]


def _rope(acc, tab_ref, half):
    cos = tab_ref[:, 0:LANES]
    sin = tab_ref[:, LANES:2 * LANES]
    lane = lax.broadcasted_iota(jnp.int32, (1, LANES), 1)
    first = (lane & (2 * half - 1)) < half
    outs = []
    for c in range(acc.shape[1] // LANES):
        a = acc[:, c * LANES:(c + 1) * LANES]
        swapped = jnp.where(first, pltpu.roll(a, LANES - half, 1), pltpu.roll(a, half, 1))
        outs.append(a * cos + swapped * sin)
    return jnp.concatenate(outs, axis=1)


def _inproj_kernel(x_ref, w_ref, wm_ref, t64_ref, t32_ref, o_ref, misc_ref, xb_ref):
    j = pl.program_id(1)

    @pl.when(j == 0)
    def _():
        xb = x_ref[...].astype(BF16)
        xb_ref[...] = xb
        misc_ref[...] = jnp.dot(xb, wm_ref[...], preferred_element_type=F32)

    acc = jnp.dot(xb_ref[...], w_ref[...], preferred_element_type=F32)

    @pl.when(j < PLAIN_STEPS)
    def _():
        o_ref[...] = acc.astype(BF16)

    @pl.when(jnp.logical_and(j >= PLAIN_STEPS, j < PLAIN_STEPS + ROPE64_STEPS))
    def _():
        o_ref[...] = _rope(acc, t64_ref, HEAD_DIM // 2).astype(BF16)

    @pl.when(j == PLAIN_STEPS + ROPE64_STEPS)
    def _():
        o_ref[...] = _rope(acc, t32_ref, IDX_DIM // 2).astype(BF16)


def _inproj(x2d, w_main, w_misc, tab64, tab32, seq):
    m, d = x2d.shape
    tm = min(1024, seq)
    nt = seq // tm
    return pl.pallas_call(
        _inproj_kernel,
        grid=(m // tm, N_COLBLK * GROUP_W // PROJ_TN),
        in_specs=[
            pl.BlockSpec((tm, d), lambda i, j: (i, 0)),
            pl.BlockSpec((d, PROJ_TN), lambda i, j: (0, j)),
            pl.BlockSpec((d, LANES), lambda i, j: (0, 0)),
            pl.BlockSpec((tm, 2 * LANES), lambda i, j: (i % nt, 0)),
            pl.BlockSpec((tm, 2 * LANES), lambda i, j: (i % nt, 0)),
        ],
        out_specs=[
            pl.BlockSpec((tm, PROJ_TN), lambda i, j: (i, j)),
            pl.BlockSpec((tm, LANES), lambda i, j: (i, 0)),
        ],
        out_shape=[
            jax.ShapeDtypeStruct((m, N_COLBLK * GROUP_W), BF16),
            jax.ShapeDtypeStruct((m, LANES), F32),
        ],
        scratch_shapes=[pltpu.VMEM((tm, d), BF16)],
        compiler_params=pltpu.CompilerParams(
            dimension_semantics=("parallel", "arbitrary"), vmem_limit_bytes=VMEM_LIMIT),
    )(x2d, w_main, w_misc, tab64, tab32)


def _gate_kernel(z_ref, b_ref, o_ref, *, chunks):
    z = z_ref[...] + b_ref[...]
    w = jnp.minimum(z, 0.0) - jnp.log1p(jnp.exp(-jnp.abs(z)))
    rows = w.shape[0]
    lane = lax.broadcasted_iota(jnp.int32, (rows, LANES), 1)
    s = 1
    while s < LANES:
        w = w + jnp.where(lane >= s, pltpu.roll(w, s, 1), 0.0)
        s *= 2
    tot = jnp.broadcast_to(w[:, LANES - 1:LANES], (rows, LANES))
    pos = lax.broadcasted_iota(jnp.int32, (rows, LANES), 0) & (chunks - 1)
    run = tot
    s = 1
    while s < chunks:
        run = run + jnp.where(pos >= s, pltpu.roll(run, s, 0), 0.0)
        s *= 2
    bias = -LOG2E * (w + (run - tot))
    hi = bias.astype(BF16).astype(F32)
    mid = (bias - hi).astype(BF16).astype(F32)
    lo = (bias - hi - mid).astype(BF16).astype(F32)
    o_ref[0] = hi
    o_ref[1] = mid
    o_ref[2] = lo


def _gate_bias(z, bias_rows, chunks):
    b, rows, _ = z.shape
    return pl.pallas_call(
        functools.partial(_gate_kernel, chunks=chunks),
        grid=(b,),
        in_specs=[pl.BlockSpec((None, rows, LANES), lambda i: (i, 0, 0)),
                  pl.BlockSpec((rows, 1), lambda i: (0, 0))],
        out_specs=pl.BlockSpec((None, 3, rows, LANES), lambda i: (i, 0, 0, 0)),
        out_shape=jax.ShapeDtypeStruct((b, 3, rows, LANES), F32),
        compiler_params=pltpu.CompilerParams(dimension_semantics=("parallel",)),
    )(z, bias_rows)


HEADS_W = GROUP_HEADS * TB


def _stack_heads(q_ref, qs_ref):
    q = q_ref[...].astype(F32)
    head_of_lane = lax.broadcasted_iota(jnp.int32, (1, GROUP_W), 1) >> 6
    for h in range(GROUP_HEADS):
        qs_ref[h * TB:(h + 1) * TB, 0:GROUP_W] = jnp.where(head_of_lane == h, q, 0.0).astype(BF16)


def _head(x, h):
    return x[:, h * TB:(h + 1) * TB]


def _tile_heads(x):
    return jnp.concatenate([x] * GROUP_HEADS, axis=1)


def _kblock(k_ref, j):
    return k_ref[pl.ds(pl.multiple_of(j * TB, TB), TB), :]


def _pair_scores(k_ref, qs_ref, j, extra_ref=None):
    kblk = _kblock(k_ref, j)
    outs = []
    for pr in range(GROUP_W // LANES):
        rows = slice(2 * pr * TB, 2 * (pr + 1) * TB)
        kp = kblk[:, pr * LANES:(pr + 1) * LANES]
        qp = qs_ref[rows, pr * LANES:(pr + 1) * LANES]
        if extra_ref is not None:
            kp = jnp.concatenate([kp, _kblock(extra_ref, j)], axis=1)
            qp = jnp.concatenate([qp, qs_ref[rows, GROUP_W:GROUP_W + LANES]], axis=1)
        outs.append(_dot_nt(kp, qp))
    return jnp.concatenate(outs, axis=1)


def _causal_mask(heads=GROUP_HEADS):
    krow = lax.broadcasted_iota(jnp.int32, (TB, heads * TB), 0)
    qcol = lax.broadcasted_iota(jnp.int32, (TB, heads * TB), 1) & (TB - 1)
    return krow <= qcol


def _consume(st, vt, m, accs, mask_fn=None, weight=None, keep=None):
    if mask_fn is not None:
        st = mask_fn(st)
    blk_max = jnp.max(st, axis=0, keepdims=True)
    m_new = jnp.maximum(m, blk_max)
    if keep is not None:
        m_new = jnp.where(keep > 0.5, m_new, m)
    m_safe = jnp.where(m_new == NEG_INF, 0.0, m_new)
    alpha = jnp.exp2(m - m_safe)
    p = jnp.exp2(st - m_safe)
    if weight is not None:
        p = p * weight
    pb = p.astype(BF16)
    new_accs = []
    for h in range(GROUP_HEADS):
        pv = jnp.dot(vt[h * VROWS:(h + 1) * VROWS, :], _head(pb, h), preferred_element_type=F32)
        upd = _head(alpha, h) * accs[h] + pv
        new_accs.append(upd if keep is None else jnp.where(_head(keep, h) > 0.5, upd, accs[h]))
    return m_new, new_accs


def _attn_scratch(qs_width=GROUP_W):
    return [
        pltpu.VMEM((HEADS_W, qs_width), BF16),
        pltpu.VMEM((2, TB, HEADS_W), F32),
        pltpu.VMEM((1, HEADS_W), F32),
        pltpu.VMEM((GROUP_HEADS * VROWS, TB), F32),
    ]


def _attend_sequence(scores_of, vt_ref, o_ref, scratch, n_steps, block_of, consume_args):
    _, st_ref, m_ref, acc_ref = scratch

    def issue(s, slot):
        st_ref[slot] = scores_of(block_of(s))

    def consume(s, slot, last):
        accs = [acc_ref[h * VROWS:(h + 1) * VROWS, :] for h in range(GROUP_HEADS)]
        m, accs = _consume(st_ref[slot], vt_ref[block_of(s)], m_ref[...], accs,
                           **consume_args(s, last))
        m_ref[...] = m
        for h in range(GROUP_HEADS):
            acc_ref[h * VROWS:(h + 1) * VROWS, :] = accs[h]

    m_ref[...] = jnp.full((1, HEADS_W), NEG_INF, F32)
    acc_ref[...] = jnp.zeros_like(acc_ref)
    issue(0, 0)

    def pair(pp, carry):
        s = 2 * pp
        issue(s + 1, 1)
        consume(s, 0, False)
        issue(s + 2, 0)
        consume(s + 1, 1, False)
        return carry

    lax.fori_loop(0, n_steps >> 1, pair, 0)

    @pl.when((n_steps & 1) == 1)
    def _():
        issue(n_steps, 1)
        consume(n_steps - 1, 0, False)
        consume(n_steps, 1, True)

    @pl.when((n_steps & 1) == 0)
    def _():
        consume(n_steps, 0, True)

    ot = jnp.concatenate(
        [acc_ref[h * VROWS:h * VROWS + HEAD_DIM, :] / acc_ref[h * VROWS + HEAD_DIM:h * VROWS + HEAD_DIM + 1, :]
         for h in range(GROUP_HEADS)], axis=0)
    o_ref[...] = ot.T.astype(o_ref.dtype)


def _attn_specs(seq, colblks):
    nq = seq // TB
    specs = []
    for kind, cb in colblks:
        if kind == "q":
            specs.append(pl.BlockSpec((TB, GROUP_W), lambda b, i, cb=cb: (b * nq + i, cb)))
        else:
            specs.append(pl.BlockSpec((seq, GROUP_W), lambda b, i, cb=cb: (b, cb)))
    return specs


def _vt_spec(nkb):
    return pl.BlockSpec((None, nkb, GROUP_HEADS * VROWS, TB), lambda b, i: (b, 0, 0, 0))


def _o_spec(seq):
    nq = seq // TB
    return pl.BlockSpec((TB, GROUP_W), lambda b, i: (b * nq + i, 0))


_ATTN_PARAMS = dict(
    compiler_params=pltpu.CompilerParams(
        dimension_semantics=("parallel", "arbitrary"), vmem_limit_bytes=VMEM_LIMIT))


BIAS_TERMS = 3


def _fox_kernel(q_ref, k_ref, cb_ref, vt_ref, o_ref, *scratch):
    i = pl.program_id(1)
    qs_ref = scratch[0]
    _stack_heads(q_ref, qs_ref)
    lane = lax.broadcasted_iota(jnp.int32, (TB, LANES), 1)
    for h in range(GROUP_HEADS):
        pick = jnp.logical_and(lane >= BIAS_TERMS * h, lane < BIAS_TERMS * (h + 1))
        qs_ref[h * TB:(h + 1) * TB, GROUP_W:GROUP_W + LANES] = jnp.where(pick, 1.0, 0.0).astype(BF16)

    def scores_of(j):
        return _pair_scores(k_ref, qs_ref, j, cb_ref)

    def consume_args(j, diag):
        if diag:
            return dict(mask_fn=lambda st: jnp.where(_causal_mask(), st, NEG_INF))
        return {}

    _attend_sequence(scores_of, vt_ref, o_ref, scratch, i, lambda s: s, consume_args)


def _fox(qkv, cbias, vt, batch, seq):
    nkb = seq // TB
    return pl.pallas_call(
        _fox_kernel,
        grid=(batch, seq // TB),
        in_specs=_attn_specs(seq, (("q", CB_QA), ("k", CB_KA))) + [
            pl.BlockSpec((seq, LANES), lambda b, i: (b, 0)),
            _vt_spec(nkb),
        ],
        out_specs=_o_spec(seq),
        out_shape=jax.ShapeDtypeStruct((batch * seq, GROUP_W), BF16),
        scratch_shapes=_attn_scratch(GROUP_W + LANES),
        **_ATTN_PARAMS,
    )(qkv, qkv, cbias, vt)


HIGH_HALF = -(1 << 16)


def _key_to_float(key_u):
    key = key_u ^ jnp.int32(-2 ** 31)
    bits = jnp.where(key < 0, key ^ jnp.int32(0x7FFFFFFF), key)
    below_neg_inf = jnp.logical_and(key_u >= 0, key_u < jnp.int32(0x007FFFFF))
    return jnp.where(below_neg_inf, NEG_INF, lax.bitcast_convert_type(bits, F32))


def _dsa_kernel(q_ref, k_ref, vt_ref, qi_ref, k8_ref, w_ref, o_ref, strip_ref, strip_hi_ref, qm_ref, *scratch,
                topk, idx_bits):
    i = pl.program_id(1)
    nblk = i + 1

    qi = qi_ref[...].astype(F32)
    low_lanes = lax.broadcasted_iota(jnp.int32, (1, LANES), 1) < IDX_DIM
    per_half = LANES // IDX_DIM
    for h in range(IDX_HEADS):
        half = qi[:, (h // per_half) * LANES:(h // per_half + 1) * LANES]
        if h % per_half:
            half = pltpu.roll(half, LANES - IDX_DIM * (h % per_half), 1)
        qm_ref[h] = jnp.where(low_lanes, half, 0.0).astype(BF16)

    wrow = w_ref[...]

    def score_block(j, diag):
        kblk = _kblock(k8_ref, j)[:, 0:LANES]
        sc = _tree_sum([wrow[h:h + 1, :] * jnp.maximum(_dot_nt(kblk, qm_ref[h]), 0.0)
                        for h in range(IDX_HEADS)])
        if diag:
            sc = jnp.where(_causal_mask(1), sc, NEG_INF)
        strip_ref[j] = sc
        strip_hi_ref[j] = lax.bitcast_convert_type(
            lax.bitcast_convert_type(sc, jnp.int32) & HIGH_HALF, F32).astype(BF16)

    def score_body(j, carry):
        score_block(j, False)
        return carry

    lax.fori_loop(0, i, score_body, 0)
    score_block(i, True)

    def count(pred):
        def body(j, acc):
            hit = jnp.where(pred(j, strip_ref[j]), 1.0, 0.0)
            return acc + _tree_sum([hit[r:r + 8, :] for r in range(0, TB, 8)])
        acc = lax.fori_loop(0, nblk, body, jnp.zeros((8, TB), F32))
        return jnp.sum(acc, axis=0, keepdims=True)

    def unsettled(cnt_at_prefix):
        return jnp.max(jnp.where(cnt_at_prefix == topk, 0.0, 1.0))

    def bit_body(c):
        it, _, prefix, cnt_at_prefix = c
        cand = prefix | jnp.left_shift(jnp.int32(1), 31 - it)
        t = _key_to_float(cand)
        cnt = count(lambda j, blk: blk >= t)
        take = cnt >= topk
        prefix = jnp.where(take, cand, prefix)
        cnt_at_prefix = jnp.where(take, cnt, cnt_at_prefix)
        return it + 1, unsettled(cnt_at_prefix), prefix, cnt_at_prefix

    n_all = jnp.full((1, TB), 1.0, F32) * (nblk * TB).astype(F32)

    def count_high(t_bf):
        def body(j, acc):
            hit = jnp.where(strip_hi_ref[j] >= t_bf, jnp.ones((), BF16), jnp.zeros((), BF16))
            part = _tree_sum([hit[r:r + BF16_ROWS, :] for r in range(0, TB, BF16_ROWS)])
            return acc + part.astype(F32)
        acc = lax.fori_loop(0, nblk, body, jnp.zeros((BF16_ROWS, TB), F32))
        return jnp.sum(acc, axis=0, keepdims=True)

    def high_body(c):
        it, _, prefix, cnt_at_prefix = c
        cand = prefix | jnp.left_shift(jnp.int32(1), 31 - it)
        t = lax.bitcast_convert_type(
            lax.bitcast_convert_type(_key_to_float(cand), jnp.int32) & HIGH_HALF, F32)
        cnt = count_high(t.astype(BF16))
        take = cnt >= topk
        prefix = jnp.where(take, cand, prefix)
        cnt_at_prefix = jnp.where(take, cnt, cnt_at_prefix)
        return it + 1, unsettled(cnt_at_prefix), prefix, cnt_at_prefix

    init = (jnp.int32(0), unsettled(n_all), jnp.zeros((1, TB), jnp.int32), n_all)
    state = lax.while_loop(lambda c: jnp.logical_and(c[0] < 16, c[1] > 0.0), high_body, init)
    _, _, prefix, cnt_ge = lax.while_loop(
        lambda c: jnp.logical_and(c[0] < 32, c[1] > 0.0), bit_body, state)
    thr = _key_to_float(prefix)

    @pl.when(jnp.max(cnt_ge) > topk)
    def _():
        need = topk - count(lambda j, blk: blk > thr)
        krow = lax.broadcasted_iota(jnp.int32, (TB, TB), 0)

        def idx_body(it, lim):
            cand = lim | jnp.left_shift(jnp.int32(1), idx_bits - 1 - it)
            cnt = count(lambda j, blk: jnp.logical_and(blk == thr, krow + j * TB < cand))
            return jnp.where(cnt <= need, cand, lim)

        lim = lax.fori_loop(0, idx_bits, idx_body, jnp.zeros((1, TB), jnp.int32))

        def drop_body(j, carry):
            blk = strip_ref[j]
            drop = jnp.logical_and(blk == thr, krow + j * TB >= lim)
            strip_ref[j] = jnp.where(drop, NEG_INF, blk)
            return carry

        lax.fori_loop(0, nblk, drop_body, 0)

    qs_ref = scratch[0]
    _stack_heads(q_ref, qs_ref)

    def consume_args(j, diag):
        def mask_fn(st):
            blk = strip_ref[j]
            sel = blk >= thr
            if diag:
                sel = jnp.logical_and(sel, blk > NEG_INF)
            return jnp.concatenate(
                [jnp.where(sel, _head(st, h), NEG_INF) for h in range(GROUP_HEADS)], axis=1)
        return dict(mask_fn=mask_fn)

    _attend_sequence(lambda j: _pair_scores(k_ref, qs_ref, j), vt_ref, o_ref, scratch,
                     i, lambda s: s, consume_args)


def _dsa(qkv, vt, wrow, batch, seq):
    nkb = seq // TB
    topk = min(DSA_TOPK, seq // 4)
    kern = functools.partial(_dsa_kernel, topk=float(topk), idx_bits=int(seq).bit_length())
    return pl.pallas_call(
        kern,
        grid=(batch, seq // TB),
        in_specs=_attn_specs(seq, (("q", CB_QB), ("k", CB_KB))) + [_vt_spec(nkb)]
        + _attn_specs(seq, (("q", CB_QI), ("k", CB_KI8)))
        + [pl.BlockSpec((None, IDX_HEADS, TB), lambda b, i: (b, 0, i))],
        out_specs=_o_spec(seq),
        out_shape=jax.ShapeDtypeStruct((batch * seq, GROUP_W), BF16),
        scratch_shapes=[pltpu.VMEM((nkb, TB, TB), F32), pltpu.VMEM((nkb, TB, TB), BF16),
                        pltpu.VMEM((IDX_HEADS, TB, LANES), BF16)] + _attn_scratch(),
        **_ATTN_PARAMS,
    )(qkv, qkv, vt, qkv, qkv, wrow)


DIL_SPAN = max(w for w, _ in DIL_PATTERNS) // TB + 1


def _dilated_multiplicity():
    k = np.arange(TB)[:, None]
    q = np.arange(TB)[None, :]
    out = np.zeros((DIL_SPAN, TB, TB), np.float32)
    for delta in range(DIL_SPAN):
        d = q - k + TB * delta
        for w, r in DIL_PATTERNS:
            out[delta] += ((d >= 0) & (d % r == 0) & (d <= w)).astype(np.float32)
    return out


def _dil_kernel(q_ref, k_ref, vt_ref, mult_ref, o_ref, *scratch):
    i = pl.program_id(1)
    qs_ref = scratch[0]
    _stack_heads(q_ref, qs_ref)
    n = jnp.minimum(i, DIL_SPAN - 1)

    def consume_args(s, last):
        mult = _tile_heads(mult_ref[n - s])
        return dict(mask_fn=lambda st: jnp.where(mult > 0.0, st, NEG_INF), weight=mult)

    _attend_sequence(lambda j: _pair_scores(k_ref, qs_ref, j), vt_ref, o_ref, scratch,
                     n, lambda s: i - n + s, consume_args)


def _dilated(qkv, vt, mult, batch, seq):
    nkb = seq // TB
    return pl.pallas_call(
        _dil_kernel,
        grid=(batch, seq // TB),
        in_specs=_attn_specs(seq, (("q", CB_QC), ("k", CB_KC))) + [
            _vt_spec(nkb),
            pl.BlockSpec((DIL_SPAN, TB, TB), lambda b, i: (0, 0, 0)),
        ],
        out_specs=_o_spec(seq),
        out_shape=jax.ShapeDtypeStruct((batch * seq, GROUP_W), BF16),
        scratch_shapes=_attn_scratch(),
        **_ATTN_PARAMS,
    )(qkv, qkv, vt, mult)


def _moba_kernel(q_ref, k_ref, vt_ref, o_ref, kmean_ref, sel_ref, *scratch, nkb, nkb_pad, topn):
    i = pl.program_id(1)

    @pl.when(i == 0)
    def _():
        kmean_ref[...] = jnp.zeros_like(kmean_ref)
        for n in range(nkb):
            kb = k_ref[n * TB:(n + 1) * TB, :].astype(F32)
            kmean_ref[n:n + 1, :] = jnp.mean(kb, axis=0, keepdims=True)

    qs_ref = scratch[0]
    _stack_heads(q_ref, qs_ref)
    blk_id = lax.broadcasted_iota(jnp.int32, (nkb_pad, HEADS_W), 0)
    blk_f = blk_id.astype(F32)
    g = _dot_nt(kmean_ref[...].astype(BF16), qs_ref[...])
    g = jnp.where(blk_id < i, g, NEG_INF)
    sel = jnp.zeros_like(g)
    for _ in range(topn):
        mx = jnp.max(g, axis=0, keepdims=True)
        is_max = jnp.logical_and(g == mx, mx > NEG_INF)
        first = jnp.min(jnp.where(is_max, blk_f, float(nkb_pad)), axis=0, keepdims=True)
        pick = blk_f == first
        sel = jnp.where(pick, 1.0, sel)
        g = jnp.where(pick, NEG_INF, g)
    sel_ref[...] = sel

    def consume_args(j, diag):
        if diag:
            return dict(mask_fn=lambda st: jnp.where(_causal_mask(), st, NEG_INF))
        return dict(keep=sel_ref[pl.ds(j, 1), :])

    _attend_sequence(lambda j: _pair_scores(k_ref, qs_ref, j), vt_ref, o_ref, scratch,
                     i, lambda s: s, consume_args)


def _moba(qkv, vt, batch, seq):
    nkb = seq // TB
    nkb_pad = -(-nkb // 8) * 8
    kern = functools.partial(_moba_kernel, nkb=nkb, nkb_pad=nkb_pad, topn=min(MOBA_TOPK, nkb))
    return pl.pallas_call(
        kern,
        grid=(batch, seq // TB),
        in_specs=_attn_specs(seq, (("q", CB_QD), ("k", CB_KD))) + [_vt_spec(nkb)],
        out_specs=_o_spec(seq),
        out_shape=jax.ShapeDtypeStruct((batch * seq, GROUP_W), BF16),
        scratch_shapes=[pltpu.VMEM((nkb_pad, GROUP_W), F32),
                        pltpu.VMEM((nkb_pad, HEADS_W), F32)] + _attn_scratch(),
        compiler_params=pltpu.CompilerParams(
            dimension_semantics=("arbitrary", "arbitrary"), vmem_limit_bytes=VMEM_LIMIT),
    )(qkv, qkv, vt)


def _outproj_kernel(oa_ref, ob_ref, oc_ref, od_ref, w_ref, x_ref, g_ref, b_ref, y_ref, *, alpha):
    acc = alpha * x_ref[...]
    for gi, o_ref in enumerate((oa_ref, ob_ref, oc_ref, od_ref)):
        acc = acc + jnp.dot(o_ref[...], w_ref[gi * GROUP_W:(gi + 1) * GROUP_W, :],
                            preferred_element_type=F32)
    y_ref[...] = _layer_norm(acc, g_ref[...], b_ref[...])


def _outproj(os4, w_o, x2d, g, b, alpha):
    m, d = x2d.shape
    tm = min(512, m)
    row = lambda i: (i, 0)
    fixed = lambda i: (0, 0)
    return pl.pallas_call(
        functools.partial(_outproj_kernel, alpha=alpha),
        grid=(m // tm,),
        in_specs=[pl.BlockSpec((tm, GROUP_W), row)] * 4 + [
            pl.BlockSpec((d, d), fixed), pl.BlockSpec((tm, d), row),
            pl.BlockSpec((1, d), fixed), pl.BlockSpec((1, d), fixed)],
        out_specs=pl.BlockSpec((tm, d), row),
        out_shape=jax.ShapeDtypeStruct((m, d), F32),
        compiler_params=pltpu.CompilerParams(
            dimension_semantics=("parallel",), vmem_limit_bytes=VMEM_LIMIT),
    )(*os4, w_o, x2d, g, b)


def _mlp_kernel(x_ref, wu_ref, wd_ref, g_ref, b_ref, y_ref, xb_ref, acc_ref, *, alpha):
    f = pl.program_id(1)

    @pl.when(f == 0)
    def _():
        x = x_ref[...]
        xb_ref[...] = x.astype(BF16)
        acc_ref[...] = alpha * x

    h = jnp.dot(xb_ref[...], wu_ref[...], preferred_element_type=F32)
    h = jnp.square(jnp.maximum(h, 0.0))
    acc_ref[...] += jnp.dot(h.astype(BF16), wd_ref[...], preferred_element_type=F32)

    @pl.when(f == pl.num_programs(1) - 1)
    def _():
        y_ref[...] = _layer_norm(acc_ref[...], g_ref[...], b_ref[...])


def _mlp(x2d, w_up, w_down, g, b, alpha):
    m, d = x2d.shape
    dff = w_up.shape[1]
    tm = min(1024, m)
    tf = min(512, dff)
    return pl.pallas_call(
        functools.partial(_mlp_kernel, alpha=alpha),
        grid=(m // tm, dff // tf),
        in_specs=[
            pl.BlockSpec((tm, d), lambda i, f: (i, 0)),
            pl.BlockSpec((d, tf), lambda i, f: (0, f)),
            pl.BlockSpec((tf, d), lambda i, f: (f, 0)),
            pl.BlockSpec((1, d), lambda i, f: (0, 0)),
            pl.BlockSpec((1, d), lambda i, f: (0, 0)),
        ],
        out_specs=pl.BlockSpec((tm, d), lambda i, f: (i, 0)),
        out_shape=jax.ShapeDtypeStruct((m, d), F32),
        scratch_shapes=[pltpu.VMEM((tm, d), BF16), pltpu.VMEM((tm, d), F32)],
        compiler_params=pltpu.CompilerParams(
            dimension_semantics=("parallel", "arbitrary"), vmem_limit_bytes=VMEM_LIMIT),
    )(x2d, w_up, w_down, g, b)


def _column_plan():
    hq = GROUP_W
    sizes = (hq, hq, hq, GROUP_HEADS,
             hq, hq, hq, IDX_HEADS * IDX_DIM, IDX_DIM, IDX_HEADS,
             hq, hq, hq, hq, hq, hq)
    offs = np.concatenate([[0], np.cumsum(sizes)])
    (qa, ka, va, fa, qb, kb, vb, qi, ki, wi, qc, kc, vc, qd, kd, vd) = [
        np.arange(offs[n], offs[n + 1]) for n in range(len(sizes))]
    blocks = {CB_QA: qa, CB_KA: ka, CB_VA: va, CB_VB: vb, CB_VC: vc, CB_VD: vd,
              CB_QB: qb, CB_KB: kb, CB_QC: qc, CB_KC: kc, CB_QD: qd, CB_KD: kd,
              CB_QI: qi, CB_KI8: np.tile(ki, IDX_HEADS)}
    scale = np.ones(N_COLBLK * GROUP_W, np.float32)
    for cb in (CB_QA, CB_QB, CB_QC, CB_QD):
        scale[cb * GROUP_W:(cb + 1) * GROUP_W] = LOG2E * HEAD_DIM ** -0.5
    return np.concatenate([blocks[cb] for cb in range(N_COLBLK)]), scale, np.concatenate([fa, wi])


def _rope_table(seq, dim):
    half = dim // 2
    lane = np.arange(LANES)
    inv = ROPE_THETA ** (-(jnp.arange(half, dtype=F32)) / half)
    ang = jnp.arange(seq, dtype=F32)[:, None] * inv[None, :]
    idx = lane % half
    sign = np.where((lane % dim) < half, -1.0, 1.0).astype(np.float32)
    return jnp.concatenate([jnp.cos(ang)[:, idx], jnp.sin(ang)[:, idx] * sign[None, :]], axis=1)


def _vt_blocks(qkv, cb, batch, seq):
    nkb = seq // TB
    v = qkv[:, cb * GROUP_W:(cb + 1) * GROUP_W].reshape(batch, nkb, TB, GROUP_HEADS, HEAD_DIM)
    pad = jnp.zeros((batch, nkb, TB, GROUP_HEADS, VROWS - HEAD_DIM), v.dtype).at[..., 0].set(1)
    v = jnp.concatenate([v, pad], axis=-1).reshape(batch, nkb, TB, GROUP_HEADS * VROWS)
    return v.transpose(0, 1, 3, 2)


def kernel(x, w_in, b_f, w_o, ln1_g, ln1_b, w_up, w_down, ln2_g, ln2_b):
    batch, seq, d = x.shape
    depth = w_in.shape[0]
    assert seq % TB == 0 and d == 4 * GROUP_W
    chunks = seq // LANES
    assert chunks & (chunks - 1) == 0
    alpha = (2.0 * depth) ** 0.25

    main_cols, main_scale, misc_cols = _column_plan()
    w_main = (w_in[:, :, main_cols] * main_scale).astype(BF16)
    w_misc = jnp.pad(w_in[:, :, misc_cols], ((0, 0), (0, 0), (0, LANES - len(misc_cols)))).astype(BF16)
    w_o16, w_up16, w_down16 = w_o.astype(BF16), w_up.astype(BF16), w_down.astype(BF16)
    tab64 = _rope_table(seq, HEAD_DIM)
    tab32 = _rope_table(seq, IDX_DIM)
    mult = jnp.asarray(_dilated_multiplicity())

    x2d = x.reshape(batch * seq, d)
    for l in range(depth):
        qkv, misc = _inproj(x2d, w_main[l], w_misc[l], tab64, tab32, seq)
        misc3 = misc.reshape(batch, seq, LANES)
        z = misc3[:, :, 0:GROUP_HEADS].transpose(0, 2, 1).reshape(batch, GROUP_HEADS * chunks, LANES)
        bias_rows = jnp.repeat(b_f[l].astype(F32), chunks)[:, None]
        gb = _gate_bias(z, bias_rows, chunks)
        gb = gb.reshape(batch, BIAS_TERMS, GROUP_HEADS, seq).transpose(0, 3, 2, 1)
        cbias = jnp.pad(gb.reshape(batch * seq, GROUP_HEADS * BIAS_TERMS),
                        ((0, 0), (0, LANES - GROUP_HEADS * BIAS_TERMS))).astype(BF16)
        wrow = misc3[:, :, GROUP_HEADS:GROUP_HEADS + IDX_HEADS].transpose(0, 2, 1)

        oa = _fox(qkv, cbias, _vt_blocks(qkv, CB_VA, batch, seq), batch, seq)
        ob = _dsa(qkv, _vt_blocks(qkv, CB_VB, batch, seq), wrow, batch, seq)
        oc = _dilated(qkv, _vt_blocks(qkv, CB_VC, batch, seq), mult, batch, seq)
        od = _moba(qkv, _vt_blocks(qkv, CB_VD, batch, seq), batch, seq)

        x2d = _outproj((oa, ob, oc, od), w_o16[l], x2d, ln1_g[l][None, :], ln1_b[l][None, :], alpha)
        x2d = _mlp(x2d, w_up16[l], w_down16[l], ln2_g[l][None, :], ln2_b[l][None, :], alpha)
    return x2d.reshape(batch, seq, d)
```

```python
import functools
import math

import numpy as np
import jax
import jax.numpy as jnp
from jax import lax
from jax.experimental import pallas as pl
from jax.experimental.pallas import tpu as pltpu

F32 = jnp.float32
BF16 = jnp.bfloat16

HEAD_DIM = 64
GROUP_HEADS = 4
GROUP_W = GROUP_HEADS * HEAD_DIM
IDX_HEADS = 8
IDX_DIM = 32
DSA_TOPK = 256
DIL_PATTERNS = ((128, 1), (512, 4), (2048, 16))
MOBA_BLOCK = 256
MOBA_TOPK = 3
ROPE_THETA = 10000.0
LN_EPS = 1e-5
LOG2E = math.log2(math.e)
LANES = 128
BF16_ROWS = 16
TB = 256
VROWS = HEAD_DIM + BF16_ROWS
NEG_INF = float("-inf")
VMEM_LIMIT = 56 * 1024 * 1024

(CB_QA, CB_KA, CB_VA, CB_VB, CB_VC, CB_VD,
 CB_QB, CB_KB, CB_QC, CB_KC, CB_QD, CB_KD,
 CB_QI, CB_KI8) = range(14)
N_COLBLK = 14
PROJ_TN = 2 * GROUP_W
PLAIN_STEPS, ROPE64_STEPS = 3, 3


def _dot_nt(a, b):
    return lax.dot_general(a, b, (((1,), (1,)), ((), ())), preferred_element_type=F32)


def _layer_norm(y, g, b):
    mu = jnp.mean(y, axis=-1, keepdims=True)
    d = y - mu
    var = jnp.mean(d * d, axis=-1, keepdims=True)
    return d * lax.rsqrt(var + LN_EPS) * g + b


def _tree_sum(parts):
    while len(parts) > 1:
        parts = [parts[n] + parts[n + 1] for n in range(0, len(parts), 2)]
    return parts[0]


def _rope(acc, tab_ref, half):
    cos = tab_ref[:, 0:LANES]
    sin = tab_ref[:, LANES:2 * LANES]
    lane = lax.broadcasted_iota(jnp.int32, (1, LANES), 1)
    first = (lane & (2 * half - 1)) < half
    outs = []
    for c in range(acc.shape[1] // LANES):
        a = acc[:, c * LANES:(c + 1) * LANES]
        swapped = jnp.where(first, pltpu.roll(a, LANES - half, 1), pltpu.roll(a, half, 1))
        outs.append(a * cos + swapped * sin)
    return jnp.concatenate(outs, axis=1)


def _inproj_kernel(x_ref, w_ref, wm_ref, t64_ref, t32_ref, o_ref, misc_ref, xb_ref):
    j = pl.program_id(1)

    @pl.when(j == 0)
    def _():
        xb = x_ref[...].astype(BF16)
        xb_ref[...] = xb
        misc_ref[...] = jnp.dot(xb, wm_ref[...], preferred_element_type=F32)

    acc = jnp.dot(xb_ref[...], w_ref[...], preferred_element_type=F32)

    @pl.when(j < PLAIN_STEPS)
    def _():
        o_ref[...] = acc.astype(BF16)

    @pl.when(jnp.logical_and(j >= PLAIN_STEPS, j < PLAIN_STEPS + ROPE64_STEPS))
    def _():
        o_ref[...] = _rope(acc, t64_ref, HEAD_DIM // 2).astype(BF16)

    @pl.when(j == PLAIN_STEPS + ROPE64_STEPS)
    def _():
        o_ref[...] = _rope(acc, t32_ref, IDX_DIM // 2).astype(BF16)


def _inproj(x2d, w_main, w_misc, tab64, tab32, seq):
    m, d = x2d.shape
    tm = min(1024, seq)
    nt = seq // tm
    return pl.pallas_call(
        _inproj_kernel,
        grid=(m // tm, N_COLBLK * GROUP_W // PROJ_TN),
        in_specs=[
            pl.BlockSpec((tm, d), lambda i, j: (i, 0)),
            pl.BlockSpec((d, PROJ_TN), lambda i, j: (0, j)),
            pl.BlockSpec((d, LANES), lambda i, j: (0, 0)),
            pl.BlockSpec((tm, 2 * LANES), lambda i, j: (i % nt, 0)),
            pl.BlockSpec((tm, 2 * LANES), lambda i, j: (i % nt, 0)),
        ],
        out_specs=[
            pl.BlockSpec((tm, PROJ_TN), lambda i, j: (i, j)),
            pl.BlockSpec((tm, LANES), lambda i, j: (i, 0)),
        ],
        out_shape=[
            jax.ShapeDtypeStruct((m, N_COLBLK * GROUP_W), BF16),
            jax.ShapeDtypeStruct((m, LANES), F32),
        ],
        scratch_shapes=[pltpu.VMEM((tm, d), BF16)],
        compiler_params=pltpu.CompilerParams(
            dimension_semantics=("parallel", "arbitrary"), vmem_limit_bytes=VMEM_LIMIT),
    )(x2d, w_main, w_misc, tab64, tab32)


def _gate_kernel(z_ref, b_ref, o_ref, *, chunks):
    z = z_ref[...] + b_ref[...]
    w = jnp.minimum(z, 0.0) - jnp.log1p(jnp.exp(-jnp.abs(z)))
    rows = w.shape[0]
    lane = lax.broadcasted_iota(jnp.int32, (rows, LANES), 1)
    s = 1
    while s < LANES:
        w = w + jnp.where(lane >= s, pltpu.roll(w, s, 1), 0.0)
        s *= 2
    tot = jnp.broadcast_to(w[:, LANES - 1:LANES], (rows, LANES))
    pos = lax.broadcasted_iota(jnp.int32, (rows, LANES), 0) & (chunks - 1)
    run = tot
    s = 1
    while s < chunks:
        run = run + jnp.where(pos >= s, pltpu.roll(run, s, 0), 0.0)
        s *= 2
    bias = -LOG2E * (w + (run - tot))
    hi = bias.astype(BF16).astype(F32)
    mid = (bias - hi).astype(BF16).astype(F32)
    lo = (bias - hi - mid).astype(BF16).astype(F32)
    o_ref[0] = hi
    o_ref[1] = mid
    o_ref[2] = lo


def _gate_bias(z, bias_rows, chunks):
    b, rows, _ = z.shape
    return pl.pallas_call(
        functools.partial(_gate_kernel, chunks=chunks),
        grid=(b,),
        in_specs=[pl.BlockSpec((None, rows, LANES), lambda i: (i, 0, 0)),
                  pl.BlockSpec((rows, 1), lambda i: (0, 0))],
        out_specs=pl.BlockSpec((None, 3, rows, LANES), lambda i: (i, 0, 0, 0)),
        out_shape=jax.ShapeDtypeStruct((b, 3, rows, LANES), F32),
        compiler_params=pltpu.CompilerParams(dimension_semantics=("parallel",)),
    )(z, bias_rows)


HEADS_W = GROUP_HEADS * TB


def _stack_heads(q_ref, qs_ref):
    q = q_ref[...].astype(F32)
    head_of_lane = lax.broadcasted_iota(jnp.int32, (1, GROUP_W), 1) >> 6
    for h in range(GROUP_HEADS):
        qs_ref[h * TB:(h + 1) * TB, 0:GROUP_W] = jnp.where(head_of_lane == h, q, 0.0).astype(BF16)


def _head(x, h):
    return x[:, h * TB:(h + 1) * TB]


def _tile_heads(x):
    return jnp.concatenate([x] * GROUP_HEADS, axis=1)


def _kblock(k_ref, j):
    return k_ref[pl.ds(pl.multiple_of(j * TB, TB), TB), :]


def _pair_scores(k_ref, qs_ref, j, extra_ref=None):
    kblk = _kblock(k_ref, j)
    outs = []
    for pr in range(GROUP_W // LANES):
        rows = slice(2 * pr * TB, 2 * (pr + 1) * TB)
        kp = kblk[:, pr * LANES:(pr + 1) * LANES]
        qp = qs_ref[rows, pr * LANES:(pr + 1) * LANES]
        if extra_ref is not None:
            kp = jnp.concatenate([kp, _kblock(extra_ref, j)], axis=1)
            qp = jnp.concatenate([qp, qs_ref[rows, GROUP_W:GROUP_W + LANES]], axis=1)
        outs.append(_dot_nt(kp, qp))
    return jnp.concatenate(outs, axis=1)


def _causal_mask(heads=GROUP_HEADS):
    krow = lax.broadcasted_iota(jnp.int32, (TB, heads * TB), 0)
    qcol = lax.broadcasted_iota(jnp.int32, (TB, heads * TB), 1) & (TB - 1)
    return krow <= qcol


def _consume(st, vt, m, accs, mask_fn=None, weight=None, keep=None):
    if mask_fn is not None:
        st = mask_fn(st)
    blk_max = jnp.max(st, axis=0, keepdims=True)
    m_new = jnp.maximum(m, blk_max)
    if keep is not None:
        m_new = jnp.where(keep > 0.5, m_new, m)
    m_safe = jnp.where(m_new == NEG_INF, 0.0, m_new)
    alpha = jnp.exp2(m - m_safe)
    p = jnp.exp2(st - m_safe)
    if weight is not None:
        p = p * weight
    pb = p.astype(BF16)
    new_accs = []
    for h in range(GROUP_HEADS):
        pv = jnp.dot(vt[h * VROWS:(h + 1) * VROWS, :], _head(pb, h), preferred_element_type=F32)
        upd = _head(alpha, h) * accs[h] + pv
        new_accs.append(upd if keep is None else jnp.where(_head(keep, h) > 0.5, upd, accs[h]))
    return m_new, new_accs


TRIP_SHIFT = 2
STEPS_PER_TRIP = 1 << TRIP_SHIFT


def _attn_scratch(qs_width=GROUP_W):
    return [
        pltpu.VMEM((HEADS_W, qs_width), BF16),
        pltpu.VMEM((2, TB, HEADS_W), F32),
        pltpu.VMEM((1, HEADS_W), F32),
        pltpu.VMEM((GROUP_HEADS * VROWS, TB), F32),
    ]


def _attend_sequence(scores_of, vt_ref, o_ref, scratch, n_steps, block_of, consume_args):
    _, st_ref, m_ref, acc_ref = scratch

    def issue(s, slot):
        st_ref[slot] = scores_of(block_of(s))

    def consume(s, slot, last):
        accs = [acc_ref[h * VROWS:(h + 1) * VROWS, :] for h in range(GROUP_HEADS)]
        m, accs = _consume(st_ref[slot], vt_ref[block_of(s)], m_ref[...], accs,
                           **consume_args(s, last))
        m_ref[...] = m
        for h in range(GROUP_HEADS):
            acc_ref[h * VROWS:(h + 1) * VROWS, :] = accs[h]

    m_ref[...] = jnp.full((1, HEADS_W), NEG_INF, F32)
    acc_ref[...] = jnp.zeros_like(acc_ref)
    issue(0, 0)

    def quad(qq, carry):
        s = STEPS_PER_TRIP * qq
        for u in range(STEPS_PER_TRIP):
            issue(s + u + 1, (u + 1) & 1)
            consume(s + u, u & 1, False)
        return carry

    n_quads = n_steps >> TRIP_SHIFT
    lax.fori_loop(0, n_quads, quad, 0)

    @pl.when((n_steps & 2) != 0)
    def _():
        s = n_quads * STEPS_PER_TRIP
        issue(s + 1, 1)
        consume(s, 0, False)
        issue(s + 2, 0)
        consume(s + 1, 1, False)

    @pl.when((n_steps & 1) == 1)
    def _():
        issue(n_steps, 1)
        consume(n_steps - 1, 0, False)
        consume(n_steps, 1, True)

    @pl.when((n_steps & 1) == 0)
    def _():
        consume(n_steps, 0, True)

    ot = jnp.concatenate(
        [acc_ref[h * VROWS:h * VROWS + HEAD_DIM, :] / acc_ref[h * VROWS + HEAD_DIM:h * VROWS + HEAD_DIM + 1, :]
         for h in range(GROUP_HEADS)], axis=0)
    o_ref[...] = ot.T.astype(o_ref.dtype)


def _attn_specs(seq, colblks):
    nq = seq // TB
    specs = []
    for kind, cb in colblks:
        if kind == "q":
            specs.append(pl.BlockSpec((TB, GROUP_W), lambda b, i, cb=cb: (b * nq + i, cb)))
        else:
            specs.append(pl.BlockSpec((seq, GROUP_W), lambda b, i, cb=cb: (b, cb)))
    return specs


def _vt_spec(nkb):
    return pl.BlockSpec((None, nkb, GROUP_HEADS * VROWS, TB), lambda b, i: (b, 0, 0, 0))


def _o_spec(seq):
    nq = seq // TB
    return pl.BlockSpec((TB, GROUP_W), lambda b, i: (b * nq + i, 0))


_ATTN_PARAMS = dict(
    compiler_params=pltpu.CompilerParams(
        dimension_semantics=("parallel", "arbitrary"), vmem_limit_bytes=VMEM_LIMIT))


BIAS_TERMS = 3


def _fox_kernel(q_ref, k_ref, cb_ref, vt_ref, o_ref, *scratch):
    i = pl.program_id(1)
    qs_ref = scratch[0]
    _stack_heads(q_ref, qs_ref)
    lane = lax.broadcasted_iota(jnp.int32, (TB, LANES), 1)
    for h in range(GROUP_HEADS):
        pick = jnp.logical_and(lane >= BIAS_TERMS * h, lane < BIAS_TERMS * (h + 1))
        qs_ref[h * TB:(h + 1) * TB, GROUP_W:GROUP_W + LANES] = jnp.where(pick, 1.0, 0.0).astype(BF16)

    def scores_of(j):
        return _pair_scores(k_ref, qs_ref, j, cb_ref)

    def consume_args(j, diag):
        if diag:
            return dict(mask_fn=lambda st: jnp.where(_causal_mask(), st, NEG_INF))
        return {}

    _attend_sequence(scores_of, vt_ref, o_ref, scratch, i, lambda s: s, consume_args)


def _fox(qkv, cbias, vt, batch, seq):
    nkb = seq // TB
    return pl.pallas_call(
        _fox_kernel,
        grid=(batch, seq // TB),
        in_specs=_attn_specs(seq, (("q", CB_QA), ("k", CB_KA))) + [
            pl.BlockSpec((seq, LANES), lambda b, i: (b, 0)),
            _vt_spec(nkb),
        ],
        out_specs=_o_spec(seq),
        out_shape=jax.ShapeDtypeStruct((batch * seq, GROUP_W), BF16),
        scratch_shapes=_attn_scratch(GROUP_W + LANES),
        **_ATTN_PARAMS,
    )(qkv, qkv, cbias, vt)


HIGH_HALF = -(1 << 16)


def _key_to_float(key_u):
    key = key_u ^ jnp.int32(-2 ** 31)
    bits = jnp.where(key < 0, key ^ jnp.int32(0x7FFFFFFF), key)
    below_neg_inf = jnp.logical_and(key_u >= 0, key_u < jnp.int32(0x007FFFFF))
    return jnp.where(below_neg_inf, NEG_INF, lax.bitcast_convert_type(bits, F32))


def _dsa_kernel(q_ref, k_ref, vt_ref, qi_ref, k8_ref, w_ref, o_ref, strip_ref, strip_hi_ref, qm_ref, *scratch,
                topk, idx_bits):
    i = pl.program_id(1)
    nblk = i + 1

    qi = qi_ref[...].astype(F32)
    low_lanes = lax.broadcasted_iota(jnp.int32, (1, LANES), 1) < IDX_DIM
    per_half = LANES // IDX_DIM
    for h in range(IDX_HEADS):
        half = qi[:, (h // per_half) * LANES:(h // per_half + 1) * LANES]
        if h % per_half:
            half = pltpu.roll(half, LANES - IDX_DIM * (h % per_half), 1)
        qm_ref[h] = jnp.where(low_lanes, half, 0.0).astype(BF16)

    wrow = w_ref[...]

    def score_block(j, diag):
        kblk = _kblock(k8_ref, j)[:, 0:LANES]
        sc = _tree_sum([wrow[h:h + 1, :] * jnp.maximum(_dot_nt(kblk, qm_ref[h]), 0.0)
                        for h in range(IDX_HEADS)])
        if diag:
            sc = jnp.where(_causal_mask(1), sc, NEG_INF)
        strip_ref[j] = sc
        strip_hi_ref[j] = lax.bitcast_convert_type(
            lax.bitcast_convert_type(sc, jnp.int32) & HIGH_HALF, F32).astype(BF16)

    def score_pair(p, carry):
        score_block(2 * p, False)
        score_block(2 * p + 1, False)
        return carry

    lax.fori_loop(0, i >> 1, score_pair, 0)

    @pl.when((i & 1) == 1)
    def _():
        score_block(i - 1, False)

    score_block(i, True)

    def count(pred):
        def body(j, acc):
            hit = jnp.where(pred(j, strip_ref[j]), 1.0, 0.0)
            return acc + _tree_sum([hit[r:r + 8, :] for r in range(0, TB, 8)])
        acc = lax.fori_loop(0, nblk, body, jnp.zeros((8, TB), F32))
        return jnp.sum(acc, axis=0, keepdims=True)

    def count_ge(ref, t, rows):
        one, zero = jnp.ones((), ref.dtype), jnp.zeros((), ref.dtype)

        def hits(j, tt):
            hit = jnp.where(ref[j] >= tt, one, zero)
            return _tree_sum([hit[r:r + rows, :] for r in range(0, TB, rows)]).astype(F32)

        def body(p, acc):
            j1 = 2 * p + 1
            t1 = jnp.where(j1 < nblk, t, jnp.full_like(t, jnp.inf))
            return acc + hits(2 * p, t) + hits(jnp.minimum(j1, nblk - 1), t1)

        acc = lax.fori_loop(0, (nblk + 1) >> 1, body, jnp.zeros((rows, TB), F32))
        return jnp.sum(acc, axis=0, keepdims=True)

    def unsettled(cnt_at_prefix):
        return jnp.max(jnp.where(cnt_at_prefix == topk, 0.0, 1.0))

    def bit_body(c):
        it, _, prefix, cnt_at_prefix = c
        cand = prefix | jnp.left_shift(jnp.int32(1), 31 - it)
        t = _key_to_float(cand)
        cnt = count_ge(strip_ref, t, 8)
        take = cnt >= topk
        prefix = jnp.where(take, cand, prefix)
        cnt_at_prefix = jnp.where(take, cnt, cnt_at_prefix)
        return it + 1, unsettled(cnt_at_prefix), prefix, cnt_at_prefix

    n_all = jnp.full((1, TB), 1.0, F32) * (nblk * TB).astype(F32)

    def high_body(c):
        it, _, prefix, cnt_at_prefix = c
        cand = prefix | jnp.left_shift(jnp.int32(1), 31 - it)
        t = lax.bitcast_convert_type(
            lax.bitcast_convert_type(_key_to_float(cand), jnp.int32) & HIGH_HALF, F32)
        cnt = count_ge(strip_hi_ref, t.astype(BF16), BF16_ROWS)
        take = cnt >= topk
        prefix = jnp.where(take, cand, prefix)
        cnt_at_prefix = jnp.where(take, cnt, cnt_at_prefix)
        return it + 1, unsettled(cnt_at_prefix), prefix, cnt_at_prefix

    init = (jnp.int32(0), unsettled(n_all), jnp.zeros((1, TB), jnp.int32), n_all)
    state = lax.while_loop(lambda c: jnp.logical_and(c[0] < 16, c[1] > 0.0), high_body, init)
    _, _, prefix, cnt_ge = lax.while_loop(
        lambda c: jnp.logical_and(c[0] < 32, c[1] > 0.0), bit_body, state)
    thr = _key_to_float(prefix)

    @pl.when(jnp.max(cnt_ge) > topk)
    def _():
        need = topk - count(lambda j, blk: blk > thr)
        krow = lax.broadcasted_iota(jnp.int32, (TB, TB), 0)

        def idx_body(it, lim):
            cand = lim | jnp.left_shift(jnp.int32(1), idx_bits - 1 - it)
            cnt = count(lambda j, blk: jnp.logical_and(blk == thr, krow + j * TB < cand))
            return jnp.where(cnt <= need, cand, lim)

        lim = lax.fori_loop(0, idx_bits, idx_body, jnp.zeros((1, TB), jnp.int32))

        def drop_body(j, carry):
            blk = strip_ref[j]
            drop = jnp.logical_and(blk == thr, krow + j * TB >= lim)
            strip_ref[j] = jnp.where(drop, NEG_INF, blk)
            return carry

        lax.fori_loop(0, nblk, drop_body, 0)

    qs_ref = scratch[0]
    _stack_heads(q_ref, qs_ref)

    def consume_args(j, diag):
        def mask_fn(st):
            blk = strip_ref[j]
            sel = blk >= thr
            if diag:
                sel = jnp.logical_and(sel, blk > NEG_INF)
            return jnp.concatenate(
                [jnp.where(sel, _head(st, h), NEG_INF) for h in range(GROUP_HEADS)], axis=1)
        return dict(mask_fn=mask_fn)

    _attend_sequence(lambda j: _pair_scores(k_ref, qs_ref, j), vt_ref, o_ref, scratch,
                     i, lambda s: s, consume_args)


def _dsa(qkv, vt, wrow, batch, seq):
    nkb = seq // TB
    topk = min(DSA_TOPK, seq // 4)
    kern = functools.partial(_dsa_kernel, topk=float(topk), idx_bits=int(seq).bit_length())
    return pl.pallas_call(
        kern,
        grid=(batch, seq // TB),
        in_specs=_attn_specs(seq, (("q", CB_QB), ("k", CB_KB))) + [_vt_spec(nkb)]
        + _attn_specs(seq, (("q", CB_QI), ("k", CB_KI8)))
        + [pl.BlockSpec((None, IDX_HEADS, TB), lambda b, i: (b, 0, i))],
        out_specs=_o_spec(seq),
        out_shape=jax.ShapeDtypeStruct((batch * seq, GROUP_W), BF16),
        scratch_shapes=[pltpu.VMEM((nkb, TB, TB), F32), pltpu.VMEM((nkb, TB, TB), BF16),
                        pltpu.VMEM((IDX_HEADS, TB, LANES), BF16)] + _attn_scratch(),
        **_ATTN_PARAMS,
    )(qkv, qkv, vt, qkv, qkv, wrow)


DIL_SPAN = max(w for w, _ in DIL_PATTERNS) // TB + 1


def _dilated_multiplicity():
    k = np.arange(TB)[:, None]
    q = np.arange(TB)[None, :]
    out = np.zeros((DIL_SPAN, TB, TB), np.float32)
    for delta in range(DIL_SPAN):
        d = q - k + TB * delta
        for w, r in DIL_PATTERNS:
            out[delta] += ((d >= 0) & (d % r == 0) & (d <= w)).astype(np.float32)
    return out


def _dil_kernel(q_ref, k_ref, vt_ref, mult_ref, o_ref, *scratch):
    i = pl.program_id(1)
    qs_ref = scratch[0]
    _stack_heads(q_ref, qs_ref)
    n = jnp.minimum(i, DIL_SPAN - 1)

    def consume_args(s, last):
        mult = _tile_heads(mult_ref[n - s])
        return dict(mask_fn=lambda st: jnp.where(mult > 0.0, st, NEG_INF), weight=mult)

    _attend_sequence(lambda j: _pair_scores(k_ref, qs_ref, j), vt_ref, o_ref, scratch,
                     n, lambda s: i - n + s, consume_args)


def _dilated(qkv, vt, mult, batch, seq):
    nkb = seq // TB
    return pl.pallas_call(
        _dil_kernel,
        grid=(batch, seq // TB),
        in_specs=_attn_specs(seq, (("q", CB_QC), ("k", CB_KC))) + [
            _vt_spec(nkb),
            pl.BlockSpec((DIL_SPAN, TB, TB), lambda b, i: (0, 0, 0)),
        ],
        out_specs=_o_spec(seq),
        out_shape=jax.ShapeDtypeStruct((batch * seq, GROUP_W), BF16),
        scratch_shapes=_attn_scratch(),
        **_ATTN_PARAMS,
    )(qkv, qkv, vt, mult)


def _moba_kernel(q_ref, k_ref, vt_ref, o_ref, kmean_ref, sel_ref, *scratch, nkb, nkb_pad, topn):
    i = pl.program_id(1)

    @pl.when(i == 0)
    def _():
        kmean_ref[...] = jnp.zeros_like(kmean_ref)
        for n in range(nkb):
            kb = k_ref[n * TB:(n + 1) * TB, :].astype(F32)
            kmean_ref[n:n + 1, :] = jnp.mean(kb, axis=0, keepdims=True)

    qs_ref = scratch[0]
    _stack_heads(q_ref, qs_ref)
    blk_id = lax.broadcasted_iota(jnp.int32, (nkb_pad, HEADS_W), 0)
    blk_f = blk_id.astype(F32)
    g = _dot_nt(kmean_ref[...].astype(BF16), qs_ref[...])
    g = jnp.where(blk_id < i, g, NEG_INF)
    sel = jnp.zeros_like(g)
    for _ in range(topn):
        mx = jnp.max(g, axis=0, keepdims=True)
        is_max = jnp.logical_and(g == mx, mx > NEG_INF)
        first = jnp.min(jnp.where(is_max, blk_f, float(nkb_pad)), axis=0, keepdims=True)
        pick = blk_f == first
        sel = jnp.where(pick, 1.0, sel)
        g = jnp.where(pick, NEG_INF, g)
    sel_ref[...] = sel

    def consume_args(j, diag):
        if diag:
            return dict(mask_fn=lambda st: jnp.where(_causal_mask(), st, NEG_INF))
        return dict(keep=sel_ref[pl.ds(j, 1), :])

    _attend_sequence(lambda j: _pair_scores(k_ref, qs_ref, j), vt_ref, o_ref, scratch,
                     i, lambda s: s, consume_args)


def _moba(qkv, vt, batch, seq):
    nkb = seq // TB
    nkb_pad = -(-nkb // 8) * 8
    kern = functools.partial(_moba_kernel, nkb=nkb, nkb_pad=nkb_pad, topn=min(MOBA_TOPK, nkb))
    return pl.pallas_call(
        kern,
        grid=(batch, seq // TB),
        in_specs=_attn_specs(seq, (("q", CB_QD), ("k", CB_KD))) + [_vt_spec(nkb)],
        out_specs=_o_spec(seq),
        out_shape=jax.ShapeDtypeStruct((batch * seq, GROUP_W), BF16),
        scratch_shapes=[pltpu.VMEM((nkb_pad, GROUP_W), F32),
                        pltpu.VMEM((nkb_pad, HEADS_W), F32)] + _attn_scratch(),
        compiler_params=pltpu.CompilerParams(
            dimension_semantics=("arbitrary", "arbitrary"), vmem_limit_bytes=VMEM_LIMIT),
    )(qkv, qkv, vt)


def _outproj_kernel(oa_ref, ob_ref, oc_ref, od_ref, w_ref, x_ref, g_ref, b_ref, y_ref, *, alpha):
    acc = alpha * x_ref[...]
    for gi, o_ref in enumerate((oa_ref, ob_ref, oc_ref, od_ref)):
        acc = acc + jnp.dot(o_ref[...], w_ref[gi * GROUP_W:(gi + 1) * GROUP_W, :],
                            preferred_element_type=F32)
    y_ref[...] = _layer_norm(acc, g_ref[...], b_ref[...])


def _outproj(os4, w_o, x2d, g, b, alpha):
    m, d = x2d.shape
    tm = min(512, m)
    row = lambda i: (i, 0)
    fixed = lambda i: (0, 0)
    return pl.pallas_call(
        functools.partial(_outproj_kernel, alpha=alpha),
        grid=(m // tm,),
        in_specs=[pl.BlockSpec((tm, GROUP_W), row)] * 4 + [
            pl.BlockSpec((d, d), fixed), pl.BlockSpec((tm, d), row),
            pl.BlockSpec((1, d), fixed), pl.BlockSpec((1, d), fixed)],
        out_specs=pl.BlockSpec((tm, d), row),
        out_shape=jax.ShapeDtypeStruct((m, d), F32),
        compiler_params=pltpu.CompilerParams(
            dimension_semantics=("parallel",), vmem_limit_bytes=VMEM_LIMIT),
    )(*os4, w_o, x2d, g, b)


def _mlp_kernel(x_ref, wu_ref, wd_ref, g_ref, b_ref, y_ref, xb_ref, acc_ref, *, alpha):
    f = pl.program_id(1)

    @pl.when(f == 0)
    def _():
        x = x_ref[...]
        xb_ref[...] = x.astype(BF16)
        acc_ref[...] = alpha * x

    h = jnp.dot(xb_ref[...], wu_ref[...], preferred_element_type=F32)
    h = jnp.square(jnp.maximum(h, 0.0))
    acc_ref[...] += jnp.dot(h.astype(BF16), wd_ref[...], preferred_element_type=F32)

    @pl.when(f == pl.num_programs(1) - 1)
    def _():
        y_ref[...] = _layer_norm(acc_ref[...], g_ref[...], b_ref[...])


def _mlp(x2d, w_up, w_down, g, b, alpha):
    m, d = x2d.shape
    dff = w_up.shape[1]
    tm = min(1024, m)
    tf = min(512, dff)
    return pl.pallas_call(
        functools.partial(_mlp_kernel, alpha=alpha),
        grid=(m // tm, dff // tf),
        in_specs=[
            pl.BlockSpec((tm, d), lambda i, f: (i, 0)),
            pl.BlockSpec((d, tf), lambda i, f: (0, f)),
            pl.BlockSpec((tf, d), lambda i, f: (f, 0)),
            pl.BlockSpec((1, d), lambda i, f: (0, 0)),
            pl.BlockSpec((1, d), lambda i, f: (0, 0)),
        ],
        out_specs=pl.BlockSpec((tm, d), lambda i, f: (i, 0)),
        out_shape=jax.ShapeDtypeStruct((m, d), F32),
        scratch_shapes=[pltpu.VMEM((tm, d), BF16), pltpu.VMEM((tm, d), F32)],
        compiler_params=pltpu.CompilerParams(
            dimension_semantics=("parallel", "arbitrary"), vmem_limit_bytes=VMEM_LIMIT),
    )(x2d, w_up, w_down, g, b)


def _column_plan():
    hq = GROUP_W
    sizes = (hq, hq, hq, GROUP_HEADS,
             hq, hq, hq, IDX_HEADS * IDX_DIM, IDX_DIM, IDX_HEADS,
             hq, hq, hq, hq, hq, hq)
    offs = np.concatenate([[0], np.cumsum(sizes)])
    (qa, ka, va, fa, qb, kb, vb, qi, ki, wi, qc, kc, vc, qd, kd, vd) = [
        np.arange(offs[n], offs[n + 1]) for n in range(len(sizes))]
    blocks = {CB_QA: qa, CB_KA: ka, CB_VA: va, CB_VB: vb, CB_VC: vc, CB_VD: vd,
              CB_QB: qb, CB_KB: kb, CB_QC: qc, CB_KC: kc, CB_QD: qd, CB_KD: kd,
              CB_QI: qi, CB_KI8: np.tile(ki, IDX_HEADS)}
    scale = np.ones(N_COLBLK * GROUP_W, np.float32)
    for cb in (CB_QA, CB_QB, CB_QC, CB_QD):
        scale[cb * GROUP_W:(cb + 1) * GROUP_W] = LOG2E * HEAD_DIM ** -0.5
    return np.concatenate([blocks[cb] for cb in range(N_COLBLK)]), scale, np.concatenate([fa, wi])


def _rope_table(seq, dim):
    half = dim // 2
    lane = np.arange(LANES)
    inv = ROPE_THETA ** (-(jnp.arange(half, dtype=F32)) / half)
    ang = jnp.arange(seq, dtype=F32)[:, None] * inv[None, :]
    idx = lane % half
    sign = np.where((lane % dim) < half, -1.0, 1.0).astype(np.float32)
    return jnp.concatenate([jnp.cos(ang)[:, idx], jnp.sin(ang)[:, idx] * sign[None, :]], axis=1)


def _vt_blocks(qkv, cb, batch, seq):
    nkb = seq // TB
    v = qkv[:, cb * GROUP_W:(cb + 1) * GROUP_W].reshape(batch, nkb, TB, GROUP_HEADS, HEAD_DIM)
    pad = jnp.zeros((batch, nkb, TB, GROUP_HEADS, VROWS - HEAD_DIM), v.dtype).at[..., 0].set(1)
    v = jnp.concatenate([v, pad], axis=-1).reshape(batch, nkb, TB, GROUP_HEADS * VROWS)
    return v.transpose(0, 1, 3, 2)


def kernel(x, w_in, b_f, w_o, ln1_g, ln1_b, w_up, w_down, ln2_g, ln2_b):
    batch, seq, d = x.shape
    depth = w_in.shape[0]
    assert seq % TB == 0 and d == 4 * GROUP_W
    chunks = seq // LANES
    assert chunks & (chunks - 1) == 0
    alpha = (2.0 * depth) ** 0.25

    main_cols, main_scale, misc_cols = _column_plan()
    w_main = (w_in[:, :, main_cols] * main_scale).astype(BF16)
    w_misc = jnp.pad(w_in[:, :, misc_cols], ((0, 0), (0, 0), (0, LANES - len(misc_cols)))).astype(BF16)
    w_o16, w_up16, w_down16 = w_o.astype(BF16), w_up.astype(BF16), w_down.astype(BF16)
    tab64 = _rope_table(seq, HEAD_DIM)
    tab32 = _rope_table(seq, IDX_DIM)
    mult = jnp.asarray(_dilated_multiplicity())

    x2d = x.reshape(batch * seq, d)
    for l in range(depth):
        qkv, misc = _inproj(x2d, w_main[l], w_misc[l], tab64, tab32, seq)
        misc3 = misc.reshape(batch, seq, LANES)
        z = misc3[:, :, 0:GROUP_HEADS].transpose(0, 2, 1).reshape(batch, GROUP_HEADS * chunks, LANES)
        bias_rows = jnp.repeat(b_f[l].astype(F32), chunks)[:, None]
        gb = _gate_bias(z, bias_rows, chunks)
        gb = gb.reshape(batch, BIAS_TERMS, GROUP_HEADS, seq).transpose(0, 3, 2, 1)
        cbias = jnp.pad(gb.reshape(batch * seq, GROUP_HEADS * BIAS_TERMS),
                        ((0, 0), (0, LANES - GROUP_HEADS * BIAS_TERMS))).astype(BF16)
        wrow = misc3[:, :, GROUP_HEADS:GROUP_HEADS + IDX_HEADS].transpose(0, 2, 1)

        oa = _fox(qkv, cbias, _vt_blocks(qkv, CB_VA, batch, seq), batch, seq)
        ob = _dsa(qkv, _vt_blocks(qkv, CB_VB, batch, seq), wrow, batch, seq)
        oc = _dilated(qkv, _vt_blocks(qkv, CB_VC, batch, seq), mult, batch, seq)
        od = _moba(qkv, _vt_blocks(qkv, CB_VD, batch, seq), batch, seq)

        x2d = _outproj((oa, ob, oc, od), w_o16[l], x2d, ln1_g[l][None, :], ln1_b[l][None, :], alpha)
        x2d = _mlp(x2d, w_up16[l], w_down16[l], ln2_g[l][None, :], ln2_b[l][None, :], alpha)
    return x2d.reshape(batch, seq, d)
```

```python
import functools
import math

import numpy as np
import jax
import jax.numpy as jnp
from jax import lax
from jax.experimental import pallas as pl
from jax.experimental.pallas import tpu as pltpu

F32 = jnp.float32
BF16 = jnp.bfloat16

HEAD_DIM = 64
GROUP_HEADS = 4
GROUP_W = GROUP_HEADS * HEAD_DIM
IDX_HEADS = 8
IDX_DIM = 32
DSA_TOPK = 256
DIL_PATTERNS = ((128, 1), (512, 4), (2048, 16))
MOBA_BLOCK = 256
MOBA_TOPK = 3
ROPE_THETA = 10000.0
LN_EPS = 1e-5
LOG2E = math.log2(math.e)
LANES = 128
BF16_ROWS = 16
TB = 256
VROWS = HEAD_DIM + BF16_ROWS
NEG_INF = float("-inf")
VMEM_LIMIT = 56 * 1024 * 1024

(CB_QA, CB_KA, CB_VA, CB_VB, CB_VC, CB_VD,
 CB_QB, CB_KB, CB_QC, CB_KC, CB_QD, CB_KD,
 CB_QI, CB_KI8) = range(14)
N_COLBLK = 14
PROJ_TN = 2 * GROUP_W
PLAIN_STEPS, ROPE64_STEPS = 3, 3


def _dot_nt(a, b):
    return lax.dot_general(a, b, (((1,), (1,)), ((), ())), preferred_element_type=F32)


def _layer_norm(y, g, b):
    mu = jnp.mean(y, axis=-1, keepdims=True)
    d = y - mu
    var = jnp.mean(d * d, axis=-1, keepdims=True)
    return d * lax.rsqrt(var + LN_EPS) * g + b


def _tree_sum(parts):
    while len(parts) > 1:
        parts = [parts[n] + parts[n + 1] for n in range(0, len(parts), 2)]
    return parts[0]


def _rope(acc, tab_ref, half):
    cos = tab_ref[:, 0:LANES]
    sin = tab_ref[:, LANES:2 * LANES]
    lane = lax.broadcasted_iota(jnp.int32, (1, LANES), 1)
    first = (lane & (2 * half - 1)) < half
    outs = []
    for c in range(acc.shape[1] // LANES):
        a = acc[:, c * LANES:(c + 1) * LANES]
        swapped = jnp.where(first, pltpu.roll(a, LANES - half, 1), pltpu.roll(a, half, 1))
        outs.append(a * cos + swapped * sin)
    return jnp.concatenate(outs, axis=1)


def _inproj_kernel(x_ref, w_ref, wm_ref, t64_ref, t32_ref, o_ref, misc_ref, xb_ref):
    j = pl.program_id(1)

    @pl.when(j == 0)
    def _():
        xb = x_ref[...].astype(BF16)
        xb_ref[...] = xb
        misc_ref[...] = jnp.dot(xb, wm_ref[...], preferred_element_type=F32)

    acc = jnp.dot(xb_ref[...], w_ref[...], preferred_element_type=F32)

    @pl.when(j < PLAIN_STEPS)
    def _():
        o_ref[...] = acc.astype(BF16)

    @pl.when(jnp.logical_and(j >= PLAIN_STEPS, j < PLAIN_STEPS + ROPE64_STEPS))
    def _():
        o_ref[...] = _rope(acc, t64_ref, HEAD_DIM // 2).astype(BF16)

    @pl.when(j == PLAIN_STEPS + ROPE64_STEPS)
    def _():
        o_ref[...] = _rope(acc, t32_ref, IDX_DIM // 2).astype(BF16)


def _inproj(x2d, w_main, w_misc, tab64, tab32, seq):
    m, d = x2d.shape
    tm = min(1024, seq)
    nt = seq // tm
    return pl.pallas_call(
        _inproj_kernel,
        grid=(m // tm, N_COLBLK * GROUP_W // PROJ_TN),
        in_specs=[
            pl.BlockSpec((tm, d), lambda i, j: (i, 0)),
            pl.BlockSpec((d, PROJ_TN), lambda i, j: (0, j)),
            pl.BlockSpec((d, LANES), lambda i, j: (0, 0)),
            pl.BlockSpec((tm, 2 * LANES), lambda i, j: (i % nt, 0)),
            pl.BlockSpec((tm, 2 * LANES), lambda i, j: (i % nt, 0)),
        ],
        out_specs=[
            pl.BlockSpec((tm, PROJ_TN), lambda i, j: (i, j)),
            pl.BlockSpec((tm, LANES), lambda i, j: (i, 0)),
        ],
        out_shape=[
            jax.ShapeDtypeStruct((m, N_COLBLK * GROUP_W), BF16),
            jax.ShapeDtypeStruct((m, LANES), F32),
        ],
        scratch_shapes=[pltpu.VMEM((tm, d), BF16)],
        compiler_params=pltpu.CompilerParams(
            dimension_semantics=("parallel", "arbitrary"), vmem_limit_bytes=VMEM_LIMIT),
    )(x2d, w_main, w_misc, tab64, tab32)


def _gate_kernel(z_ref, b_ref, o_ref, *, chunks):
    z = z_ref[...] + b_ref[...]
    w = jnp.minimum(z, 0.0) - jnp.log1p(jnp.exp(-jnp.abs(z)))
    rows = w.shape[0]
    lane = lax.broadcasted_iota(jnp.int32, (rows, LANES), 1)
    s = 1
    while s < LANES:
        w = w + jnp.where(lane >= s, pltpu.roll(w, s, 1), 0.0)
        s *= 2
    tot = jnp.broadcast_to(w[:, LANES - 1:LANES], (rows, LANES))
    pos = lax.broadcasted_iota(jnp.int32, (rows, LANES), 0) & (chunks - 1)
    run = tot
    s = 1
    while s < chunks:
        run = run + jnp.where(pos >= s, pltpu.roll(run, s, 0), 0.0)
        s *= 2
    bias = -LOG2E * (w + (run - tot))
    hi = bias.astype(BF16).astype(F32)
    mid = (bias - hi).astype(BF16).astype(F32)
    lo = (bias - hi - mid).astype(BF16).astype(F32)
    o_ref[0] = hi
    o_ref[1] = mid
    o_ref[2] = lo


def _gate_bias(z, bias_rows, chunks):
    b, rows, _ = z.shape
    return pl.pallas_call(
        functools.partial(_gate_kernel, chunks=chunks),
        grid=(b,),
        in_specs=[pl.BlockSpec((None, rows, LANES), lambda i: (i, 0, 0)),
                  pl.BlockSpec((rows, 1), lambda i: (0, 0))],
        out_specs=pl.BlockSpec((None, 3, rows, LANES), lambda i: (i, 0, 0, 0)),
        out_shape=jax.ShapeDtypeStruct((b, 3, rows, LANES), F32),
        compiler_params=pltpu.CompilerParams(dimension_semantics=("parallel",)),
    )(z, bias_rows)


HEADS_W = GROUP_HEADS * TB


def _stack_heads(q_ref, qs_ref):
    q = q_ref[...].astype(F32)
    head_of_lane = lax.broadcasted_iota(jnp.int32, (1, GROUP_W), 1) >> 6
    for h in range(GROUP_HEADS):
        qs_ref[h * TB:(h + 1) * TB, 0:GROUP_W] = jnp.where(head_of_lane == h, q, 0.0).astype(BF16)


def _head(x, h):
    return x[:, h * TB:(h + 1) * TB]


def _tile_heads(x):
    return jnp.concatenate([x] * GROUP_HEADS, axis=1)


def _kblock(k_ref, j):
    return k_ref[pl.ds(pl.multiple_of(j * TB, TB), TB), :]


def _pair_scores(k_ref, qs_ref, j, extra_ref=None):
    kblk = _kblock(k_ref, j)
    outs = []
    for pr in range(GROUP_W // LANES):
        rows = slice(2 * pr * TB, 2 * (pr + 1) * TB)
        kp = kblk[:, pr * LANES:(pr + 1) * LANES]
        qp = qs_ref[rows, pr * LANES:(pr + 1) * LANES]
        if extra_ref is not None:
            kp = jnp.concatenate([kp, _kblock(extra_ref, j)], axis=1)
            qp = jnp.concatenate([qp, qs_ref[rows, GROUP_W:GROUP_W + LANES]], axis=1)
        outs.append(_dot_nt(kp, qp))
    return jnp.concatenate(outs, axis=1)


def _causal_mask(heads=GROUP_HEADS):
    krow = lax.broadcasted_iota(jnp.int32, (TB, heads * TB), 0)
    qcol = lax.broadcasted_iota(jnp.int32, (TB, heads * TB), 1) & (TB - 1)
    return krow <= qcol


def _consume(st, vt, m, accs, mask_fn=None, weight=None, keep=None):
    if mask_fn is not None:
        st = mask_fn(st)
    blk_max = jnp.max(st, axis=0, keepdims=True)
    m_new = jnp.maximum(m, blk_max)
    if keep is not None:
        m_new = jnp.where(keep > 0.5, m_new, m)
    m_safe = jnp.where(m_new == NEG_INF, 0.0, m_new)
    alpha = jnp.exp2(m - m_safe)
    p = jnp.exp2(st - m_safe)
    if weight is not None:
        p = p * weight
    pb = p.astype(BF16)
    new_accs = []
    for h in range(GROUP_HEADS):
        pv = jnp.dot(vt[h * VROWS:(h + 1) * VROWS, :], _head(pb, h), preferred_element_type=F32)
        upd = _head(alpha, h) * accs[h] + pv
        new_accs.append(upd if keep is None else jnp.where(_head(keep, h) > 0.5, upd, accs[h]))
    return m_new, new_accs


TRIP_SHIFT = 2
STEPS_PER_TRIP = 1 << TRIP_SHIFT


def _attn_scratch(qs_width=GROUP_W):
    return [
        pltpu.VMEM((HEADS_W, qs_width), BF16),
        pltpu.VMEM((2, TB, HEADS_W), F32),
        pltpu.VMEM((1, HEADS_W), F32),
        pltpu.VMEM((GROUP_HEADS * VROWS, TB), F32),
    ]


def _attend_sequence(scores_of, vt_ref, o_ref, scratch, n_steps, block_of, consume_args):
    _, st_ref, m_ref, acc_ref = scratch

    def issue(s, slot):
        st_ref[slot] = scores_of(block_of(s))

    def consume(s, slot, last):
        accs = [acc_ref[h * VROWS:(h + 1) * VROWS, :] for h in range(GROUP_HEADS)]
        m, accs = _consume(st_ref[slot], vt_ref[block_of(s)], m_ref[...], accs,
                           **consume_args(s, last))
        m_ref[...] = m
        for h in range(GROUP_HEADS):
            acc_ref[h * VROWS:(h + 1) * VROWS, :] = accs[h]

    m_ref[...] = jnp.full((1, HEADS_W), NEG_INF, F32)
    acc_ref[...] = jnp.zeros_like(acc_ref)
    issue(0, 0)

    def quad(qq, carry):
        s = STEPS_PER_TRIP * qq
        for u in range(STEPS_PER_TRIP):
            issue(s + u + 1, (u + 1) & 1)
            consume(s + u, u & 1, False)
        return carry

    n_quads = n_steps >> TRIP_SHIFT
    lax.fori_loop(0, n_quads, quad, 0)

    @pl.when((n_steps & 2) != 0)
    def _():
        s = n_quads * STEPS_PER_TRIP
        issue(s + 1, 1)
        consume(s, 0, False)
        issue(s + 2, 0)
        consume(s + 1, 1, False)

    @pl.when((n_steps & 1) == 1)
    def _():
        issue(n_steps, 1)
        consume(n_steps - 1, 0, False)
        consume(n_steps, 1, True)

    @pl.when((n_steps & 1) == 0)
    def _():
        consume(n_steps, 0, True)

    ot = jnp.concatenate(
        [acc_ref[h * VROWS:h * VROWS + HEAD_DIM, :] / acc_ref[h * VROWS + HEAD_DIM:h * VROWS + HEAD_DIM + 1, :]
         for h in range(GROUP_HEADS)], axis=0)
    o_ref[...] = ot.T.astype(o_ref.dtype)


def _attn_specs(seq, colblks):
    nq = seq // TB
    specs = []
    for kind, cb in colblks:
        if kind == "q":
            specs.append(pl.BlockSpec((TB, GROUP_W), lambda b, i, cb=cb: (b * nq + i, cb)))
        else:
            specs.append(pl.BlockSpec((seq, GROUP_W), lambda b, i, cb=cb: (b, cb)))
    return specs


def _vt_spec(nkb):
    return pl.BlockSpec((None, nkb, GROUP_HEADS * VROWS, TB), lambda b, i: (b, 0, 0, 0))


def _o_spec(seq):
    nq = seq // TB
    return pl.BlockSpec((TB, GROUP_W), lambda b, i: (b * nq + i, 0))


_ATTN_PARAMS = dict(
    compiler_params=pltpu.CompilerParams(
        dimension_semantics=("parallel", "arbitrary"), vmem_limit_bytes=VMEM_LIMIT))


BIAS_TERMS = 3


def _fox_kernel(q_ref, k_ref, cb_ref, vt_ref, o_ref, *scratch):
    i = pl.program_id(1)
    qs_ref = scratch[0]
    _stack_heads(q_ref, qs_ref)
    lane = lax.broadcasted_iota(jnp.int32, (TB, LANES), 1)
    for h in range(GROUP_HEADS):
        pick = jnp.logical_and(lane >= BIAS_TERMS * h, lane < BIAS_TERMS * (h + 1))
        qs_ref[h * TB:(h + 1) * TB, GROUP_W:GROUP_W + LANES] = jnp.where(pick, 1.0, 0.0).astype(BF16)

    def scores_of(j):
        return _pair_scores(k_ref, qs_ref, j, cb_ref)

    def consume_args(j, diag):
        if diag:
            return dict(mask_fn=lambda st: jnp.where(_causal_mask(), st, NEG_INF))
        return {}

    _attend_sequence(scores_of, vt_ref, o_ref, scratch, i, lambda s: s, consume_args)


def _fox(qkv, cbias, vt, batch, seq):
    nkb = seq // TB
    return pl.pallas_call(
        _fox_kernel,
        grid=(batch, seq // TB),
        in_specs=_attn_specs(seq, (("q", CB_QA), ("k", CB_KA))) + [
            pl.BlockSpec((seq, LANES), lambda b, i: (b, 0)),
            _vt_spec(nkb),
        ],
        out_specs=_o_spec(seq),
        out_shape=jax.ShapeDtypeStruct((batch * seq, GROUP_W), BF16),
        scratch_shapes=_attn_scratch(GROUP_W + LANES),
        **_ATTN_PARAMS,
    )(qkv, qkv, cbias, vt)


HIGH_HALF = -(1 << 16)


def _key_to_float(key_u):
    key = key_u ^ jnp.int32(-2 ** 31)
    bits = jnp.where(key < 0, key ^ jnp.int32(0x7FFFFFFF), key)
    below_neg_inf = jnp.logical_and(key_u >= 0, key_u < jnp.int32(0x007FFFFF))
    return jnp.where(below_neg_inf, NEG_INF, lax.bitcast_convert_type(bits, F32))


def _dsa_kernel(q_ref, k_ref, vt_ref, qi_ref, k8_ref, w_ref, o_ref, strip_ref, strip_hi_ref, qm_ref, *scratch,
                topk, idx_bits):
    i = pl.program_id(1)
    nblk = i + 1

    qi = qi_ref[...].astype(F32)
    low_lanes = lax.broadcasted_iota(jnp.int32, (1, LANES), 1) < IDX_DIM
    per_half = LANES // IDX_DIM
    for h in range(IDX_HEADS):
        half = qi[:, (h // per_half) * LANES:(h // per_half + 1) * LANES]
        if h % per_half:
            half = pltpu.roll(half, LANES - IDX_DIM * (h % per_half), 1)
        qm_ref[h] = jnp.where(low_lanes, half, 0.0).astype(BF16)

    wrow = w_ref[...]

    def score_block(j, diag):
        kblk = _kblock(k8_ref, j)[:, 0:LANES]
        sc = _tree_sum([wrow[h:h + 1, :] * jnp.maximum(_dot_nt(kblk, qm_ref[h]), 0.0)
                        for h in range(IDX_HEADS)])
        if diag:
            sc = jnp.where(_causal_mask(1), sc, NEG_INF)
        strip_ref[j] = sc
        strip_hi_ref[j] = lax.bitcast_convert_type(
            lax.bitcast_convert_type(sc, jnp.int32) & HIGH_HALF, F32).astype(BF16)

    def score_pair(p, carry):
        score_block(2 * p, False)
        score_block(2 * p + 1, False)
        return carry

    lax.fori_loop(0, i >> 1, score_pair, 0)

    @pl.when((i & 1) == 1)
    def _():
        score_block(i - 1, False)

    score_block(i, True)

    def count(pred):
        def body(j, acc):
            hit = jnp.where(pred(j, strip_ref[j]), 1.0, 0.0)
            return acc + _tree_sum([hit[r:r + 8, :] for r in range(0, TB, 8)])
        acc = lax.fori_loop(0, nblk, body, jnp.zeros((8, TB), F32))
        return jnp.sum(acc, axis=0, keepdims=True)

    def count_ge(ref, t, rows):
        one, zero = jnp.ones((), ref.dtype), jnp.zeros((), ref.dtype)

        def hits(j, tt):
            hit = jnp.where(ref[j] >= tt, one, zero)
            return _tree_sum([hit[r:r + rows, :] for r in range(0, TB, rows)]).astype(F32)

        def body(p, acc):
            j1 = 2 * p + 1
            t1 = jnp.where(j1 < nblk, t, jnp.full_like(t, jnp.inf))
            return acc + hits(2 * p, t) + hits(jnp.minimum(j1, nblk - 1), t1)

        acc = lax.fori_loop(0, (nblk + 1) >> 1, body, jnp.zeros((rows, TB), F32))
        return jnp.sum(acc, axis=0, keepdims=True)

    def radix_pass(it, state, high):
        prefix, cnt_at_prefix = state
        cand = prefix | jnp.left_shift(jnp.int32(1), 31 - it)
        t = _key_to_float(cand)
        if high:
            t = lax.bitcast_convert_type(lax.bitcast_convert_type(t, jnp.int32) & HIGH_HALF, F32)
            cnt = count_ge(strip_hi_ref, t.astype(BF16), BF16_ROWS)
        else:
            cnt = count_ge(strip_ref, t, 8)
        take = cnt >= topk
        return jnp.where(take, cand, prefix), jnp.where(take, cnt, cnt_at_prefix)

    n_all = jnp.full((1, TB), 1.0, F32) * (nblk * TB).astype(F32)
    state = (jnp.zeros((1, TB), jnp.int32), n_all)
    state = lax.fori_loop(0, 16, lambda it, s: radix_pass(it, s, True), state)
    prefix, cnt_ge = lax.fori_loop(16, 32, lambda it, s: radix_pass(it, s, False), state)
    thr = _key_to_float(prefix)

    @pl.when(jnp.max(cnt_ge) > topk)
    def _():
        need = topk - count(lambda j, blk: blk > thr)
        krow = lax.broadcasted_iota(jnp.int32, (TB, TB), 0)

        def idx_body(it, lim):
            cand = lim | jnp.left_shift(jnp.int32(1), idx_bits - 1 - it)
            cnt = count(lambda j, blk: jnp.logical_and(blk == thr, krow + j * TB < cand))
            return jnp.where(cnt <= need, cand, lim)

        lim = lax.fori_loop(0, idx_bits, idx_body, jnp.zeros((1, TB), jnp.int32))

        def drop_body(j, carry):
            blk = strip_ref[j]
            drop = jnp.logical_and(blk == thr, krow + j * TB >= lim)
            strip_ref[j] = jnp.where(drop, NEG_INF, blk)
            return carry

        lax.fori_loop(0, nblk, drop_body, 0)

    qs_ref = scratch[0]
    _stack_heads(q_ref, qs_ref)

    def consume_args(j, diag):
        def mask_fn(st):
            blk = strip_ref[j]
            sel = blk >= thr
            if diag:
                sel = jnp.logical_and(sel, blk > NEG_INF)
            return jnp.concatenate(
                [jnp.where(sel, _head(st, h), NEG_INF) for h in range(GROUP_HEADS)], axis=1)
        return dict(mask_fn=mask_fn)

    _attend_sequence(lambda j: _pair_scores(k_ref, qs_ref, j), vt_ref, o_ref, scratch,
                     i, lambda s: s, consume_args)


def _dsa(qkv, vt, wrow, batch, seq):
    nkb = seq // TB
    topk = min(DSA_TOPK, seq // 4)
    kern = functools.partial(_dsa_kernel, topk=float(topk), idx_bits=int(seq).bit_length())
    return pl.pallas_call(
        kern,
        grid=(batch, seq // TB),
        in_specs=_attn_specs(seq, (("q", CB_QB), ("k", CB_KB))) + [_vt_spec(nkb)]
        + _attn_specs(seq, (("q", CB_QI), ("k", CB_KI8)))
        + [pl.BlockSpec((None, IDX_HEADS, TB), lambda b, i: (b, 0, i))],
        out_specs=_o_spec(seq),
        out_shape=jax.ShapeDtypeStruct((batch * seq, GROUP_W), BF16),
        scratch_shapes=[pltpu.VMEM((nkb, TB, TB), F32), pltpu.VMEM((nkb, TB, TB), BF16),
                        pltpu.VMEM((IDX_HEADS, TB, LANES), BF16)] + _attn_scratch(),
        **_ATTN_PARAMS,
    )(qkv, qkv, vt, qkv, qkv, wrow)


DIL_SPAN = max(w for w, _ in DIL_PATTERNS) // TB + 1


def _dilated_multiplicity():
    k = np.arange(TB)[:, None]
    q = np.arange(TB)[None, :]
    out = np.zeros((DIL_SPAN, TB, TB), np.float32)
    for delta in range(DIL_SPAN):
        d = q - k + TB * delta
        for w, r in DIL_PATTERNS:
            out[delta] += ((d >= 0) & (d % r == 0) & (d <= w)).astype(np.float32)
    return out


def _dil_kernel(q_ref, k_ref, vt_ref, mult_ref, o_ref, *scratch):
    i = pl.program_id(1)
    qs_ref = scratch[0]
    _stack_heads(q_ref, qs_ref)
    n = jnp.minimum(i, DIL_SPAN - 1)

    def consume_args(s, last):
        mult = _tile_heads(mult_ref[n - s])
        return dict(mask_fn=lambda st: jnp.where(mult > 0.0, st, NEG_INF), weight=mult)

    _attend_sequence(lambda j: _pair_scores(k_ref, qs_ref, j), vt_ref, o_ref, scratch,
                     n, lambda s: i - n + s, consume_args)


def _dilated(qkv, vt, mult, batch, seq):
    nkb = seq // TB
    return pl.pallas_call(
        _dil_kernel,
        grid=(batch, seq // TB),
        in_specs=_attn_specs(seq, (("q", CB_QC), ("k", CB_KC))) + [
            _vt_spec(nkb),
            pl.BlockSpec((DIL_SPAN, TB, TB), lambda b, i: (0, 0, 0)),
        ],
        out_specs=_o_spec(seq),
        out_shape=jax.ShapeDtypeStruct((batch * seq, GROUP_W), BF16),
        scratch_shapes=_attn_scratch(),
        **_ATTN_PARAMS,
    )(qkv, qkv, vt, mult)


def _moba_kernel(q_ref, k_ref, vt_ref, o_ref, kmean_ref, sel_ref, *scratch, nkb, nkb_pad, topn):
    i = pl.program_id(1)

    @pl.when(i == 0)
    def _():
        kmean_ref[...] = jnp.zeros_like(kmean_ref)
        for n in range(nkb):
            kb = k_ref[n * TB:(n + 1) * TB, :].astype(F32)
            kmean_ref[n:n + 1, :] = jnp.mean(kb, axis=0, keepdims=True)

    qs_ref = scratch[0]
    _stack_heads(q_ref, qs_ref)
    blk_id = lax.broadcasted_iota(jnp.int32, (nkb_pad, HEADS_W), 0)
    blk_f = blk_id.astype(F32)
    g = _dot_nt(kmean_ref[...].astype(BF16), qs_ref[...])
    g = jnp.where(blk_id < i, g, NEG_INF)
    sel = jnp.zeros_like(g)
    for _ in range(topn):
        mx = jnp.max(g, axis=0, keepdims=True)
        is_max = jnp.logical_and(g == mx, mx > NEG_INF)
        first = jnp.min(jnp.where(is_max, blk_f, float(nkb_pad)), axis=0, keepdims=True)
        pick = blk_f == first
        sel = jnp.where(pick, 1.0, sel)
        g = jnp.where(pick, NEG_INF, g)
    sel_ref[...] = sel

    def consume_args(j, diag):
        if diag:
            return dict(mask_fn=lambda st: jnp.where(_causal_mask(), st, NEG_INF))
        return dict(keep=sel_ref[pl.ds(j, 1), :])

    _attend_sequence(lambda j: _pair_scores(k_ref, qs_ref, j), vt_ref, o_ref, scratch,
                     i, lambda s: s, consume_args)


def _moba(qkv, vt, batch, seq):
    nkb = seq // TB
    nkb_pad = -(-nkb // 8) * 8
    kern = functools.partial(_moba_kernel, nkb=nkb, nkb_pad=nkb_pad, topn=min(MOBA_TOPK, nkb))
    return pl.pallas_call(
        kern,
        grid=(batch, seq // TB),
        in_specs=_attn_specs(seq, (("q", CB_QD), ("k", CB_KD))) + [_vt_spec(nkb)],
        out_specs=_o_spec(seq),
        out_shape=jax.ShapeDtypeStruct((batch * seq, GROUP_W), BF16),
        scratch_shapes=[pltpu.VMEM((nkb_pad, GROUP_W), F32),
                        pltpu.VMEM((nkb_pad, HEADS_W), F32)] + _attn_scratch(),
        compiler_params=pltpu.CompilerParams(
            dimension_semantics=("arbitrary", "arbitrary"), vmem_limit_bytes=VMEM_LIMIT),
    )(qkv, qkv, vt)


def _outproj_kernel(oa_ref, ob_ref, oc_ref, od_ref, w_ref, x_ref, g_ref, b_ref, y_ref, *, alpha):
    acc = alpha * x_ref[...]
    for gi, o_ref in enumerate((oa_ref, ob_ref, oc_ref, od_ref)):
        acc = acc + jnp.dot(o_ref[...], w_ref[gi * GROUP_W:(gi + 1) * GROUP_W, :],
                            preferred_element_type=F32)
    y_ref[...] = _layer_norm(acc, g_ref[...], b_ref[...])


def _outproj(os4, w_o, x2d, g, b, alpha):
    m, d = x2d.shape
    tm = min(512, m)
    row = lambda i: (i, 0)
    fixed = lambda i: (0, 0)
    return pl.pallas_call(
        functools.partial(_outproj_kernel, alpha=alpha),
        grid=(m // tm,),
        in_specs=[pl.BlockSpec((tm, GROUP_W), row)] * 4 + [
            pl.BlockSpec((d, d), fixed), pl.BlockSpec((tm, d), row),
            pl.BlockSpec((1, d), fixed), pl.BlockSpec((1, d), fixed)],
        out_specs=pl.BlockSpec((tm, d), row),
        out_shape=jax.ShapeDtypeStruct((m, d), F32),
        compiler_params=pltpu.CompilerParams(
            dimension_semantics=("parallel",), vmem_limit_bytes=VMEM_LIMIT),
    )(*os4, w_o, x2d, g, b)


def _mlp_kernel(x_ref, wu_ref, wd_ref, g_ref, b_ref, y_ref, xb_ref, acc_ref, *, alpha):
    f = pl.program_id(1)

    @pl.when(f == 0)
    def _():
        x = x_ref[...]
        xb_ref[...] = x.astype(BF16)
        acc_ref[...] = alpha * x

    h = jnp.dot(xb_ref[...], wu_ref[...], preferred_element_type=F32)
    h = jnp.square(jnp.maximum(h, 0.0))
    acc_ref[...] += jnp.dot(h.astype(BF16), wd_ref[...], preferred_element_type=F32)

    @pl.when(f == pl.num_programs(1) - 1)
    def _():
        y_ref[...] = _layer_norm(acc_ref[...], g_ref[...], b_ref[...])


def _mlp(x2d, w_up, w_down, g, b, alpha):
    m, d = x2d.shape
    dff = w_up.shape[1]
    tm = min(1024, m)
    tf = min(512, dff)
    return pl.pallas_call(
        functools.partial(_mlp_kernel, alpha=alpha),
        grid=(m // tm, dff // tf),
        in_specs=[
            pl.BlockSpec((tm, d), lambda i, f: (i, 0)),
            pl.BlockSpec((d, tf), lambda i, f: (0, f)),
            pl.BlockSpec((tf, d), lambda i, f: (f, 0)),
            pl.BlockSpec((1, d), lambda i, f: (0, 0)),
            pl.BlockSpec((1, d), lambda i, f: (0, 0)),
        ],
        out_specs=pl.BlockSpec((tm, d), lambda i, f: (i, 0)),
        out_shape=jax.ShapeDtypeStruct((m, d), F32),
        scratch_shapes=[pltpu.VMEM((tm, d), BF16), pltpu.VMEM((tm, d), F32)],
        compiler_params=pltpu.CompilerParams(
            dimension_semantics=("parallel", "arbitrary"), vmem_limit_bytes=VMEM_LIMIT),
    )(x2d, w_up, w_down, g, b)


def _column_plan():
    hq = GROUP_W
    sizes = (hq, hq, hq, GROUP_HEADS,
             hq, hq, hq, IDX_HEADS * IDX_DIM, IDX_DIM, IDX_HEADS,
             hq, hq, hq, hq, hq, hq)
    offs = np.concatenate([[0], np.cumsum(sizes)])
    (qa, ka, va, fa, qb, kb, vb, qi, ki, wi, qc, kc, vc, qd, kd, vd) = [
        np.arange(offs[n], offs[n + 1]) for n in range(len(sizes))]
    blocks = {CB_QA: qa, CB_KA: ka, CB_VA: va, CB_VB: vb, CB_VC: vc, CB_VD: vd,
              CB_QB: qb, CB_KB: kb, CB_QC: qc, CB_KC: kc, CB_QD: qd, CB_KD: kd,
              CB_QI: qi, CB_KI8: np.tile(ki, IDX_HEADS)}
    scale = np.ones(N_COLBLK * GROUP_W, np.float32)
    for cb in (CB_QA, CB_QB, CB_QC, CB_QD):
        scale[cb * GROUP_W:(cb + 1) * GROUP_W] = LOG2E * HEAD_DIM ** -0.5
    return np.concatenate([blocks[cb] for cb in range(N_COLBLK)]), scale, np.concatenate([fa, wi])


def _rope_table(seq, dim):
    half = dim // 2
    lane = np.arange(LANES)
    inv = ROPE_THETA ** (-(jnp.arange(half, dtype=F32)) / half)
    ang = jnp.arange(seq, dtype=F32)[:, None] * inv[None, :]
    idx = lane % half
    sign = np.where((lane % dim) < half, -1.0, 1.0).astype(np.float32)
    return jnp.concatenate([jnp.cos(ang)[:, idx], jnp.sin(ang)[:, idx] * sign[None, :]], axis=1)


def _vt_blocks(qkv, cb, batch, seq):
    nkb = seq // TB
    v = qkv[:, cb * GROUP_W:(cb + 1) * GROUP_W].reshape(batch, nkb, TB, GROUP_HEADS, HEAD_DIM)
    pad = jnp.zeros((batch, nkb, TB, GROUP_HEADS, VROWS - HEAD_DIM), v.dtype).at[..., 0].set(1)
    v = jnp.concatenate([v, pad], axis=-1).reshape(batch, nkb, TB, GROUP_HEADS * VROWS)
    return v.transpose(0, 1, 3, 2)


def kernel(x, w_in, b_f, w_o, ln1_g, ln1_b, w_up, w_down, ln2_g, ln2_b):
    batch, seq, d = x.shape
    depth = w_in.shape[0]
    assert seq % TB == 0 and d == 4 * GROUP_W
    chunks = seq // LANES
    assert chunks & (chunks - 1) == 0
    alpha = (2.0 * depth) ** 0.25

    main_cols, main_scale, misc_cols = _column_plan()
    w_main = (w_in[:, :, main_cols] * main_scale).astype(BF16)
    w_misc = jnp.pad(w_in[:, :, misc_cols], ((0, 0), (0, 0), (0, LANES - len(misc_cols)))).astype(BF16)
    w_o16, w_up16, w_down16 = w_o.astype(BF16), w_up.astype(BF16), w_down.astype(BF16)
    tab64 = _rope_table(seq, HEAD_DIM)
    tab32 = _rope_table(seq, IDX_DIM)
    mult = jnp.asarray(_dilated_multiplicity())

    x2d = x.reshape(batch * seq, d)
    for l in range(depth):
        qkv, misc = _inproj(x2d, w_main[l], w_misc[l], tab64, tab32, seq)
        misc3 = misc.reshape(batch, seq, LANES)
        z = misc3[:, :, 0:GROUP_HEADS].transpose(0, 2, 1).reshape(batch, GROUP_HEADS * chunks, LANES)
        bias_rows = jnp.repeat(b_f[l].astype(F32), chunks)[:, None]
        gb = _gate_bias(z, bias_rows, chunks)
        gb = gb.reshape(batch, BIAS_TERMS, GROUP_HEADS, seq).transpose(0, 3, 2, 1)
        cbias = jnp.pad(gb.reshape(batch * seq, GROUP_HEADS * BIAS_TERMS),
                        ((0, 0), (0, LANES - GROUP_HEADS * BIAS_TERMS))).astype(BF16)
        wrow = misc3[:, :, GROUP_HEADS:GROUP_HEADS + IDX_HEADS].transpose(0, 2, 1)

        oa = _fox(qkv, cbias, _vt_blocks(qkv, CB_VA, batch, seq), batch, seq)
        ob = _dsa(qkv, _vt_blocks(qkv, CB_VB, batch, seq), wrow, batch, seq)
        oc = _dilated(qkv, _vt_blocks(qkv, CB_VC, batch, seq), mult, batch, seq)
        od = _moba(qkv, _vt_blocks(qkv, CB_VD, batch, seq), batch, seq)

        x2d = _outproj((oa, ob, oc, od), w_o16[l], x2d, ln1_g[l][None, :], ln1_b[l][None, :], alpha)
        x2d = _mlp(x2d, w_up16[l], w_down16[l], ln2_g[l][None, :], ln2_b[l][None, :], alpha)
    return x2d.reshape(batch, seq, d)
```

```python
import functools
import math

import numpy as np
import jax
import jax.numpy as jnp
from jax import lax
from jax.experimental import pallas as pl
from jax.experimental.pallas import tpu as pltpu

F32 = jnp.float32
BF16 = jnp.bfloat16

HEAD_DIM = 64
GROUP_HEADS = 4
GROUP_W = GROUP_HEADS * HEAD_DIM
IDX_HEADS = 8
IDX_DIM = 32
DSA_TOPK = 256
DIL_PATTERNS = ((128, 1), (512, 4), (2048, 16))
MOBA_BLOCK = 256
MOBA_TOPK = 3
ROPE_THETA = 10000.0
LN_EPS = 1e-5
LOG2E = math.log2(math.e)
LANES = 128
BF16_ROWS = 16
TB = 256
VROWS = HEAD_DIM + BF16_ROWS
NEG_INF = float("-inf")
VMEM_LIMIT = 56 * 1024 * 1024

(CB_QA, CB_KA, CB_VA, CB_VB, CB_VC, CB_VD,
 CB_QB, CB_KB, CB_QC, CB_KC, CB_QD, CB_KD,
 CB_QI, CB_KI8) = range(14)
N_COLBLK = 14
PROJ_TN = 2 * GROUP_W
PLAIN_STEPS, ROPE64_STEPS = 3, 3


def _dot_nt(a, b):
    return lax.dot_general(a, b, (((1,), (1,)), ((), ())), preferred_element_type=F32)


def _layer_norm(y, g, b):
    mu = jnp.mean(y, axis=-1, keepdims=True)
    d = y - mu
    var = jnp.mean(d * d, axis=-1, keepdims=True)
    return d * lax.rsqrt(var + LN_EPS) * g + b


def _tree_sum(parts):
    while len(parts) > 1:
        parts = [parts[n] + parts[n + 1] for n in range(0, len(parts), 2)]
    return parts[0]


def _rope(acc, tab_ref, half):
    cos = tab_ref[:, 0:LANES]
    sin = tab_ref[:, LANES:2 * LANES]
    lane = lax.broadcasted_iota(jnp.int32, (1, LANES), 1)
    first = (lane & (2 * half - 1)) < half
    outs = []
    for c in range(acc.shape[1] // LANES):
        a = acc[:, c * LANES:(c + 1) * LANES]
        swapped = jnp.where(first, pltpu.roll(a, LANES - half, 1), pltpu.roll(a, half, 1))
        outs.append(a * cos + swapped * sin)
    return jnp.concatenate(outs, axis=1)


def _inproj_kernel(x_ref, w_ref, wm_ref, t64_ref, t32_ref, o_ref, misc_ref, xb_ref):
    j = pl.program_id(1)

    @pl.when(j == 0)
    def _():
        xb = x_ref[...].astype(BF16)
        xb_ref[...] = xb
        misc_ref[...] = jnp.dot(xb, wm_ref[...], preferred_element_type=F32)

    acc = jnp.dot(xb_ref[...], w_ref[...], preferred_element_type=F32)

    @pl.when(j < PLAIN_STEPS)
    def _():
        o_ref[...] = acc.astype(BF16)

    @pl.when(jnp.logical_and(j >= PLAIN_STEPS, j < PLAIN_STEPS + ROPE64_STEPS))
    def _():
        o_ref[...] = _rope(acc, t64_ref, HEAD_DIM // 2).astype(BF16)

    @pl.when(j == PLAIN_STEPS + ROPE64_STEPS)
    def _():
        o_ref[...] = _rope(acc, t32_ref, IDX_DIM // 2).astype(BF16)


def _inproj(x2d, w_main, w_misc, tab64, tab32, seq):
    m, d = x2d.shape
    tm = min(1024, seq)
    nt = seq // tm
    return pl.pallas_call(
        _inproj_kernel,
        grid=(m // tm, N_COLBLK * GROUP_W // PROJ_TN),
        in_specs=[
            pl.BlockSpec((tm, d), lambda i, j: (i, 0)),
            pl.BlockSpec((d, PROJ_TN), lambda i, j: (0, j)),
            pl.BlockSpec((d, LANES), lambda i, j: (0, 0)),
            pl.BlockSpec((tm, 2 * LANES), lambda i, j: (i % nt, 0)),
            pl.BlockSpec((tm, 2 * LANES), lambda i, j: (i % nt, 0)),
        ],
        out_specs=[
            pl.BlockSpec((tm, PROJ_TN), lambda i, j: (i, j)),
            pl.BlockSpec((tm, LANES), lambda i, j: (i, 0)),
        ],
        out_shape=[
            jax.ShapeDtypeStruct((m, N_COLBLK * GROUP_W), BF16),
            jax.ShapeDtypeStruct((m, LANES), F32),
        ],
        scratch_shapes=[pltpu.VMEM((tm, d), BF16)],
        compiler_params=pltpu.CompilerParams(
            dimension_semantics=("parallel", "arbitrary"), vmem_limit_bytes=VMEM_LIMIT),
    )(x2d, w_main, w_misc, tab64, tab32)


def _gate_kernel(z_ref, b_ref, o_ref, *, chunks):
    z = z_ref[...] + b_ref[...]
    w = jnp.minimum(z, 0.0) - jnp.log1p(jnp.exp(-jnp.abs(z)))
    rows = w.shape[0]
    lane = lax.broadcasted_iota(jnp.int32, (rows, LANES), 1)
    s = 1
    while s < LANES:
        w = w + jnp.where(lane >= s, pltpu.roll(w, s, 1), 0.0)
        s *= 2
    tot = jnp.broadcast_to(w[:, LANES - 1:LANES], (rows, LANES))
    pos = lax.broadcasted_iota(jnp.int32, (rows, LANES), 0) & (chunks - 1)
    run = tot
    s = 1
    while s < chunks:
        run = run + jnp.where(pos >= s, pltpu.roll(run, s, 0), 0.0)
        s *= 2
    bias = -LOG2E * (w + (run - tot))
    hi = bias.astype(BF16).astype(F32)
    mid = (bias - hi).astype(BF16).astype(F32)
    lo = (bias - hi - mid).astype(BF16).astype(F32)
    o_ref[0] = hi
    o_ref[1] = mid
    o_ref[2] = lo


def _gate_bias(z, bias_rows, chunks):
    b, rows, _ = z.shape
    return pl.pallas_call(
        functools.partial(_gate_kernel, chunks=chunks),
        grid=(b,),
        in_specs=[pl.BlockSpec((None, rows, LANES), lambda i: (i, 0, 0)),
                  pl.BlockSpec((rows, 1), lambda i: (0, 0))],
        out_specs=pl.BlockSpec((None, 3, rows, LANES), lambda i: (i, 0, 0, 0)),
        out_shape=jax.ShapeDtypeStruct((b, 3, rows, LANES), F32),
        compiler_params=pltpu.CompilerParams(dimension_semantics=("parallel",)),
    )(z, bias_rows)


HEADS_W = GROUP_HEADS * TB


def _stack_heads(q_ref, qs_ref):
    q = q_ref[...].astype(F32)
    head_of_lane = lax.broadcasted_iota(jnp.int32, (1, GROUP_W), 1) >> 6
    for h in range(GROUP_HEADS):
        qs_ref[h * TB:(h + 1) * TB, 0:GROUP_W] = jnp.where(head_of_lane == h, q, 0.0).astype(BF16)


def _head(x, h):
    return x[:, h * TB:(h + 1) * TB]


def _tile_heads(x):
    return jnp.concatenate([x] * GROUP_HEADS, axis=1)


def _kblock(k_ref, j):
    return k_ref[pl.ds(pl.multiple_of(j * TB, TB), TB), :]


def _pair_scores(k_ref, qs_ref, j, extra_ref=None):
    kblk = _kblock(k_ref, j)
    outs = []
    for pr in range(GROUP_W // LANES):
        rows = slice(2 * pr * TB, 2 * (pr + 1) * TB)
        kp = kblk[:, pr * LANES:(pr + 1) * LANES]
        qp = qs_ref[rows, pr * LANES:(pr + 1) * LANES]
        if extra_ref is not None:
            kp = jnp.concatenate([kp, _kblock(extra_ref, j)], axis=1)
            qp = jnp.concatenate([qp, qs_ref[rows, GROUP_W:GROUP_W + LANES]], axis=1)
        outs.append(_dot_nt(kp, qp))
    return jnp.concatenate(outs, axis=1)


def _causal_mask(heads=GROUP_HEADS):
    krow = lax.broadcasted_iota(jnp.int32, (TB, heads * TB), 0)
    qcol = lax.broadcasted_iota(jnp.int32, (TB, heads * TB), 1) & (TB - 1)
    return krow <= qcol


def _consume(st, blk_max, vt, m, accs, weight=None, keep=None):
    m_new = jnp.maximum(m, blk_max)
    if keep is not None:
        m_new = jnp.where(keep > 0.5, m_new, m)
    m_safe = jnp.where(m_new == NEG_INF, 0.0, m_new)
    alpha = jnp.exp2(m - m_safe)
    p = jnp.exp2(st - m_safe)
    if weight is not None:
        p = p * weight
    pb = p.astype(BF16)
    new_accs = []
    for h in range(GROUP_HEADS):
        pv = jnp.dot(vt[h * VROWS:(h + 1) * VROWS, :], _head(pb, h), preferred_element_type=F32)
        upd = _head(alpha, h) * accs[h] + pv
        new_accs.append(upd if keep is None else jnp.where(_head(keep, h) > 0.5, upd, accs[h]))
    return m_new, new_accs


TRIP_SHIFT = 2
STEPS_PER_TRIP = 1 << TRIP_SHIFT


def _attn_scratch(qs_width=GROUP_W):
    return [
        pltpu.VMEM((HEADS_W, qs_width), BF16),
        pltpu.VMEM((2, TB, HEADS_W), F32),
        pltpu.VMEM((2, 1, HEADS_W), F32),
        pltpu.VMEM((1, HEADS_W), F32),
        pltpu.VMEM((GROUP_HEADS * VROWS, TB), F32),
    ]


def _attend_sequence(scores_of, vt_ref, o_ref, scratch, n_steps, block_of, step_args):
    _, st_ref, bm_ref, m_ref, acc_ref = scratch

    def issue(s, slot, first=False):
        st = scores_of(block_of(s))
        mask_fn = step_args(s, first).get("mask_fn")
        if mask_fn is not None:
            st = mask_fn(st)
        st_ref[slot] = st
        bm_ref[slot] = jnp.max(st, axis=0, keepdims=True)

    def consume(s, slot, first=False):
        args = step_args(s, first)
        accs = [acc_ref[h * VROWS:(h + 1) * VROWS, :] for h in range(GROUP_HEADS)]
        m, accs = _consume(st_ref[slot], bm_ref[slot], vt_ref[block_of(s)], m_ref[...], accs,
                           weight=args.get("weight"), keep=args.get("keep"))
        m_ref[...] = m
        for h in range(GROUP_HEADS):
            acc_ref[h * VROWS:(h + 1) * VROWS, :] = accs[h]

    m_ref[...] = jnp.full((1, HEADS_W), NEG_INF, F32)
    acc_ref[...] = jnp.zeros_like(acc_ref)
    issue(0, 0, True)
    issue(1, 1)
    consume(0, 0, True)

    def quad(qq, carry):
        s = 1 + STEPS_PER_TRIP * qq
        for u in range(STEPS_PER_TRIP):
            issue(s + u + 1, u & 1)
            consume(s + u, (u + 1) & 1)
        return carry

    n_quads = n_steps >> TRIP_SHIFT
    lax.fori_loop(0, n_quads, quad, 0)
    rest = 1 + n_quads * STEPS_PER_TRIP

    @pl.when((n_steps & 2) != 0)
    def _():
        issue(rest + 1, 0)
        consume(rest, 1)
        issue(rest + 2, 1)
        consume(rest + 1, 0)

    @pl.when((n_steps & 1) == 1)
    def _():
        consume(n_steps, 1)

    ot = jnp.concatenate(
        [acc_ref[h * VROWS:h * VROWS + HEAD_DIM, :] / acc_ref[h * VROWS + HEAD_DIM:h * VROWS + HEAD_DIM + 1, :]
         for h in range(GROUP_HEADS)], axis=0)
    o_ref[...] = ot.T.astype(o_ref.dtype)


def _diagonal_first(i):
    return lambda s: jnp.where(s == 0, i, s - 1)


def _attn_specs(seq, colblks):
    nq = seq // TB
    specs = []
    for kind, cb in colblks:
        if kind == "q":
            specs.append(pl.BlockSpec((TB, GROUP_W), lambda b, i, cb=cb: (b * nq + i, cb)))
        else:
            specs.append(pl.BlockSpec((seq, GROUP_W), lambda b, i, cb=cb: (b, cb)))
    return specs


def _vt_spec(nkb):
    return pl.BlockSpec((None, nkb, GROUP_HEADS * VROWS, TB), lambda b, i: (b, 0, 0, 0))


def _o_spec(seq):
    nq = seq // TB
    return pl.BlockSpec((TB, GROUP_W), lambda b, i: (b * nq + i, 0))


_ATTN_PARAMS = dict(
    compiler_params=pltpu.CompilerParams(
        dimension_semantics=("parallel", "arbitrary"), vmem_limit_bytes=VMEM_LIMIT))


BIAS_TERMS = 3


def _fox_kernel(q_ref, k_ref, cb_ref, vt_ref, o_ref, *scratch):
    i = pl.program_id(1)
    qs_ref = scratch[0]
    _stack_heads(q_ref, qs_ref)
    lane = lax.broadcasted_iota(jnp.int32, (TB, LANES), 1)
    for h in range(GROUP_HEADS):
        pick = jnp.logical_and(lane >= BIAS_TERMS * h, lane < BIAS_TERMS * (h + 1))
        qs_ref[h * TB:(h + 1) * TB, GROUP_W:GROUP_W + LANES] = jnp.where(pick, 1.0, 0.0).astype(BF16)

    def scores_of(j):
        return _pair_scores(k_ref, qs_ref, j, cb_ref)

    def step_args(s, diag):
        if diag:
            return dict(mask_fn=lambda st: jnp.where(_causal_mask(), st, NEG_INF))
        return {}

    _attend_sequence(scores_of, vt_ref, o_ref, scratch, i, _diagonal_first(i), step_args)


def _fox(qkv, cbias, vt, batch, seq):
    nkb = seq // TB
    return pl.pallas_call(
        _fox_kernel,
        grid=(batch, seq // TB),
        in_specs=_attn_specs(seq, (("q", CB_QA), ("k", CB_KA))) + [
            pl.BlockSpec((seq, LANES), lambda b, i: (b, 0)),
            _vt_spec(nkb),
        ],
        out_specs=_o_spec(seq),
        out_shape=jax.ShapeDtypeStruct((batch * seq, GROUP_W), BF16),
        scratch_shapes=_attn_scratch(GROUP_W + LANES),
        **_ATTN_PARAMS,
    )(qkv, qkv, cbias, vt)


HIGH_HALF = -(1 << 16)


def _key_to_float(key_u):
    key = key_u ^ jnp.int32(-2 ** 31)
    bits = jnp.where(key < 0, key ^ jnp.int32(0x7FFFFFFF), key)
    below_neg_inf = jnp.logical_and(key_u >= 0, key_u < jnp.int32(0x007FFFFF))
    return jnp.where(below_neg_inf, NEG_INF, lax.bitcast_convert_type(bits, F32))


def _dsa_kernel(q_ref, k_ref, vt_ref, qi_ref, k8_ref, w_ref, o_ref, strip_ref, strip_hi_ref, qm_ref, *scratch,
                topk, idx_bits):
    i = pl.program_id(1)
    nblk = i + 1

    qi = qi_ref[...].astype(F32)
    low_lanes = lax.broadcasted_iota(jnp.int32, (1, LANES), 1) < IDX_DIM
    per_half = LANES // IDX_DIM
    for h in range(IDX_HEADS):
        half = qi[:, (h // per_half) * LANES:(h // per_half + 1) * LANES]
        if h % per_half:
            half = pltpu.roll(half, LANES - IDX_DIM * (h % per_half), 1)
        qm_ref[h] = jnp.where(low_lanes, half, 0.0).astype(BF16)

    wrow = w_ref[...]

    def score_block(j, diag):
        kblk = _kblock(k8_ref, j)[:, 0:LANES]
        sc = _tree_sum([wrow[h:h + 1, :] * jnp.maximum(_dot_nt(kblk, qm_ref[h]), 0.0)
                        for h in range(IDX_HEADS)])
        if diag:
            sc = jnp.where(_causal_mask(1), sc, NEG_INF)
        strip_ref[j] = sc
        strip_hi_ref[j] = lax.bitcast_convert_type(
            lax.bitcast_convert_type(sc, jnp.int32) & HIGH_HALF, F32).astype(BF16)

    def score_pair(p, carry):
        score_block(2 * p, False)
        score_block(2 * p + 1, False)
        return carry

    lax.fori_loop(0, i >> 1, score_pair, 0)

    @pl.when((i & 1) == 1)
    def _():
        score_block(i - 1, False)

    score_block(i, True)

    def count(pred):
        def body(j, acc):
            hit = jnp.where(pred(j, strip_ref[j]), 1.0, 0.0)
            return acc + _tree_sum([hit[r:r + 8, :] for r in range(0, TB, 8)])
        acc = lax.fori_loop(0, nblk, body, jnp.zeros((8, TB), F32))
        return jnp.sum(acc, axis=0, keepdims=True)

    def count_ge(ref, t, rows):
        one, zero = jnp.ones((), ref.dtype), jnp.zeros((), ref.dtype)

        def hits(j, tt):
            hit = jnp.where(ref[j] >= tt, one, zero)
            return _tree_sum([hit[r:r + rows, :] for r in range(0, TB, rows)]).astype(F32)

        def body(p, acc):
            j1 = 2 * p + 1
            t1 = jnp.where(j1 < nblk, t, jnp.full_like(t, jnp.inf))
            return acc + hits(2 * p, t) + hits(jnp.minimum(j1, nblk - 1), t1)

        acc = lax.fori_loop(0, (nblk + 1) >> 1, body, jnp.zeros((rows, TB), F32))
        return jnp.sum(acc, axis=0, keepdims=True)

    def radix_pass(it, state, high):
        prefix, cnt_at_prefix = state
        cand = prefix | jnp.left_shift(jnp.int32(1), 31 - it)
        t = _key_to_float(cand)
        if high:
            t = lax.bitcast_convert_type(lax.bitcast_convert_type(t, jnp.int32) & HIGH_HALF, F32)
            cnt = count_ge(strip_hi_ref, t.astype(BF16), BF16_ROWS)
        else:
            cnt = count_ge(strip_ref, t, 8)
        take = cnt >= topk
        return jnp.where(take, cand, prefix), jnp.where(take, cnt, cnt_at_prefix)

    n_all = jnp.full((1, TB), 1.0, F32) * (nblk * TB).astype(F32)
    state = (jnp.zeros((1, TB), jnp.int32), n_all)
    state = lax.fori_loop(0, 16, lambda it, s: radix_pass(it, s, True), state)
    prefix, cnt_ge = lax.fori_loop(16, 32, lambda it, s: radix_pass(it, s, False), state)
    thr = _key_to_float(prefix)

    @pl.when(jnp.max(cnt_ge) > topk)
    def _():
        need = topk - count(lambda j, blk: blk > thr)
        krow = lax.broadcasted_iota(jnp.int32, (TB, TB), 0)

        def idx_body(it, lim):
            cand = lim | jnp.left_shift(jnp.int32(1), idx_bits - 1 - it)
            cnt = count(lambda j, blk: jnp.logical_and(blk == thr, krow + j * TB < cand))
            return jnp.where(cnt <= need, cand, lim)

        lim = lax.fori_loop(0, idx_bits, idx_body, jnp.zeros((1, TB), jnp.int32))

        def drop_body(j, carry):
            blk = strip_ref[j]
            drop = jnp.logical_and(blk == thr, krow + j * TB >= lim)
            strip_ref[j] = jnp.where(drop, NEG_INF, blk)
            return carry

        lax.fori_loop(0, nblk, drop_body, 0)

    qs_ref = scratch[0]
    _stack_heads(q_ref, qs_ref)

    block_of = _diagonal_first(i)

    def step_args(s, diag):
        def mask_fn(st):
            blk = strip_ref[block_of(s)]
            sel = blk >= thr
            if diag:
                sel = jnp.logical_and(sel, blk > NEG_INF)
            return jnp.concatenate(
                [jnp.where(sel, _head(st, h), NEG_INF) for h in range(GROUP_HEADS)], axis=1)
        return dict(mask_fn=mask_fn)

    _attend_sequence(lambda j: _pair_scores(k_ref, qs_ref, j), vt_ref, o_ref, scratch,
                     i, block_of, step_args)


def _dsa(qkv, vt, wrow, batch, seq):
    nkb = seq // TB
    topk = min(DSA_TOPK, seq // 4)
    kern = functools.partial(_dsa_kernel, topk=float(topk), idx_bits=int(seq).bit_length())
    return pl.pallas_call(
        kern,
        grid=(batch, seq // TB),
        in_specs=_attn_specs(seq, (("q", CB_QB), ("k", CB_KB))) + [_vt_spec(nkb)]
        + _attn_specs(seq, (("q", CB_QI), ("k", CB_KI8)))
        + [pl.BlockSpec((None, IDX_HEADS, TB), lambda b, i: (b, 0, i))],
        out_specs=_o_spec(seq),
        out_shape=jax.ShapeDtypeStruct((batch * seq, GROUP_W), BF16),
        scratch_shapes=[pltpu.VMEM((nkb, TB, TB), F32), pltpu.VMEM((nkb, TB, TB), BF16),
                        pltpu.VMEM((IDX_HEADS, TB, LANES), BF16)] + _attn_scratch(),
        **_ATTN_PARAMS,
    )(qkv, qkv, vt, qkv, qkv, wrow)


DIL_SPAN = max(w for w, _ in DIL_PATTERNS) // TB + 1


def _dilated_multiplicity():
    k = np.arange(TB)[:, None]
    q = np.arange(TB)[None, :]
    out = np.zeros((DIL_SPAN, TB, TB), np.float32)
    for delta in range(DIL_SPAN):
        d = q - k + TB * delta
        for w, r in DIL_PATTERNS:
            out[delta] += ((d >= 0) & (d % r == 0) & (d <= w)).astype(np.float32)
    return out


def _dil_kernel(q_ref, k_ref, vt_ref, mult_ref, o_ref, *scratch):
    i = pl.program_id(1)
    qs_ref = scratch[0]
    _stack_heads(q_ref, qs_ref)
    n = jnp.minimum(i, DIL_SPAN - 1)

    def step_args(s, first):
        mult = _tile_heads(mult_ref[jnp.minimum(s, DIL_SPAN - 1)])
        return dict(mask_fn=lambda st: jnp.where(mult > 0.0, st, NEG_INF), weight=mult)

    _attend_sequence(lambda j: _pair_scores(k_ref, qs_ref, j), vt_ref, o_ref, scratch,
                     n, lambda s: jnp.maximum(i - s, 0), step_args)


def _dilated(qkv, vt, mult, batch, seq):
    nkb = seq // TB
    return pl.pallas_call(
        _dil_kernel,
        grid=(batch, seq // TB),
        in_specs=_attn_specs(seq, (("q", CB_QC), ("k", CB_KC))) + [
            _vt_spec(nkb),
            pl.BlockSpec((DIL_SPAN, TB, TB), lambda b, i: (0, 0, 0)),
        ],
        out_specs=_o_spec(seq),
        out_shape=jax.ShapeDtypeStruct((batch * seq, GROUP_W), BF16),
        scratch_shapes=_attn_scratch(),
        **_ATTN_PARAMS,
    )(qkv, qkv, vt, mult)


def _moba_kernel(q_ref, k_ref, vt_ref, o_ref, kmean_ref, sel_ref, *scratch, nkb, nkb_pad, topn):
    i = pl.program_id(1)

    @pl.when(i == 0)
    def _():
        kmean_ref[...] = jnp.zeros_like(kmean_ref)
        for n in range(nkb):
            kb = k_ref[n * TB:(n + 1) * TB, :].astype(F32)
            kmean_ref[n:n + 1, :] = jnp.mean(kb, axis=0, keepdims=True)

    qs_ref = scratch[0]
    _stack_heads(q_ref, qs_ref)
    blk_id = lax.broadcasted_iota(jnp.int32, (nkb_pad, HEADS_W), 0)
    blk_f = blk_id.astype(F32)
    g = _dot_nt(kmean_ref[...].astype(BF16), qs_ref[...])
    g = jnp.where(blk_id < i, g, NEG_INF)
    sel = jnp.zeros_like(g)
    for _ in range(topn):
        mx = jnp.max(g, axis=0, keepdims=True)
        is_max = jnp.logical_and(g == mx, mx > NEG_INF)
        first = jnp.min(jnp.where(is_max, blk_f, float(nkb_pad)), axis=0, keepdims=True)
        pick = blk_f == first
        sel = jnp.where(pick, 1.0, sel)
        g = jnp.where(pick, NEG_INF, g)
    sel_ref[...] = sel

    block_of = _diagonal_first(i)

    def step_args(s, diag):
        if diag:
            return dict(mask_fn=lambda st: jnp.where(_causal_mask(), st, NEG_INF))
        return dict(keep=sel_ref[pl.ds(block_of(s), 1), :])

    _attend_sequence(lambda j: _pair_scores(k_ref, qs_ref, j), vt_ref, o_ref, scratch,
                     i, block_of, step_args)


def _moba(qkv, vt, batch, seq):
    nkb = seq // TB
    nkb_pad = -(-nkb // 8) * 8
    kern = functools.partial(_moba_kernel, nkb=nkb, nkb_pad=nkb_pad, topn=min(MOBA_TOPK, nkb))
    return pl.pallas_call(
        kern,
        grid=(batch, seq // TB),
        in_specs=_attn_specs(seq, (("q", CB_QD), ("k", CB_KD))) + [_vt_spec(nkb)],
        out_specs=_o_spec(seq),
        out_shape=jax.ShapeDtypeStruct((batch * seq, GROUP_W), BF16),
        scratch_shapes=[pltpu.VMEM((nkb_pad, GROUP_W), F32),
                        pltpu.VMEM((nkb_pad, HEADS_W), F32)] + _attn_scratch(),
        compiler_params=pltpu.CompilerParams(
            dimension_semantics=("arbitrary", "arbitrary"), vmem_limit_bytes=VMEM_LIMIT),
    )(qkv, qkv, vt)


def _outproj_kernel(oa_ref, ob_ref, oc_ref, od_ref, w_ref, x_ref, g_ref, b_ref, y_ref, *, alpha):
    acc = alpha * x_ref[...]
    for gi, o_ref in enumerate((oa_ref, ob_ref, oc_ref, od_ref)):
        acc = acc + jnp.dot(o_ref[...], w_ref[gi * GROUP_W:(gi + 1) * GROUP_W, :],
                            preferred_element_type=F32)
    y_ref[...] = _layer_norm(acc, g_ref[...], b_ref[...])


def _outproj(os4, w_o, x2d, g, b, alpha):
    m, d = x2d.shape
    tm = min(512, m)
    row = lambda i: (i, 0)
    fixed = lambda i: (0, 0)
    return pl.pallas_call(
        functools.partial(_outproj_kernel, alpha=alpha),
        grid=(m // tm,),
        in_specs=[pl.BlockSpec((tm, GROUP_W), row)] * 4 + [
            pl.BlockSpec((d, d), fixed), pl.BlockSpec((tm, d), row),
            pl.BlockSpec((1, d), fixed), pl.BlockSpec((1, d), fixed)],
        out_specs=pl.BlockSpec((tm, d), row),
        out_shape=jax.ShapeDtypeStruct((m, d), F32),
        compiler_params=pltpu.CompilerParams(
            dimension_semantics=("parallel",), vmem_limit_bytes=VMEM_LIMIT),
    )(*os4, w_o, x2d, g, b)


def _mlp_kernel(x_ref, wu_ref, wd_ref, g_ref, b_ref, y_ref, xb_ref, acc_ref, *, alpha):
    f = pl.program_id(1)

    @pl.when(f == 0)
    def _():
        x = x_ref[...]
        xb_ref[...] = x.astype(BF16)
        acc_ref[...] = alpha * x

    h = jnp.dot(xb_ref[...], wu_ref[...], preferred_element_type=F32)
    h = jnp.square(jnp.maximum(h, 0.0))
    acc_ref[...] += jnp.dot(h.astype(BF16), wd_ref[...], preferred_element_type=F32)

    @pl.when(f == pl.num_programs(1) - 1)
    def _():
        y_ref[...] = _layer_norm(acc_ref[...], g_ref[...], b_ref[...])


def _mlp(x2d, w_up, w_down, g, b, alpha):
    m, d = x2d.shape
    dff = w_up.shape[1]
    tm = min(1024, m)
    tf = min(512, dff)
    return pl.pallas_call(
        functools.partial(_mlp_kernel, alpha=alpha),
        grid=(m // tm, dff // tf),
        in_specs=[
            pl.BlockSpec((tm, d), lambda i, f: (i, 0)),
            pl.BlockSpec((d, tf), lambda i, f: (0, f)),
            pl.BlockSpec((tf, d), lambda i, f: (f, 0)),
            pl.BlockSpec((1, d), lambda i, f: (0, 0)),
            pl.BlockSpec((1, d), lambda i, f: (0, 0)),
        ],
        out_specs=pl.BlockSpec((tm, d), lambda i, f: (i, 0)),
        out_shape=jax.ShapeDtypeStruct((m, d), F32),
        scratch_shapes=[pltpu.VMEM((tm, d), BF16), pltpu.VMEM((tm, d), F32)],
        compiler_params=pltpu.CompilerParams(
            dimension_semantics=("parallel", "arbitrary"), vmem_limit_bytes=VMEM_LIMIT),
    )(x2d, w_up, w_down, g, b)


def _column_plan():
    hq = GROUP_W
    sizes = (hq, hq, hq, GROUP_HEADS,
             hq, hq, hq, IDX_HEADS * IDX_DIM, IDX_DIM, IDX_HEADS,
             hq, hq, hq, hq, hq, hq)
    offs = np.concatenate([[0], np.cumsum(sizes)])
    (qa, ka, va, fa, qb, kb, vb, qi, ki, wi, qc, kc, vc, qd, kd, vd) = [
        np.arange(offs[n], offs[n + 1]) for n in range(len(sizes))]
    blocks = {CB_QA: qa, CB_KA: ka, CB_VA: va, CB_VB: vb, CB_VC: vc, CB_VD: vd,
              CB_QB: qb, CB_KB: kb, CB_QC: qc, CB_KC: kc, CB_QD: qd, CB_KD: kd,
              CB_QI: qi, CB_KI8: np.tile(ki, IDX_HEADS)}
    scale = np.ones(N_COLBLK * GROUP_W, np.float32)
    for cb in (CB_QA, CB_QB, CB_QC, CB_QD):
        scale[cb * GROUP_W:(cb + 1) * GROUP_W] = LOG2E * HEAD_DIM ** -0.5
    return np.concatenate([blocks[cb] for cb in range(N_COLBLK)]), scale, np.concatenate([fa, wi])


def _rope_table(seq, dim):
    half = dim // 2
    lane = np.arange(LANES)
    inv = ROPE_THETA ** (-(jnp.arange(half, dtype=F32)) / half)
    ang = jnp.arange(seq, dtype=F32)[:, None] * inv[None, :]
    idx = lane % half
    sign = np.where((lane % dim) < half, -1.0, 1.0).astype(np.float32)
    return jnp.concatenate([jnp.cos(ang)[:, idx], jnp.sin(ang)[:, idx] * sign[None, :]], axis=1)


def _vt_blocks(qkv, cb, batch, seq):
    nkb = seq // TB
    v = qkv[:, cb * GROUP_W:(cb + 1) * GROUP_W].reshape(batch, nkb, TB, GROUP_HEADS, HEAD_DIM)
    pad = jnp.zeros((batch, nkb, TB, GROUP_HEADS, VROWS - HEAD_DIM), v.dtype).at[..., 0].set(1)
    v = jnp.concatenate([v, pad], axis=-1).reshape(batch, nkb, TB, GROUP_HEADS * VROWS)
    return v.transpose(0, 1, 3, 2)


def kernel(x, w_in, b_f, w_o, ln1_g, ln1_b, w_up, w_down, ln2_g, ln2_b):
    batch, seq, d = x.shape
    depth = w_in.shape[0]
    assert seq % TB == 0 and d == 4 * GROUP_W
    chunks = seq // LANES
    assert chunks & (chunks - 1) == 0
    alpha = (2.0 * depth) ** 0.25

    main_cols, main_scale, misc_cols = _column_plan()
    w_main = (w_in[:, :, main_cols] * main_scale).astype(BF16)
    w_misc = jnp.pad(w_in[:, :, misc_cols], ((0, 0), (0, 0), (0, LANES - len(misc_cols)))).astype(BF16)
    w_o16, w_up16, w_down16 = w_o.astype(BF16), w_up.astype(BF16), w_down.astype(BF16)
    tab64 = _rope_table(seq, HEAD_DIM)
    tab32 = _rope_table(seq, IDX_DIM)
    mult = jnp.asarray(_dilated_multiplicity())

    x2d = x.reshape(batch * seq, d)
    for l in range(depth):
        qkv, misc = _inproj(x2d, w_main[l], w_misc[l], tab64, tab32, seq)
        misc3 = misc.reshape(batch, seq, LANES)
        z = misc3[:, :, 0:GROUP_HEADS].transpose(0, 2, 1).reshape(batch, GROUP_HEADS * chunks, LANES)
        bias_rows = jnp.repeat(b_f[l].astype(F32), chunks)[:, None]
        gb = _gate_bias(z, bias_rows, chunks)
        gb = gb.reshape(batch, BIAS_TERMS, GROUP_HEADS, seq).transpose(0, 3, 2, 1)
        cbias = jnp.pad(gb.reshape(batch * seq, GROUP_HEADS * BIAS_TERMS),
                        ((0, 0), (0, LANES - GROUP_HEADS * BIAS_TERMS))).astype(BF16)
        wrow = misc3[:, :, GROUP_HEADS:GROUP_HEADS + IDX_HEADS].transpose(0, 2, 1)

        oa = _fox(qkv, cbias, _vt_blocks(qkv, CB_VA, batch, seq), batch, seq)
        ob = _dsa(qkv, _vt_blocks(qkv, CB_VB, batch, seq), wrow, batch, seq)
        oc = _dilated(qkv, _vt_blocks(qkv, CB_VC, batch, seq), mult, batch, seq)
        od = _moba(qkv, _vt_blocks(qkv, CB_VD, batch, seq), batch, seq)

        x2d = _outproj((oa, ob, oc, od), w_o16[l], x2d, ln1_g[l][None, :], ln1_b[l][None, :], alpha)
        x2d = _mlp(x2d, w_up16[l], w_down16[l], ln2_g[l][None, :], ln2_b[l][None, :], alpha)
    return x2d.reshape(batch, seq, d)
```

```python
import functools
import math

import numpy as np
import jax
import jax.numpy as jnp
from jax import lax
from jax.experimental import pallas as pl
from jax.experimental.pallas import tpu as pltpu

F32 = jnp.float32
BF16 = jnp.bfloat16

HEAD_DIM = 64
GROUP_HEADS = 4
GROUP_W = GROUP_HEADS * HEAD_DIM
IDX_HEADS = 8
IDX_DIM = 32
DSA_TOPK = 256
DIL_PATTERNS = ((128, 1), (512, 4), (2048, 16))
MOBA_BLOCK = 256
MOBA_TOPK = 3
ROPE_THETA = 10000.0
LN_EPS = 1e-5
LOG2E = math.log2(math.e)
LANES = 128
BF16_ROWS = 16
TB = 256
VROWS = HEAD_DIM + BF16_ROWS
NEG_INF = float("-inf")
VMEM_LIMIT = 56 * 1024 * 1024

(CB_QA, CB_KA, CB_VA, CB_VB, CB_VC, CB_VD,
 CB_QB, CB_KB, CB_QC, CB_KC, CB_QD, CB_KD,
 CB_QI, CB_KI8) = range(14)
N_COLBLK = 14
PROJ_TN = 2 * GROUP_W
PLAIN_STEPS, ROPE64_STEPS = 3, 3


def _dot_nt(a, b):
    return lax.dot_general(a, b, (((1,), (1,)), ((), ())), preferred_element_type=F32)


def _layer_norm(y, g, b):
    mu = jnp.mean(y, axis=-1, keepdims=True)
    d = y - mu
    var = jnp.mean(d * d, axis=-1, keepdims=True)
    return d * lax.rsqrt(var + LN_EPS) * g + b


def _tree_sum(parts):
    while len(parts) > 1:
        parts = [parts[n] + parts[n + 1] for n in range(0, len(parts), 2)]
    return parts[0]


def _rope(acc, tab_ref, half):
    cos = tab_ref[:, 0:LANES]
    sin = tab_ref[:, LANES:2 * LANES]
    lane = lax.broadcasted_iota(jnp.int32, (1, LANES), 1)
    first = (lane & (2 * half - 1)) < half
    outs = []
    for c in range(acc.shape[1] // LANES):
        a = acc[:, c * LANES:(c + 1) * LANES]
        swapped = jnp.where(first, pltpu.roll(a, LANES - half, 1), pltpu.roll(a, half, 1))
        outs.append(a * cos + swapped * sin)
    return jnp.concatenate(outs, axis=1)


def _inproj_kernel(x_ref, w_ref, wm_ref, t64_ref, t32_ref, o_ref, misc_ref, xb_ref):
    j = pl.program_id(1)

    @pl.when(j == 0)
    def _():
        xb = x_ref[...].astype(BF16)
        xb_ref[...] = xb
        misc_ref[...] = jnp.dot(xb, wm_ref[...], preferred_element_type=F32)

    acc = jnp.dot(xb_ref[...], w_ref[...], preferred_element_type=F32)

    @pl.when(j < PLAIN_STEPS)
    def _():
        o_ref[...] = acc.astype(BF16)

    @pl.when(jnp.logical_and(j >= PLAIN_STEPS, j < PLAIN_STEPS + ROPE64_STEPS))
    def _():
        o_ref[...] = _rope(acc, t64_ref, HEAD_DIM // 2).astype(BF16)

    @pl.when(j == PLAIN_STEPS + ROPE64_STEPS)
    def _():
        o_ref[...] = _rope(acc, t32_ref, IDX_DIM // 2).astype(BF16)


def _inproj(x2d, w_main, w_misc, tab64, tab32, seq):
    m, d = x2d.shape
    tm = min(1024, seq)
    nt = seq // tm
    return pl.pallas_call(
        _inproj_kernel,
        grid=(m // tm, N_COLBLK * GROUP_W // PROJ_TN),
        in_specs=[
            pl.BlockSpec((tm, d), lambda i, j: (i, 0)),
            pl.BlockSpec((d, PROJ_TN), lambda i, j: (0, j)),
            pl.BlockSpec((d, LANES), lambda i, j: (0, 0)),
            pl.BlockSpec((tm, 2 * LANES), lambda i, j: (i % nt, 0)),
            pl.BlockSpec((tm, 2 * LANES), lambda i, j: (i % nt, 0)),
        ],
        out_specs=[
            pl.BlockSpec((tm, PROJ_TN), lambda i, j: (i, j)),
            pl.BlockSpec((tm, LANES), lambda i, j: (i, 0)),
        ],
        out_shape=[
            jax.ShapeDtypeStruct((m, N_COLBLK * GROUP_W), BF16),
            jax.ShapeDtypeStruct((m, LANES), F32),
        ],
        scratch_shapes=[pltpu.VMEM((tm, d), BF16)],
        compiler_params=pltpu.CompilerParams(
            dimension_semantics=("parallel", "arbitrary"), vmem_limit_bytes=VMEM_LIMIT),
    )(x2d, w_main, w_misc, tab64, tab32)


def _gate_kernel(z_ref, b_ref, o_ref, *, chunks):
    z = z_ref[...] + b_ref[...]
    w = jnp.minimum(z, 0.0) - jnp.log1p(jnp.exp(-jnp.abs(z)))
    rows = w.shape[0]
    lane = lax.broadcasted_iota(jnp.int32, (rows, LANES), 1)
    s = 1
    while s < LANES:
        w = w + jnp.where(lane >= s, pltpu.roll(w, s, 1), 0.0)
        s *= 2
    tot = jnp.broadcast_to(w[:, LANES - 1:LANES], (rows, LANES))
    pos = lax.broadcasted_iota(jnp.int32, (rows, LANES), 0) & (chunks - 1)
    run = tot
    s = 1
    while s < chunks:
        run = run + jnp.where(pos >= s, pltpu.roll(run, s, 0), 0.0)
        s *= 2
    bias = -LOG2E * (w + (run - tot))
    hi = bias.astype(BF16).astype(F32)
    mid = (bias - hi).astype(BF16).astype(F32)
    lo = (bias - hi - mid).astype(BF16).astype(F32)
    o_ref[0] = hi
    o_ref[1] = mid
    o_ref[2] = lo


def _gate_bias(z, bias_rows, chunks):
    b, rows, _ = z.shape
    return pl.pallas_call(
        functools.partial(_gate_kernel, chunks=chunks),
        grid=(b,),
        in_specs=[pl.BlockSpec((None, rows, LANES), lambda i: (i, 0, 0)),
                  pl.BlockSpec((rows, 1), lambda i: (0, 0))],
        out_specs=pl.BlockSpec((None, 3, rows, LANES), lambda i: (i, 0, 0, 0)),
        out_shape=jax.ShapeDtypeStruct((b, 3, rows, LANES), F32),
        compiler_params=pltpu.CompilerParams(dimension_semantics=("parallel",)),
    )(z, bias_rows)


HEADS_W = GROUP_HEADS * TB


def _stack_heads(q_ref, qs_ref):
    q = q_ref[...].astype(F32)
    head_of_lane = lax.broadcasted_iota(jnp.int32, (1, GROUP_W), 1) >> 6
    for h in range(GROUP_HEADS):
        qs_ref[h * TB:(h + 1) * TB, 0:GROUP_W] = jnp.where(head_of_lane == h, q, 0.0).astype(BF16)


def _head(x, h):
    return x[:, h * TB:(h + 1) * TB]


def _tile_heads(x):
    return jnp.concatenate([x] * GROUP_HEADS, axis=1)


def _kblock(k_ref, j):
    return k_ref[pl.ds(pl.multiple_of(j * TB, TB), TB), :]


def _pair_scores(k_ref, qs_ref, j, extra_ref=None):
    kblk = _kblock(k_ref, j)
    outs = []
    for pr in range(GROUP_W // LANES):
        rows = slice(2 * pr * TB, 2 * (pr + 1) * TB)
        kp = kblk[:, pr * LANES:(pr + 1) * LANES]
        qp = qs_ref[rows, pr * LANES:(pr + 1) * LANES]
        if extra_ref is not None:
            kp = jnp.concatenate([kp, _kblock(extra_ref, j)], axis=1)
            qp = jnp.concatenate([qp, qs_ref[rows, GROUP_W:GROUP_W + LANES]], axis=1)
        outs.append(_dot_nt(kp, qp))
    return jnp.concatenate(outs, axis=1)


def _causal_mask(heads=GROUP_HEADS):
    krow = lax.broadcasted_iota(jnp.int32, (TB, heads * TB), 0)
    qcol = lax.broadcasted_iota(jnp.int32, (TB, heads * TB), 1) & (TB - 1)
    return krow <= qcol


def _consume(st, blk_max, vt, m, accs, weight=None, keep=None):
    m_new = jnp.maximum(m, blk_max)
    if keep is not None:
        m_new = jnp.where(keep > 0.5, m_new, m)
    m_safe = jnp.where(m_new == NEG_INF, 0.0, m_new)
    alpha = jnp.exp2(m - m_safe)
    p = jnp.exp2(st - m_safe)
    if weight is not None:
        p = p * weight
    pb = p.astype(BF16)
    new_accs = []
    for h in range(GROUP_HEADS):
        pv = jnp.dot(vt[h * VROWS:(h + 1) * VROWS, :], _head(pb, h), preferred_element_type=F32)
        upd = _head(alpha, h) * accs[h] + pv
        new_accs.append(upd if keep is None else jnp.where(_head(keep, h) > 0.5, upd, accs[h]))
    return m_new, new_accs


TRIP_SHIFT = 3
STEPS_PER_TRIP = 1 << TRIP_SHIFT


def _attn_scratch(qs_width=GROUP_W):
    return [
        pltpu.VMEM((HEADS_W, qs_width), BF16),
        pltpu.VMEM((2, TB, HEADS_W), F32),
        pltpu.VMEM((2, 1, HEADS_W), F32),
        pltpu.VMEM((1, HEADS_W), F32),
        pltpu.VMEM((GROUP_HEADS * VROWS, TB), F32),
    ]


def _attend_sequence(scores_of, vt_ref, o_ref, scratch, n_steps, block_of, step_args):
    _, st_ref, bm_ref, m_ref, acc_ref = scratch

    def issue(s, slot, first=False):
        st = scores_of(block_of(s))
        mask_fn = step_args(s, first).get("mask_fn")
        if mask_fn is not None:
            st = mask_fn(st)
        st_ref[slot] = st
        bm_ref[slot] = jnp.max(st, axis=0, keepdims=True)

    def consume(s, slot, first=False):
        args = step_args(s, first)
        accs = [acc_ref[h * VROWS:(h + 1) * VROWS, :] for h in range(GROUP_HEADS)]
        m, accs = _consume(st_ref[slot], bm_ref[slot], vt_ref[block_of(s)], m_ref[...], accs,
                           weight=args.get("weight"), keep=args.get("keep"))
        m_ref[...] = m
        for h in range(GROUP_HEADS):
            acc_ref[h * VROWS:(h + 1) * VROWS, :] = accs[h]

    m_ref[...] = jnp.full((1, HEADS_W), NEG_INF, F32)
    acc_ref[...] = jnp.zeros_like(acc_ref)
    issue(0, 0, True)
    issue(1, 1)
    consume(0, 0, True)

    def run(s, count):
        for u in range(count):
            issue(s + u + 1, u & 1)
            consume(s + u, (u + 1) & 1)

    def trip(tt, carry):
        run(1 + STEPS_PER_TRIP * tt, STEPS_PER_TRIP)
        return carry

    n_trips = n_steps >> TRIP_SHIFT
    lax.fori_loop(0, n_trips, trip, 0)

    for bit in range(TRIP_SHIFT - 1, 0, -1):
        size = 1 << bit

        @pl.when((n_steps & size) != 0)
        def _(size=size):
            run(1 + ((n_steps >> TRIP_SHIFT) << TRIP_SHIFT) + (n_steps & (STEPS_PER_TRIP - 2 * size)), size)

    @pl.when((n_steps & 1) == 1)
    def _():
        consume(n_steps, 1)

    ot = jnp.concatenate(
        [acc_ref[h * VROWS:h * VROWS + HEAD_DIM, :] / acc_ref[h * VROWS + HEAD_DIM:h * VROWS + HEAD_DIM + 1, :]
         for h in range(GROUP_HEADS)], axis=0)
    o_ref[...] = ot.T.astype(o_ref.dtype)


def _diagonal_first(i):
    return lambda s: jnp.where(s == 0, i, s - 1)


def _attn_specs(seq, colblks):
    nq = seq // TB
    specs = []
    for kind, cb in colblks:
        if kind == "q":
            specs.append(pl.BlockSpec((TB, GROUP_W), lambda b, i, cb=cb: (b * nq + i, cb)))
        else:
            specs.append(pl.BlockSpec((seq, GROUP_W), lambda b, i, cb=cb: (b, cb)))
    return specs


def _vt_spec(nkb, cb):
    return pl.BlockSpec((None, nkb, GROUP_HEADS * VROWS, TB), lambda b, i: (b, 0, cb - CB_VA, 0))


def _o_spec(seq):
    nq = seq // TB
    return pl.BlockSpec((TB, GROUP_W), lambda b, i: (b * nq + i, 0))


_ATTN_PARAMS = dict(
    compiler_params=pltpu.CompilerParams(
        dimension_semantics=("parallel", "arbitrary"), vmem_limit_bytes=VMEM_LIMIT))


BIAS_TERMS = 3


def _fox_kernel(q_ref, k_ref, cb_ref, vt_ref, o_ref, *scratch):
    i = pl.program_id(1)
    qs_ref = scratch[0]
    _stack_heads(q_ref, qs_ref)
    lane = lax.broadcasted_iota(jnp.int32, (TB, LANES), 1)
    for h in range(GROUP_HEADS):
        pick = jnp.logical_and(lane >= BIAS_TERMS * h, lane < BIAS_TERMS * (h + 1))
        qs_ref[h * TB:(h + 1) * TB, GROUP_W:GROUP_W + LANES] = jnp.where(pick, 1.0, 0.0).astype(BF16)

    def scores_of(j):
        return _pair_scores(k_ref, qs_ref, j, cb_ref)

    def step_args(s, diag):
        if diag:
            return dict(mask_fn=lambda st: jnp.where(_causal_mask(), st, NEG_INF))
        return {}

    _attend_sequence(scores_of, vt_ref, o_ref, scratch, i, _diagonal_first(i), step_args)


def _fox(qkv, cbias, vt, batch, seq):
    nkb = seq // TB
    return pl.pallas_call(
        _fox_kernel,
        grid=(batch, seq // TB),
        in_specs=_attn_specs(seq, (("q", CB_QA), ("k", CB_KA))) + [
            pl.BlockSpec((seq, LANES), lambda b, i: (b, 0)),
            _vt_spec(nkb, CB_VA),
        ],
        out_specs=_o_spec(seq),
        out_shape=jax.ShapeDtypeStruct((batch * seq, GROUP_W), BF16),
        scratch_shapes=_attn_scratch(GROUP_W + LANES),
        **_ATTN_PARAMS,
    )(qkv, qkv, cbias, vt)


HIGH_HALF = -(1 << 16)


def _key_to_float(key_u):
    key = key_u ^ jnp.int32(-2 ** 31)
    bits = jnp.where(key < 0, key ^ jnp.int32(0x7FFFFFFF), key)
    below_neg_inf = jnp.logical_and(key_u >= 0, key_u < jnp.int32(0x007FFFFF))
    return jnp.where(below_neg_inf, NEG_INF, lax.bitcast_convert_type(bits, F32))


def _dsa_kernel(q_ref, k_ref, vt_ref, qi_ref, k8_ref, w_ref, o_ref, strip_ref, strip_hi_ref, qm_ref, *scratch,
                topk, idx_bits):
    i = pl.program_id(1)
    nblk = i + 1

    qi = qi_ref[...].astype(F32)
    low_lanes = lax.broadcasted_iota(jnp.int32, (1, LANES), 1) < IDX_DIM
    per_half = LANES // IDX_DIM
    for h in range(IDX_HEADS):
        half = qi[:, (h // per_half) * LANES:(h // per_half + 1) * LANES]
        if h % per_half:
            half = pltpu.roll(half, LANES - IDX_DIM * (h % per_half), 1)
        qm_ref[h] = jnp.where(low_lanes, half, 0.0).astype(BF16)

    wrow = w_ref[...]

    def score_block(j, diag):
        kblk = _kblock(k8_ref, j)[:, 0:LANES]
        sc = _tree_sum([wrow[h:h + 1, :] * jnp.maximum(_dot_nt(kblk, qm_ref[h]), 0.0)
                        for h in range(IDX_HEADS)])
        if diag:
            sc = jnp.where(_causal_mask(1), sc, NEG_INF)
        strip_ref[j] = sc
        strip_hi_ref[j] = lax.bitcast_convert_type(
            lax.bitcast_convert_type(sc, jnp.int32) & HIGH_HALF, F32).astype(BF16)

    def score_pair(p, carry):
        score_block(2 * p, False)
        score_block(2 * p + 1, False)
        return carry

    lax.fori_loop(0, i >> 1, score_pair, 0)

    @pl.when((i & 1) == 1)
    def _():
        score_block(i - 1, False)

    score_block(i, True)

    def count(pred):
        def body(j, acc):
            hit = jnp.where(pred(j, strip_ref[j]), 1.0, 0.0)
            return acc + _tree_sum([hit[r:r + 8, :] for r in range(0, TB, 8)])
        acc = lax.fori_loop(0, nblk, body, jnp.zeros((8, TB), F32))
        return jnp.sum(acc, axis=0, keepdims=True)

    def count_ge(ref, t, rows):
        one, zero = jnp.ones((), ref.dtype), jnp.zeros((), ref.dtype)

        def hits(j, tt):
            hit = jnp.where(ref[j] >= tt, one, zero)
            return _tree_sum([hit[r:r + rows, :] for r in range(0, TB, rows)]).astype(F32)

        def body(p, acc):
            j1 = 2 * p + 1
            t1 = jnp.where(j1 < nblk, t, jnp.full_like(t, jnp.inf))
            return acc + hits(2 * p, t) + hits(jnp.minimum(j1, nblk - 1), t1)

        acc = lax.fori_loop(0, (nblk + 1) >> 1, body, jnp.zeros((rows, TB), F32))
        return jnp.sum(acc, axis=0, keepdims=True)

    def radix_pass(it, state, high):
        prefix, cnt_at_prefix = state
        cand = prefix | jnp.left_shift(jnp.int32(1), 31 - it)
        t = _key_to_float(cand)
        if high:
            t = lax.bitcast_convert_type(lax.bitcast_convert_type(t, jnp.int32) & HIGH_HALF, F32)
            cnt = count_ge(strip_hi_ref, t.astype(BF16), BF16_ROWS)
        else:
            cnt = count_ge(strip_ref, t, 8)
        take = cnt >= topk
        return jnp.where(take, cand, prefix), jnp.where(take, cnt, cnt_at_prefix)

    n_all = jnp.full((1, TB), 1.0, F32) * (nblk * TB).astype(F32)
    state = (jnp.zeros((1, TB), jnp.int32), n_all)
    state = lax.fori_loop(0, 16, lambda it, s: radix_pass(it, s, True), state)
    prefix, cnt_ge = lax.fori_loop(16, 32, lambda it, s: radix_pass(it, s, False), state)
    thr = _key_to_float(prefix)

    @pl.when(jnp.max(cnt_ge) > topk)
    def _():
        need = topk - count(lambda j, blk: blk > thr)
        krow = lax.broadcasted_iota(jnp.int32, (TB, TB), 0)

        def idx_body(it, lim):
            cand = lim | jnp.left_shift(jnp.int32(1), idx_bits - 1 - it)
            cnt = count(lambda j, blk: jnp.logical_and(blk == thr, krow + j * TB < cand))
            return jnp.where(cnt <= need, cand, lim)

        lim = lax.fori_loop(0, idx_bits, idx_body, jnp.zeros((1, TB), jnp.int32))

        def drop_body(j, carry):
            blk = strip_ref[j]
            drop = jnp.logical_and(blk == thr, krow + j * TB >= lim)
            strip_ref[j] = jnp.where(drop, NEG_INF, blk)
            return carry

        lax.fori_loop(0, nblk, drop_body, 0)

    qs_ref = scratch[0]
    _stack_heads(q_ref, qs_ref)

    block_of = _diagonal_first(i)

    def step_args(s, diag):
        def mask_fn(st):
            blk = strip_ref[block_of(s)]
            sel = blk >= thr
            if diag:
                sel = jnp.logical_and(sel, blk > NEG_INF)
            return jnp.concatenate(
                [jnp.where(sel, _head(st, h), NEG_INF) for h in range(GROUP_HEADS)], axis=1)
        return dict(mask_fn=mask_fn)

    _attend_sequence(lambda j: _pair_scores(k_ref, qs_ref, j), vt_ref, o_ref, scratch,
                     i, block_of, step_args)


def _dsa(qkv, vt, wrow, batch, seq):
    nkb = seq // TB
    topk = min(DSA_TOPK, seq // 4)
    kern = functools.partial(_dsa_kernel, topk=float(topk), idx_bits=int(seq).bit_length())
    return pl.pallas_call(
        kern,
        grid=(batch, seq // TB),
        in_specs=_attn_specs(seq, (("q", CB_QB), ("k", CB_KB))) + [_vt_spec(nkb, CB_VB)]
        + _attn_specs(seq, (("q", CB_QI), ("k", CB_KI8)))
        + [pl.BlockSpec((None, IDX_HEADS, TB), lambda b, i: (b, 0, i))],
        out_specs=_o_spec(seq),
        out_shape=jax.ShapeDtypeStruct((batch * seq, GROUP_W), BF16),
        scratch_shapes=[pltpu.VMEM((nkb, TB, TB), F32), pltpu.VMEM((nkb, TB, TB), BF16),
                        pltpu.VMEM((IDX_HEADS, TB, LANES), BF16)] + _attn_scratch(),
        **_ATTN_PARAMS,
    )(qkv, qkv, vt, qkv, qkv, wrow)


DIL_SPAN = max(w for w, _ in DIL_PATTERNS) // TB + 1


def _dilated_multiplicity():
    k = np.arange(TB)[:, None]
    q = np.arange(TB)[None, :]
    out = np.zeros((DIL_SPAN, TB, TB), np.float32)
    for delta in range(DIL_SPAN):
        d = q - k + TB * delta
        for w, r in DIL_PATTERNS:
            out[delta] += ((d >= 0) & (d % r == 0) & (d <= w)).astype(np.float32)
    return out


def _dil_kernel(q_ref, k_ref, vt_ref, mult_ref, o_ref, *scratch):
    i = pl.program_id(1)
    qs_ref = scratch[0]
    _stack_heads(q_ref, qs_ref)
    n = jnp.minimum(i, DIL_SPAN - 1)

    def step_args(s, first):
        mult = _tile_heads(mult_ref[jnp.minimum(s, DIL_SPAN - 1)])
        return dict(mask_fn=lambda st: jnp.where(mult > 0.0, st, NEG_INF), weight=mult)

    _attend_sequence(lambda j: _pair_scores(k_ref, qs_ref, j), vt_ref, o_ref, scratch,
                     n, lambda s: jnp.maximum(i - s, 0), step_args)


def _dilated(qkv, vt, mult, batch, seq):
    nkb = seq // TB
    return pl.pallas_call(
        _dil_kernel,
        grid=(batch, seq // TB),
        in_specs=_attn_specs(seq, (("q", CB_QC), ("k", CB_KC))) + [
            _vt_spec(nkb, CB_VC),
            pl.BlockSpec((DIL_SPAN, TB, TB), lambda b, i: (0, 0, 0)),
        ],
        out_specs=_o_spec(seq),
        out_shape=jax.ShapeDtypeStruct((batch * seq, GROUP_W), BF16),
        scratch_shapes=_attn_scratch(),
        **_ATTN_PARAMS,
    )(qkv, qkv, vt, mult)


def _moba_kernel(q_ref, k_ref, vt_ref, o_ref, kmean_ref, sel_ref, *scratch, nkb, nkb_pad, topn):
    i = pl.program_id(1)

    @pl.when(i == 0)
    def _():
        kmean_ref[...] = jnp.zeros_like(kmean_ref)
        for n in range(nkb):
            kb = k_ref[n * TB:(n + 1) * TB, :].astype(F32)
            kmean_ref[n:n + 1, :] = jnp.mean(kb, axis=0, keepdims=True)

    qs_ref = scratch[0]
    _stack_heads(q_ref, qs_ref)
    blk_id = lax.broadcasted_iota(jnp.int32, (nkb_pad, HEADS_W), 0)
    blk_f = blk_id.astype(F32)
    g = _dot_nt(kmean_ref[...].astype(BF16), qs_ref[...])
    g = jnp.where(blk_id < i, g, NEG_INF)
    sel = jnp.zeros_like(g)
    for _ in range(topn):
        mx = jnp.max(g, axis=0, keepdims=True)
        is_max = jnp.logical_and(g == mx, mx > NEG_INF)
        first = jnp.min(jnp.where(is_max, blk_f, float(nkb_pad)), axis=0, keepdims=True)
        pick = blk_f == first
        sel = jnp.where(pick, 1.0, sel)
        g = jnp.where(pick, NEG_INF, g)
    sel_ref[...] = sel

    block_of = _diagonal_first(i)

    def step_args(s, diag):
        if diag:
            return dict(mask_fn=lambda st: jnp.where(_causal_mask(), st, NEG_INF))
        return dict(keep=sel_ref[pl.ds(block_of(s), 1), :])

    _attend_sequence(lambda j: _pair_scores(k_ref, qs_ref, j), vt_ref, o_ref, scratch,
                     i, block_of, step_args)


def _moba(qkv, vt, batch, seq):
    nkb = seq // TB
    nkb_pad = -(-nkb // 8) * 8
    kern = functools.partial(_moba_kernel, nkb=nkb, nkb_pad=nkb_pad, topn=min(MOBA_TOPK, nkb))
    return pl.pallas_call(
        kern,
        grid=(batch, seq // TB),
        in_specs=_attn_specs(seq, (("q", CB_QD), ("k", CB_KD))) + [_vt_spec(nkb, CB_VD)],
        out_specs=_o_spec(seq),
        out_shape=jax.ShapeDtypeStruct((batch * seq, GROUP_W), BF16),
        scratch_shapes=[pltpu.VMEM((nkb_pad, GROUP_W), F32),
                        pltpu.VMEM((nkb_pad, HEADS_W), F32)] + _attn_scratch(),
        compiler_params=pltpu.CompilerParams(
            dimension_semantics=("arbitrary", "arbitrary"), vmem_limit_bytes=VMEM_LIMIT),
    )(qkv, qkv, vt)


def _outproj_kernel(oa_ref, ob_ref, oc_ref, od_ref, w_ref, x_ref, g_ref, b_ref, y_ref, *, alpha):
    acc = alpha * x_ref[...]
    for gi, o_ref in enumerate((oa_ref, ob_ref, oc_ref, od_ref)):
        acc = acc + jnp.dot(o_ref[...], w_ref[gi * GROUP_W:(gi + 1) * GROUP_W, :],
                            preferred_element_type=F32)
    y_ref[...] = _layer_norm(acc, g_ref[...], b_ref[...])


def _outproj(os4, w_o, x2d, g, b, alpha):
    m, d = x2d.shape
    tm = min(512, m)
    row = lambda i: (i, 0)
    fixed = lambda i: (0, 0)
    return pl.pallas_call(
        functools.partial(_outproj_kernel, alpha=alpha),
        grid=(m // tm,),
        in_specs=[pl.BlockSpec((tm, GROUP_W), row)] * 4 + [
            pl.BlockSpec((d, d), fixed), pl.BlockSpec((tm, d), row),
            pl.BlockSpec((1, d), fixed), pl.BlockSpec((1, d), fixed)],
        out_specs=pl.BlockSpec((tm, d), row),
        out_shape=jax.ShapeDtypeStruct((m, d), F32),
        compiler_params=pltpu.CompilerParams(
            dimension_semantics=("parallel",), vmem_limit_bytes=VMEM_LIMIT),
    )(*os4, w_o, x2d, g, b)


def _mlp_kernel(x_ref, wu_ref, wd_ref, g_ref, b_ref, y_ref, xb_ref, acc_ref, *, alpha):
    f = pl.program_id(1)

    @pl.when(f == 0)
    def _():
        x = x_ref[...]
        xb_ref[...] = x.astype(BF16)
        acc_ref[...] = alpha * x

    h = jnp.dot(xb_ref[...], wu_ref[...], preferred_element_type=F32)
    h = jnp.square(jnp.maximum(h, 0.0))
    acc_ref[...] += jnp.dot(h.astype(BF16), wd_ref[...], preferred_element_type=F32)

    @pl.when(f == pl.num_programs(1) - 1)
    def _():
        y_ref[...] = _layer_norm(acc_ref[...], g_ref[...], b_ref[...])


def _mlp(x2d, w_up, w_down, g, b, alpha):
    m, d = x2d.shape
    dff = w_up.shape[1]
    tm = min(1024, m)
    tf = min(512, dff)
    return pl.pallas_call(
        functools.partial(_mlp_kernel, alpha=alpha),
        grid=(m // tm, dff // tf),
        in_specs=[
            pl.BlockSpec((tm, d), lambda i, f: (i, 0)),
            pl.BlockSpec((d, tf), lambda i, f: (0, f)),
            pl.BlockSpec((tf, d), lambda i, f: (f, 0)),
            pl.BlockSpec((1, d), lambda i, f: (0, 0)),
            pl.BlockSpec((1, d), lambda i, f: (0, 0)),
        ],
        out_specs=pl.BlockSpec((tm, d), lambda i, f: (i, 0)),
        out_shape=jax.ShapeDtypeStruct((m, d), F32),
        scratch_shapes=[pltpu.VMEM((tm, d), BF16), pltpu.VMEM((tm, d), F32)],
        compiler_params=pltpu.CompilerParams(
            dimension_semantics=("parallel", "arbitrary"), vmem_limit_bytes=VMEM_LIMIT),
    )(x2d, w_up, w_down, g, b)


def _column_plan():
    hq = GROUP_W
    sizes = (hq, hq, hq, GROUP_HEADS,
             hq, hq, hq, IDX_HEADS * IDX_DIM, IDX_DIM, IDX_HEADS,
             hq, hq, hq, hq, hq, hq)
    offs = np.concatenate([[0], np.cumsum(sizes)])
    (qa, ka, va, fa, qb, kb, vb, qi, ki, wi, qc, kc, vc, qd, kd, vd) = [
        np.arange(offs[n], offs[n + 1]) for n in range(len(sizes))]
    blocks = {CB_QA: qa, CB_KA: ka, CB_VA: va, CB_VB: vb, CB_VC: vc, CB_VD: vd,
              CB_QB: qb, CB_KB: kb, CB_QC: qc, CB_KC: kc, CB_QD: qd, CB_KD: kd,
              CB_QI: qi, CB_KI8: np.tile(ki, IDX_HEADS)}
    scale = np.ones(N_COLBLK * GROUP_W, np.float32)
    for cb in (CB_QA, CB_QB, CB_QC, CB_QD):
        scale[cb * GROUP_W:(cb + 1) * GROUP_W] = LOG2E * HEAD_DIM ** -0.5
    return np.concatenate([blocks[cb] for cb in range(N_COLBLK)]), scale, np.concatenate([fa, wi])


def _rope_table(seq, dim):
    half = dim // 2
    lane = np.arange(LANES)
    inv = ROPE_THETA ** (-(jnp.arange(half, dtype=F32)) / half)
    ang = jnp.arange(seq, dtype=F32)[:, None] * inv[None, :]
    idx = lane % half
    sign = np.where((lane % dim) < half, -1.0, 1.0).astype(np.float32)
    return jnp.concatenate([jnp.cos(ang)[:, idx], jnp.sin(ang)[:, idx] * sign[None, :]], axis=1)


def _vt_all(qkv, batch, seq):
    nkb = seq // TB
    heads = (CB_VD - CB_VA + 1) * GROUP_HEADS
    v = qkv[:, CB_VA * GROUP_W:(CB_VD + 1) * GROUP_W].reshape(batch, nkb, TB, heads, HEAD_DIM)
    pad = jnp.zeros((batch, nkb, TB, heads, VROWS - HEAD_DIM), v.dtype).at[..., 0].set(1)
    v = jnp.concatenate([v, pad], axis=-1).reshape(batch, nkb, TB, heads * VROWS)
    return v.transpose(0, 1, 3, 2)


def kernel(x, w_in, b_f, w_o, ln1_g, ln1_b, w_up, w_down, ln2_g, ln2_b):
    batch, seq, d = x.shape
    depth = w_in.shape[0]
    assert seq % TB == 0 and d == 4 * GROUP_W
    chunks = seq // LANES
    assert chunks & (chunks - 1) == 0
    alpha = (2.0 * depth) ** 0.25

    main_cols, main_scale, misc_cols = _column_plan()
    w_main = (w_in[:, :, main_cols] * main_scale).astype(BF16)
    w_misc = jnp.pad(w_in[:, :, misc_cols], ((0, 0), (0, 0), (0, LANES - len(misc_cols)))).astype(BF16)
    w_o16, w_up16, w_down16 = w_o.astype(BF16), w_up.astype(BF16), w_down.astype(BF16)
    tab64 = _rope_table(seq, HEAD_DIM)
    tab32 = _rope_table(seq, IDX_DIM)
    mult = jnp.asarray(_dilated_multiplicity())

    x2d = x.reshape(batch * seq, d)
    for l in range(depth):
        qkv, misc = _inproj(x2d, w_main[l], w_misc[l], tab64, tab32, seq)
        misc3 = misc.reshape(batch, seq, LANES)
        z = misc3[:, :, 0:GROUP_HEADS].transpose(0, 2, 1).reshape(batch, GROUP_HEADS * chunks, LANES)
        bias_rows = jnp.repeat(b_f[l].astype(F32), chunks)[:, None]
        gb = _gate_bias(z, bias_rows, chunks)
        gb = gb.reshape(batch, BIAS_TERMS, GROUP_HEADS, seq).transpose(0, 3, 2, 1)
        cbias = jnp.pad(gb.reshape(batch * seq, GROUP_HEADS * BIAS_TERMS),
                        ((0, 0), (0, LANES - GROUP_HEADS * BIAS_TERMS))).astype(BF16)
        wrow = misc3[:, :, GROUP_HEADS:GROUP_HEADS + IDX_HEADS].transpose(0, 2, 1)

        vt = _vt_all(qkv, batch, seq)
        oa = _fox(qkv, cbias, vt, batch, seq)
        ob = _dsa(qkv, vt, wrow, batch, seq)
        oc = _dilated(qkv, vt, mult, batch, seq)
        od = _moba(qkv, vt, batch, seq)

        x2d = _outproj((oa, ob, oc, od), w_o16[l], x2d, ln1_g[l][None, :], ln1_b[l][None, :], alpha)
        x2d = _mlp(x2d, w_up16[l], w_down16[l], ln2_g[l][None, :], ln2_b[l][None, :], alpha)
    return x2d.reshape(batch, seq, d)
```

```python
import functools
import math

import numpy as np
import jax
import jax.numpy as jnp
from jax import lax
from jax.experimental import pallas as pl
from jax.experimental.pallas import tpu as pltpu

F32 = jnp.float32
BF16 = jnp.bfloat16

HEAD_DIM = 64
GROUP_HEADS = 4
GROUP_W = GROUP_HEADS * HEAD_DIM
IDX_HEADS = 8
IDX_DIM = 32
DSA_TOPK = 256
DIL_PATTERNS = ((128, 1), (512, 4), (2048, 16))
MOBA_BLOCK = 256
MOBA_TOPK = 3
ROPE_THETA = 10000.0
LN_EPS = 1e-5
LOG2E = math.log2(math.e)
LANES = 128
BF16_ROWS = 16
TB = 256
VROWS = HEAD_DIM + BF16_ROWS
NEG_INF = float("-inf")
VMEM_LIMIT = 56 * 1024 * 1024

(CB_QA, CB_KA, CB_VA, CB_VB, CB_VC, CB_VD,
 CB_QB, CB_KB, CB_QC, CB_KC, CB_QD, CB_KD,
 CB_QI, CB_KI8) = range(14)
N_COLBLK = 14
PROJ_TN = 2 * GROUP_W
PLAIN_STEPS, ROPE64_STEPS = 3, 3


def _dot_nt(a, b):
    return lax.dot_general(a, b, (((1,), (1,)), ((), ())), preferred_element_type=F32)


def _layer_norm(y, g, b):
    mu = jnp.mean(y, axis=-1, keepdims=True)
    d = y - mu
    var = jnp.mean(d * d, axis=-1, keepdims=True)
    return d * lax.rsqrt(var + LN_EPS) * g + b


def _tree_sum(parts):
    while len(parts) > 1:
        parts = [parts[n] + parts[n + 1] for n in range(0, len(parts), 2)]
    return parts[0]


def _rope(acc, tab_ref, half):
    cos = tab_ref[:, 0:LANES]
    sin = tab_ref[:, LANES:2 * LANES]
    lane = lax.broadcasted_iota(jnp.int32, (1, LANES), 1)
    first = (lane & (2 * half - 1)) < half
    outs = []
    for c in range(acc.shape[1] // LANES):
        a = acc[:, c * LANES:(c + 1) * LANES]
        swapped = jnp.where(first, pltpu.roll(a, LANES - half, 1), pltpu.roll(a, half, 1))
        outs.append(a * cos + swapped * sin)
    return jnp.concatenate(outs, axis=1)


def _inproj_kernel(x_ref, w_ref, wm_ref, t64_ref, t32_ref, o_ref, misc_ref, xb_ref):
    j = pl.program_id(1)

    @pl.when(j == 0)
    def _():
        xb = x_ref[...].astype(BF16)
        xb_ref[...] = xb
        misc_ref[...] = jnp.dot(xb, wm_ref[...], preferred_element_type=F32)

    acc = jnp.dot(xb_ref[...], w_ref[...], preferred_element_type=F32)

    @pl.when(j < PLAIN_STEPS)
    def _():
        o_ref[...] = acc.astype(BF16)

    @pl.when(jnp.logical_and(j >= PLAIN_STEPS, j < PLAIN_STEPS + ROPE64_STEPS))
    def _():
        o_ref[...] = _rope(acc, t64_ref, HEAD_DIM // 2).astype(BF16)

    @pl.when(j == PLAIN_STEPS + ROPE64_STEPS)
    def _():
        o_ref[...] = _rope(acc, t32_ref, IDX_DIM // 2).astype(BF16)


def _inproj(x2d, w_main, w_misc, tab64, tab32, seq):
    m, d = x2d.shape
    tm = min(1024, seq)
    nt = seq // tm
    return pl.pallas_call(
        _inproj_kernel,
        grid=(m // tm, N_COLBLK * GROUP_W // PROJ_TN),
        in_specs=[
            pl.BlockSpec((tm, d), lambda i, j: (i, 0)),
            pl.BlockSpec((d, PROJ_TN), lambda i, j: (0, j)),
            pl.BlockSpec((d, LANES), lambda i, j: (0, 0)),
            pl.BlockSpec((tm, 2 * LANES), lambda i, j: (i % nt, 0)),
            pl.BlockSpec((tm, 2 * LANES), lambda i, j: (i % nt, 0)),
        ],
        out_specs=[
            pl.BlockSpec((tm, PROJ_TN), lambda i, j: (i, j)),
            pl.BlockSpec((tm, LANES), lambda i, j: (i, 0)),
        ],
        out_shape=[
            jax.ShapeDtypeStruct((m, N_COLBLK * GROUP_W), BF16),
            jax.ShapeDtypeStruct((m, LANES), F32),
        ],
        scratch_shapes=[pltpu.VMEM((tm, d), BF16)],
        compiler_params=pltpu.CompilerParams(
            dimension_semantics=("parallel", "arbitrary"), vmem_limit_bytes=VMEM_LIMIT),
    )(x2d, w_main, w_misc, tab64, tab32)


def _gate_kernel(z_ref, b_ref, o_ref, *, chunks):
    z = z_ref[...] + b_ref[...]
    w = jnp.minimum(z, 0.0) - jnp.log1p(jnp.exp(-jnp.abs(z)))
    rows = w.shape[0]
    lane = lax.broadcasted_iota(jnp.int32, (rows, LANES), 1)
    s = 1
    while s < LANES:
        w = w + jnp.where(lane >= s, pltpu.roll(w, s, 1), 0.0)
        s *= 2
    tot = jnp.broadcast_to(w[:, LANES - 1:LANES], (rows, LANES))
    pos = lax.broadcasted_iota(jnp.int32, (rows, LANES), 0) & (chunks - 1)
    run = tot
    s = 1
    while s < chunks:
        run = run + jnp.where(pos >= s, pltpu.roll(run, s, 0), 0.0)
        s *= 2
    bias = -LOG2E * (w + (run - tot))
    hi = bias.astype(BF16).astype(F32)
    mid = (bias - hi).astype(BF16).astype(F32)
    lo = (bias - hi - mid).astype(BF16).astype(F32)
    o_ref[0] = hi
    o_ref[1] = mid
    o_ref[2] = lo


def _gate_bias(z, bias_rows, chunks):
    b, rows, _ = z.shape
    return pl.pallas_call(
        functools.partial(_gate_kernel, chunks=chunks),
        grid=(b,),
        in_specs=[pl.BlockSpec((None, rows, LANES), lambda i: (i, 0, 0)),
                  pl.BlockSpec((rows, 1), lambda i: (0, 0))],
        out_specs=pl.BlockSpec((None, 3, rows, LANES), lambda i: (i, 0, 0, 0)),
        out_shape=jax.ShapeDtypeStruct((b, 3, rows, LANES), F32),
        compiler_params=pltpu.CompilerParams(dimension_semantics=("parallel",)),
    )(z, bias_rows)


HEADS_W = GROUP_HEADS * TB


def _stack_heads(q_ref, qs_ref):
    q = q_ref[...].astype(F32)
    head_of_lane = lax.broadcasted_iota(jnp.int32, (1, GROUP_W), 1) >> 6
    for h in range(GROUP_HEADS):
        qs_ref[h * TB:(h + 1) * TB, 0:GROUP_W] = jnp.where(head_of_lane == h, q, 0.0).astype(BF16)


def _head(x, h):
    return x[:, h * TB:(h + 1) * TB]


def _tile_heads(x):
    return jnp.concatenate([x] * GROUP_HEADS, axis=1)


def _kblock(k_ref, j):
    return k_ref[pl.ds(pl.multiple_of(j * TB, TB), TB), :]


def _pair_scores(k_ref, qs_ref, j, extra_ref=None):
    kblk = _kblock(k_ref, j)
    outs = []
    for pr in range(GROUP_W // LANES):
        rows = slice(2 * pr * TB, 2 * (pr + 1) * TB)
        kp = kblk[:, pr * LANES:(pr + 1) * LANES]
        qp = qs_ref[rows, pr * LANES:(pr + 1) * LANES]
        if extra_ref is not None:
            kp = jnp.concatenate([kp, _kblock(extra_ref, j)], axis=1)
            qp = jnp.concatenate([qp, qs_ref[rows, GROUP_W:GROUP_W + LANES]], axis=1)
        outs.append(_dot_nt(kp, qp))
    return jnp.concatenate(outs, axis=1)


def _causal_mask(heads=GROUP_HEADS):
    krow = lax.broadcasted_iota(jnp.int32, (TB, heads * TB), 0)
    qcol = lax.broadcasted_iota(jnp.int32, (TB, heads * TB), 1) & (TB - 1)
    return krow <= qcol


def _consume(st, blk_max, vt, m, accs, weight=None, keep=None):
    m_new = jnp.maximum(m, blk_max)
    if keep is not None:
        m_new = jnp.where(keep > 0.5, m_new, m)
    m_safe = jnp.where(m_new == NEG_INF, 0.0, m_new)
    alpha = jnp.exp2(m - m_safe)
    p = jnp.exp2(st - m_safe)
    if weight is not None:
        p = p * weight
    pb = p.astype(BF16)
    new_accs = []
    for h in range(GROUP_HEADS):
        pv = jnp.dot(vt[h * VROWS:(h + 1) * VROWS, :], _head(pb, h), preferred_element_type=F32)
        upd = _head(alpha, h) * accs[h] + pv
        new_accs.append(upd if keep is None else jnp.where(_head(keep, h) > 0.5, upd, accs[h]))
    return m_new, new_accs


TRIP_SHIFT = 3
STEPS_PER_TRIP = 1 << TRIP_SHIFT


def _attn_scratch(qs_width=GROUP_W):
    return [
        pltpu.VMEM((HEADS_W, qs_width), BF16),
        pltpu.VMEM((2, TB, HEADS_W), F32),
        pltpu.VMEM((2, 1, HEADS_W), F32),
        pltpu.VMEM((1, HEADS_W), F32),
        pltpu.VMEM((GROUP_HEADS * VROWS, TB), F32),
    ]


def _attend_sequence(scores_of, vt_ref, o_ref, scratch, n_steps, block_of, step_args):
    _, st_ref, bm_ref, m_ref, acc_ref = scratch

    def issue(s, slot, first=False):
        st = scores_of(block_of(s))
        mask_fn = step_args(s, first).get("mask_fn")
        if mask_fn is not None:
            st = mask_fn(st)
        st_ref[slot] = st
        bm_ref[slot] = jnp.max(st, axis=0, keepdims=True)

    def consume(s, slot, first=False):
        args = step_args(s, first)
        accs = [acc_ref[h * VROWS:(h + 1) * VROWS, :] for h in range(GROUP_HEADS)]
        m, accs = _consume(st_ref[slot], bm_ref[slot], vt_ref[block_of(s)], m_ref[...], accs,
                           weight=args.get("weight"), keep=args.get("keep"))
        m_ref[...] = m
        for h in range(GROUP_HEADS):
            acc_ref[h * VROWS:(h + 1) * VROWS, :] = accs[h]

    m_ref[...] = jnp.full((1, HEADS_W), NEG_INF, F32)
    acc_ref[...] = jnp.zeros_like(acc_ref)
    issue(0, 0, True)
    issue(1, 1)
    consume(0, 0, True)

    def run(s, count):
        for u in range(count):
            issue(s + u + 1, u & 1)
            consume(s + u, (u + 1) & 1)

    def trip(tt, carry):
        run(1 + STEPS_PER_TRIP * tt, STEPS_PER_TRIP)
        return carry

    n_trips = n_steps >> TRIP_SHIFT
    lax.fori_loop(0, n_trips, trip, 0)

    for bit in range(TRIP_SHIFT - 1, 0, -1):
        size = 1 << bit

        @pl.when((n_steps & size) != 0)
        def _(size=size):
            run(1 + ((n_steps >> TRIP_SHIFT) << TRIP_SHIFT) + (n_steps & (STEPS_PER_TRIP - 2 * size)), size)

    @pl.when((n_steps & 1) == 1)
    def _():
        consume(n_steps, 1)

    ot = jnp.concatenate(
        [acc_ref[h * VROWS:h * VROWS + HEAD_DIM, :] / acc_ref[h * VROWS + HEAD_DIM:h * VROWS + HEAD_DIM + 1, :]
         for h in range(GROUP_HEADS)], axis=0)
    o_ref[...] = ot.T.astype(o_ref.dtype)


def _diagonal_first(i):
    return lambda s: jnp.where(s == 0, i, s - 1)


def _attn_specs(seq, colblks):
    nq = seq // TB
    specs = []
    for kind, cb in colblks:
        if kind == "q":
            specs.append(pl.BlockSpec((TB, GROUP_W), lambda b, i, cb=cb: (b * nq + i, cb)))
        else:
            specs.append(pl.BlockSpec((seq, GROUP_W), lambda b, i, cb=cb: (b, cb)))
    return specs


def _vt_spec(nkb, cb):
    return pl.BlockSpec((None, nkb, GROUP_HEADS * VROWS, TB), lambda b, i: (b, 0, cb - CB_VA, 0))


def _o_spec(seq):
    nq = seq // TB
    return pl.BlockSpec((TB, GROUP_W), lambda b, i: (b * nq + i, 0))


_ATTN_PARAMS = dict(
    compiler_params=pltpu.CompilerParams(
        dimension_semantics=("parallel", "arbitrary"), vmem_limit_bytes=VMEM_LIMIT))


BIAS_TERMS = 3


def _fox_kernel(q_ref, k_ref, cb_ref, vt_ref, o_ref, *scratch):
    i = pl.program_id(1)
    qs_ref = scratch[0]
    _stack_heads(q_ref, qs_ref)
    lane = lax.broadcasted_iota(jnp.int32, (TB, LANES), 1)
    for h in range(GROUP_HEADS):
        pick = jnp.logical_and(lane >= BIAS_TERMS * h, lane < BIAS_TERMS * (h + 1))
        qs_ref[h * TB:(h + 1) * TB, GROUP_W:GROUP_W + LANES] = jnp.where(pick, 1.0, 0.0).astype(BF16)

    def scores_of(j):
        return _pair_scores(k_ref, qs_ref, j, cb_ref)

    def step_args(s, diag):
        if diag:
            return dict(mask_fn=lambda st: jnp.where(_causal_mask(), st, NEG_INF))
        return {}

    _attend_sequence(scores_of, vt_ref, o_ref, scratch, i, _diagonal_first(i), step_args)


def _fox(qkv, cbias, vt, batch, seq):
    nkb = seq // TB
    return pl.pallas_call(
        _fox_kernel,
        grid=(batch, seq // TB),
        in_specs=_attn_specs(seq, (("q", CB_QA), ("k", CB_KA))) + [
            pl.BlockSpec((seq, LANES), lambda b, i: (b, 0)),
            _vt_spec(nkb, CB_VA),
        ],
        out_specs=_o_spec(seq),
        out_shape=jax.ShapeDtypeStruct((batch * seq, GROUP_W), BF16),
        scratch_shapes=_attn_scratch(GROUP_W + LANES),
        **_ATTN_PARAMS,
    )(qkv, qkv, cbias, vt)


HIGH_HALF = -(1 << 16)


def _key_to_float(key_u):
    key = key_u ^ jnp.int32(-2 ** 31)
    bits = jnp.where(key < 0, key ^ jnp.int32(0x7FFFFFFF), key)
    below_neg_inf = jnp.logical_and(key_u >= 0, key_u < jnp.int32(0x007FFFFF))
    return jnp.where(below_neg_inf, NEG_INF, lax.bitcast_convert_type(bits, F32))


def _dsa_kernel(q_ref, k_ref, vt_ref, qi_ref, k8_ref, w_ref, o_ref, strip_ref, strip_hi_ref, qm_ref, *scratch,
                topk, idx_bits):
    i = pl.program_id(1)
    nblk = i + 1

    qi = qi_ref[...].astype(F32)
    low_lanes = lax.broadcasted_iota(jnp.int32, (1, LANES), 1) < IDX_DIM
    per_half = LANES // IDX_DIM
    for h in range(IDX_HEADS):
        half = qi[:, (h // per_half) * LANES:(h // per_half + 1) * LANES]
        if h % per_half:
            half = pltpu.roll(half, LANES - IDX_DIM * (h % per_half), 1)
        qm_ref[h] = jnp.where(low_lanes, half, 0.0).astype(BF16)

    wrow = w_ref[...]

    def score_block(j, diag):
        kblk = _kblock(k8_ref, j)[:, 0:LANES]
        sc = _tree_sum([wrow[h:h + 1, :] * jnp.maximum(_dot_nt(kblk, qm_ref[h]), 0.0)
                        for h in range(IDX_HEADS)])
        if diag:
            sc = jnp.where(_causal_mask(1), sc, NEG_INF)
        strip_ref[j] = sc
        strip_hi_ref[j] = lax.bitcast_convert_type(
            lax.bitcast_convert_type(sc, jnp.int32) & HIGH_HALF, F32).astype(BF16)

    def score_trip(t, carry):
        for u in range(4):
            score_block(4 * t + u, False)
        return carry

    lax.fori_loop(0, i >> 2, score_trip, 0)

    @pl.when((i & 2) != 0)
    def _():
        score_block(((i >> 2) << 2), False)
        score_block(((i >> 2) << 2) + 1, False)

    @pl.when((i & 1) == 1)
    def _():
        score_block(i - 1, False)

    score_block(i, True)

    def count(pred):
        def body(j, acc):
            hit = jnp.where(pred(j, strip_ref[j]), 1.0, 0.0)
            return acc + _tree_sum([hit[r:r + 8, :] for r in range(0, TB, 8)])
        acc = lax.fori_loop(0, nblk, body, jnp.zeros((8, TB), F32))
        return jnp.sum(acc, axis=0, keepdims=True)

    def count_ge(ref, t, rows):
        one, zero = jnp.ones((), ref.dtype), jnp.zeros((), ref.dtype)

        def hits(j, tt):
            hit = jnp.where(ref[j] >= tt, one, zero)
            return _tree_sum([hit[r:r + rows, :] for r in range(0, TB, rows)]).astype(F32)

        def body(p, acc):
            j1 = 2 * p + 1
            t1 = jnp.where(j1 < nblk, t, jnp.full_like(t, jnp.inf))
            return acc + hits(2 * p, t) + hits(jnp.minimum(j1, nblk - 1), t1)

        acc = lax.fori_loop(0, (nblk + 1) >> 1, body, jnp.zeros((rows, TB), F32))
        return jnp.sum(acc, axis=0, keepdims=True)

    def radix_pass(it, state, high):
        prefix, cnt_at_prefix = state
        cand = prefix | jnp.left_shift(jnp.int32(1), 31 - it)
        t = _key_to_float(cand)
        if high:
            t = lax.bitcast_convert_type(lax.bitcast_convert_type(t, jnp.int32) & HIGH_HALF, F32)
            cnt = count_ge(strip_hi_ref, t.astype(BF16), BF16_ROWS)
        else:
            cnt = count_ge(strip_ref, t, 8)
        take = cnt >= topk
        return jnp.where(take, cand, prefix), jnp.where(take, cnt, cnt_at_prefix)

    n_all = jnp.full((1, TB), 1.0, F32) * (nblk * TB).astype(F32)
    state = (jnp.zeros((1, TB), jnp.int32), n_all)
    state = lax.fori_loop(0, 16, lambda it, s: radix_pass(it, s, True), state)
    prefix, cnt_ge = lax.fori_loop(16, 32, lambda it, s: radix_pass(it, s, False), state)
    thr = _key_to_float(prefix)

    @pl.when(jnp.max(cnt_ge) > topk)
    def _():
        need = topk - count(lambda j, blk: blk > thr)
        krow = lax.broadcasted_iota(jnp.int32, (TB, TB), 0)

        def idx_body(it, lim):
            cand = lim | jnp.left_shift(jnp.int32(1), idx_bits - 1 - it)
            cnt = count(lambda j, blk: jnp.logical_and(blk == thr, krow + j * TB < cand))
            return jnp.where(cnt <= need, cand, lim)

        lim = lax.fori_loop(0, idx_bits, idx_body, jnp.zeros((1, TB), jnp.int32))

        def drop_body(j, carry):
            blk = strip_ref[j]
            drop = jnp.logical_and(blk == thr, krow + j * TB >= lim)
            strip_ref[j] = jnp.where(drop, NEG_INF, blk)
            return carry

        lax.fori_loop(0, nblk, drop_body, 0)

    qs_ref = scratch[0]
    _stack_heads(q_ref, qs_ref)

    block_of = _diagonal_first(i)

    def step_args(s, diag):
        def mask_fn(st):
            blk = strip_ref[block_of(s)]
            sel = blk >= thr
            if diag:
                sel = jnp.logical_and(sel, blk > NEG_INF)
            return jnp.concatenate(
                [jnp.where(sel, _head(st, h), NEG_INF) for h in range(GROUP_HEADS)], axis=1)
        return dict(mask_fn=mask_fn)

    _attend_sequence(lambda j: _pair_scores(k_ref, qs_ref, j), vt_ref, o_ref, scratch,
                     i, block_of, step_args)


def _dsa(qkv, vt, wrow, batch, seq):
    nkb = seq // TB
    topk = min(DSA_TOPK, seq // 4)
    kern = functools.partial(_dsa_kernel, topk=float(topk), idx_bits=int(seq).bit_length())
    return pl.pallas_call(
        kern,
        grid=(batch, seq // TB),
        in_specs=_attn_specs(seq, (("q", CB_QB), ("k", CB_KB))) + [_vt_spec(nkb, CB_VB)]
        + _attn_specs(seq, (("q", CB_QI), ("k", CB_KI8)))
        + [pl.BlockSpec((None, IDX_HEADS, TB), lambda b, i: (b, 0, i))],
        out_specs=_o_spec(seq),
        out_shape=jax.ShapeDtypeStruct((batch * seq, GROUP_W), BF16),
        scratch_shapes=[pltpu.VMEM((nkb, TB, TB), F32), pltpu.VMEM((nkb, TB, TB), BF16),
                        pltpu.VMEM((IDX_HEADS, TB, LANES), BF16)] + _attn_scratch(),
        **_ATTN_PARAMS,
    )(qkv, qkv, vt, qkv, qkv, wrow)


DIL_SPAN = max(w for w, _ in DIL_PATTERNS) // TB + 1


def _dilated_multiplicity():
    k = np.arange(TB)[:, None]
    q = np.arange(TB)[None, :]
    out = np.zeros((DIL_SPAN, TB, TB), np.float32)
    for delta in range(DIL_SPAN):
        d = q - k + TB * delta
        for w, r in DIL_PATTERNS:
            out[delta] += ((d >= 0) & (d % r == 0) & (d <= w)).astype(np.float32)
    return out


def _dil_kernel(q_ref, k_ref, vt_ref, mult_ref, o_ref, *scratch):
    i = pl.program_id(1)
    qs_ref = scratch[0]
    _stack_heads(q_ref, qs_ref)
    n = jnp.minimum(i, DIL_SPAN - 1)

    def step_args(s, first):
        mult = _tile_heads(mult_ref[jnp.minimum(s, DIL_SPAN - 1)])
        return dict(mask_fn=lambda st: jnp.where(mult > 0.0, st, NEG_INF), weight=mult)

    _attend_sequence(lambda j: _pair_scores(k_ref, qs_ref, j), vt_ref, o_ref, scratch,
                     n, lambda s: jnp.maximum(i - s, 0), step_args)


def _dilated(qkv, vt, mult, batch, seq):
    nkb = seq // TB
    return pl.pallas_call(
        _dil_kernel,
        grid=(batch, seq // TB),
        in_specs=_attn_specs(seq, (("q", CB_QC), ("k", CB_KC))) + [
            _vt_spec(nkb, CB_VC),
            pl.BlockSpec((DIL_SPAN, TB, TB), lambda b, i: (0, 0, 0)),
        ],
        out_specs=_o_spec(seq),
        out_shape=jax.ShapeDtypeStruct((batch * seq, GROUP_W), BF16),
        scratch_shapes=_attn_scratch(),
        **_ATTN_PARAMS,
    )(qkv, qkv, vt, mult)


def _moba_kernel(q_ref, k_ref, vt_ref, o_ref, kmean_ref, sel_ref, *scratch, nkb, nkb_pad, topn):
    i = pl.program_id(1)

    @pl.when(i == 0)
    def _():
        kmean_ref[...] = jnp.zeros_like(kmean_ref)
        for n in range(nkb):
            kb = k_ref[n * TB:(n + 1) * TB, :].astype(F32)
            kmean_ref[n:n + 1, :] = jnp.mean(kb, axis=0, keepdims=True)

    qs_ref = scratch[0]
    _stack_heads(q_ref, qs_ref)
    blk_id = lax.broadcasted_iota(jnp.int32, (nkb_pad, HEADS_W), 0)
    blk_f = blk_id.astype(F32)
    g = _dot_nt(kmean_ref[...].astype(BF16), qs_ref[...])
    g = jnp.where(blk_id < i, g, NEG_INF)
    sel = jnp.zeros_like(g)
    for _ in range(topn):
        mx = jnp.max(g, axis=0, keepdims=True)
        is_max = jnp.logical_and(g == mx, mx > NEG_INF)
        first = jnp.min(jnp.where(is_max, blk_f, float(nkb_pad)), axis=0, keepdims=True)
        pick = blk_f == first
        sel = jnp.where(pick, 1.0, sel)
        g = jnp.where(pick, NEG_INF, g)
    sel_ref[...] = sel

    block_of = _diagonal_first(i)

    def step_args(s, diag):
        if diag:
            return dict(mask_fn=lambda st: jnp.where(_causal_mask(), st, NEG_INF))
        return dict(keep=sel_ref[pl.ds(block_of(s), 1), :])

    _attend_sequence(lambda j: _pair_scores(k_ref, qs_ref, j), vt_ref, o_ref, scratch,
                     i, block_of, step_args)


def _moba(qkv, vt, batch, seq):
    nkb = seq // TB
    nkb_pad = -(-nkb // 8) * 8
    kern = functools.partial(_moba_kernel, nkb=nkb, nkb_pad=nkb_pad, topn=min(MOBA_TOPK, nkb))
    return pl.pallas_call(
        kern,
        grid=(batch, seq // TB),
        in_specs=_attn_specs(seq, (("q", CB_QD), ("k", CB_KD))) + [_vt_spec(nkb, CB_VD)],
        out_specs=_o_spec(seq),
        out_shape=jax.ShapeDtypeStruct((batch * seq, GROUP_W), BF16),
        scratch_shapes=[pltpu.VMEM((nkb_pad, GROUP_W), F32),
                        pltpu.VMEM((nkb_pad, HEADS_W), F32)] + _attn_scratch(),
        compiler_params=pltpu.CompilerParams(
            dimension_semantics=("arbitrary", "arbitrary"), vmem_limit_bytes=VMEM_LIMIT),
    )(qkv, qkv, vt)


def _outproj_kernel(oa_ref, ob_ref, oc_ref, od_ref, w_ref, x_ref, g_ref, b_ref, y_ref, *, alpha):
    acc = alpha * x_ref[...]
    for gi, o_ref in enumerate((oa_ref, ob_ref, oc_ref, od_ref)):
        acc = acc + jnp.dot(o_ref[...], w_ref[gi * GROUP_W:(gi + 1) * GROUP_W, :],
                            preferred_element_type=F32)
    y_ref[...] = _layer_norm(acc, g_ref[...], b_ref[...])


def _outproj(os4, w_o, x2d, g, b, alpha):
    m, d = x2d.shape
    tm = min(512, m)
    row = lambda i: (i, 0)
    fixed = lambda i: (0, 0)
    return pl.pallas_call(
        functools.partial(_outproj_kernel, alpha=alpha),
        grid=(m // tm,),
        in_specs=[pl.BlockSpec((tm, GROUP_W), row)] * 4 + [
            pl.BlockSpec((d, d), fixed), pl.BlockSpec((tm, d), row),
            pl.BlockSpec((1, d), fixed), pl.BlockSpec((1, d), fixed)],
        out_specs=pl.BlockSpec((tm, d), row),
        out_shape=jax.ShapeDtypeStruct((m, d), F32),
        compiler_params=pltpu.CompilerParams(
            dimension_semantics=("parallel",), vmem_limit_bytes=VMEM_LIMIT),
    )(*os4, w_o, x2d, g, b)


def _mlp_kernel(x_ref, wu_ref, wd_ref, g_ref, b_ref, y_ref, xb_ref, acc_ref, *, alpha):
    f = pl.program_id(1)

    @pl.when(f == 0)
    def _():
        x = x_ref[...]
        xb_ref[...] = x.astype(BF16)
        acc_ref[...] = alpha * x

    h = jnp.dot(xb_ref[...], wu_ref[...], preferred_element_type=F32)
    h = jnp.square(jnp.maximum(h, 0.0))
    acc_ref[...] += jnp.dot(h.astype(BF16), wd_ref[...], preferred_element_type=F32)

    @pl.when(f == pl.num_programs(1) - 1)
    def _():
        y_ref[...] = _layer_norm(acc_ref[...], g_ref[...], b_ref[...])


def _mlp(x2d, w_up, w_down, g, b, alpha):
    m, d = x2d.shape
    dff = w_up.shape[1]
    tm = min(1024, m)
    tf = min(1024, dff)
    return pl.pallas_call(
        functools.partial(_mlp_kernel, alpha=alpha),
        grid=(m // tm, dff // tf),
        in_specs=[
            pl.BlockSpec((tm, d), lambda i, f: (i, 0)),
            pl.BlockSpec((d, tf), lambda i, f: (0, f)),
            pl.BlockSpec((tf, d), lambda i, f: (f, 0)),
            pl.BlockSpec((1, d), lambda i, f: (0, 0)),
            pl.BlockSpec((1, d), lambda i, f: (0, 0)),
        ],
        out_specs=pl.BlockSpec((tm, d), lambda i, f: (i, 0)),
        out_shape=jax.ShapeDtypeStruct((m, d), F32),
        scratch_shapes=[pltpu.VMEM((tm, d), BF16), pltpu.VMEM((tm, d), F32)],
        compiler_params=pltpu.CompilerParams(
            dimension_semantics=("parallel", "arbitrary"), vmem_limit_bytes=VMEM_LIMIT),
    )(x2d, w_up, w_down, g, b)


def _column_plan():
    hq = GROUP_W
    sizes = (hq, hq, hq, GROUP_HEADS,
             hq, hq, hq, IDX_HEADS * IDX_DIM, IDX_DIM, IDX_HEADS,
             hq, hq, hq, hq, hq, hq)
    offs = np.concatenate([[0], np.cumsum(sizes)])
    (qa, ka, va, fa, qb, kb, vb, qi, ki, wi, qc, kc, vc, qd, kd, vd) = [
        np.arange(offs[n], offs[n + 1]) for n in range(len(sizes))]
    blocks = {CB_QA: qa, CB_KA: ka, CB_VA: va, CB_VB: vb, CB_VC: vc, CB_VD: vd,
              CB_QB: qb, CB_KB: kb, CB_QC: qc, CB_KC: kc, CB_QD: qd, CB_KD: kd,
              CB_QI: qi, CB_KI8: np.tile(ki, IDX_HEADS)}
    scale = np.ones(N_COLBLK * GROUP_W, np.float32)
    for cb in (CB_QA, CB_QB, CB_QC, CB_QD):
        scale[cb * GROUP_W:(cb + 1) * GROUP_W] = LOG2E * HEAD_DIM ** -0.5
    return np.concatenate([blocks[cb] for cb in range(N_COLBLK)]), scale, np.concatenate([fa, wi])


def _rope_table(seq, dim):
    half = dim // 2
    lane = np.arange(LANES)
    inv = ROPE_THETA ** (-(jnp.arange(half, dtype=F32)) / half)
    ang = jnp.arange(seq, dtype=F32)[:, None] * inv[None, :]
    idx = lane % half
    sign = np.where((lane % dim) < half, -1.0, 1.0).astype(np.float32)
    return jnp.concatenate([jnp.cos(ang)[:, idx], jnp.sin(ang)[:, idx] * sign[None, :]], axis=1)


def _vt_all(qkv, batch, seq):
    nkb = seq // TB
    heads = (CB_VD - CB_VA + 1) * GROUP_HEADS
    v = qkv[:, CB_VA * GROUP_W:(CB_VD + 1) * GROUP_W].reshape(batch, nkb, TB, heads, HEAD_DIM)
    pad = jnp.zeros((batch, nkb, TB, heads, VROWS - HEAD_DIM), v.dtype).at[..., 0].set(1)
    v = jnp.concatenate([v, pad], axis=-1).reshape(batch, nkb, TB, heads * VROWS)
    return v.transpose(0, 1, 3, 2)


def kernel(x, w_in, b_f, w_o, ln1_g, ln1_b, w_up, w_down, ln2_g, ln2_b):
    batch, seq, d = x.shape
    depth = w_in.shape[0]
    assert seq % TB == 0 and d == 4 * GROUP_W
    chunks = seq // LANES
    assert chunks & (chunks - 1) == 0
    alpha = (2.0 * depth) ** 0.25

    main_cols, main_scale, misc_cols = _column_plan()
    w_main = (w_in[:, :, main_cols] * main_scale).astype(BF16)
    w_misc = jnp.pad(w_in[:, :, misc_cols], ((0, 0), (0, 0), (0, LANES - len(misc_cols)))).astype(BF16)
    w_o16, w_up16, w_down16 = w_o.astype(BF16), w_up.astype(BF16), w_down.astype(BF16)
    tab64 = _rope_table(seq, HEAD_DIM)
    tab32 = _rope_table(seq, IDX_DIM)
    mult = jnp.asarray(_dilated_multiplicity())

    x2d = x.reshape(batch * seq, d)
    for l in range(depth):
        qkv, misc = _inproj(x2d, w_main[l], w_misc[l], tab64, tab32, seq)
        misc3 = misc.reshape(batch, seq, LANES)
        z = misc3[:, :, 0:GROUP_HEADS].transpose(0, 2, 1).reshape(batch, GROUP_HEADS * chunks, LANES)
        bias_rows = jnp.repeat(b_f[l].astype(F32), chunks)[:, None]
        gb = _gate_bias(z, bias_rows, chunks)
        gb = gb.reshape(batch, BIAS_TERMS, GROUP_HEADS, seq).transpose(0, 3, 2, 1)
        cbias = jnp.pad(gb.reshape(batch * seq, GROUP_HEADS * BIAS_TERMS),
                        ((0, 0), (0, LANES - GROUP_HEADS * BIAS_TERMS))).astype(BF16)
        wrow = misc3[:, :, GROUP_HEADS:GROUP_HEADS + IDX_HEADS].transpose(0, 2, 1)

        vt = _vt_all(qkv, batch, seq)
        oa = _fox(qkv, cbias, vt, batch, seq)
        ob = _dsa(qkv, vt, wrow, batch, seq)
        oc = _dilated(qkv, vt, mult, batch, seq)
        od = _moba(qkv, vt, batch, seq)

        x2d = _outproj((oa, ob, oc, od), w_o16[l], x2d, ln1_g[l][None, :], ln1_b[l][None, :], alpha)
        x2d = _mlp(x2d, w_up16[l], w_down16[l], ln2_g[l][None, :], ln2_b[l][None, :], alpha)
    return x2d.reshape(batch, seq, d)
```

```python
import functools
import math

import numpy as np
import jax
import jax.numpy as jnp
from jax import lax
from jax.experimental import pallas as pl
from jax.experimental.pallas import tpu as pltpu

F32 = jnp.float32
BF16 = jnp.bfloat16

HEAD_DIM = 64
GROUP_HEADS = 4
GROUP_W = GROUP_HEADS * HEAD_DIM
IDX_HEADS = 8
IDX_DIM = 32
DSA_TOPK = 256
DIL_PATTERNS = ((128, 1), (512, 4), (2048, 16))
MOBA_BLOCK = 256
MOBA_TOPK = 3
ROPE_THETA = 10000.0
LN_EPS = 1e-5
LOG2E = math.log2(math.e)
LANES = 128
BF16_ROWS = 16
TB = 256
VROWS = HEAD_DIM + BF16_ROWS
NEG_INF = float("-inf")
VMEM_LIMIT = 56 * 1024 * 1024

(CB_QA, CB_KA, CB_VA, CB_VB, CB_VC, CB_VD,
 CB_QB, CB_KB, CB_QC, CB_KC, CB_QD, CB_KD,
 CB_QI, CB_KI8) = range(14)
N_COLBLK = 14
PROJ_TN = 2 * GROUP_W
PLAIN_STEPS, ROPE64_STEPS = 3, 3


def _dot_nt(a, b):
    return lax.dot_general(a, b, (((1,), (1,)), ((), ())), preferred_element_type=F32)


def _layer_norm(y, g, b):
    mu = jnp.mean(y, axis=-1, keepdims=True)
    d = y - mu
    var = jnp.mean(d * d, axis=-1, keepdims=True)
    return d * lax.rsqrt(var + LN_EPS) * g + b


def _tree_sum(parts):
    while len(parts) > 1:
        parts = [parts[n] + parts[n + 1] for n in range(0, len(parts), 2)]
    return parts[0]


def _rope(acc, tab_ref, half):
    cos = tab_ref[:, 0:LANES]
    sin = tab_ref[:, LANES:2 * LANES]
    lane = lax.broadcasted_iota(jnp.int32, (1, LANES), 1)
    first = (lane & (2 * half - 1)) < half
    outs = []
    for c in range(acc.shape[1] // LANES):
        a = acc[:, c * LANES:(c + 1) * LANES]
        swapped = jnp.where(first, pltpu.roll(a, LANES - half, 1), pltpu.roll(a, half, 1))
        outs.append(a * cos + swapped * sin)
    return jnp.concatenate(outs, axis=1)


def _inproj_kernel(x_ref, w_ref, wm_ref, t64_ref, t32_ref, o_ref, misc_ref, xb_ref):
    j = pl.program_id(1)

    @pl.when(j == 0)
    def _():
        xb = x_ref[...].astype(BF16)
        xb_ref[...] = xb
        misc_ref[...] = jnp.dot(xb, wm_ref[...], preferred_element_type=F32)

    acc = jnp.dot(xb_ref[...], w_ref[...], preferred_element_type=F32)

    @pl.when(j < PLAIN_STEPS)
    def _():
        o_ref[...] = acc.astype(BF16)

    @pl.when(jnp.logical_and(j >= PLAIN_STEPS, j < PLAIN_STEPS + ROPE64_STEPS))
    def _():
        o_ref[...] = _rope(acc, t64_ref, HEAD_DIM // 2).astype(BF16)

    @pl.when(j == PLAIN_STEPS + ROPE64_STEPS)
    def _():
        o_ref[...] = _rope(acc, t32_ref, IDX_DIM // 2).astype(BF16)


def _inproj(x2d, w_main, w_misc, tab64, tab32, seq):
    m, d = x2d.shape
    tm = min(1024, seq)
    nt = seq // tm
    return pl.pallas_call(
        _inproj_kernel,
        grid=(m // tm, N_COLBLK * GROUP_W // PROJ_TN),
        in_specs=[
            pl.BlockSpec((tm, d), lambda i, j: (i, 0)),
            pl.BlockSpec((d, PROJ_TN), lambda i, j: (0, j)),
            pl.BlockSpec((d, LANES), lambda i, j: (0, 0)),
            pl.BlockSpec((tm, 2 * LANES), lambda i, j: (i % nt, 0)),
            pl.BlockSpec((tm, 2 * LANES), lambda i, j: (i % nt, 0)),
        ],
        out_specs=[
            pl.BlockSpec((tm, PROJ_TN), lambda i, j: (i, j)),
            pl.BlockSpec((tm, LANES), lambda i, j: (i, 0)),
        ],
        out_shape=[
            jax.ShapeDtypeStruct((m, N_COLBLK * GROUP_W), BF16),
            jax.ShapeDtypeStruct((m, LANES), F32),
        ],
        scratch_shapes=[pltpu.VMEM((tm, d), BF16)],
        compiler_params=pltpu.CompilerParams(
            dimension_semantics=("parallel", "arbitrary"), vmem_limit_bytes=VMEM_LIMIT),
    )(x2d, w_main, w_misc, tab64, tab32)


def _gate_kernel(z_ref, b_ref, o_ref, *, chunks):
    z = z_ref[...] + b_ref[...]
    w = jnp.minimum(z, 0.0) - jnp.log1p(jnp.exp(-jnp.abs(z)))
    rows = w.shape[0]
    lane = lax.broadcasted_iota(jnp.int32, (rows, LANES), 1)
    s = 1
    while s < LANES:
        w = w + jnp.where(lane >= s, pltpu.roll(w, s, 1), 0.0)
        s *= 2
    tot = jnp.broadcast_to(w[:, LANES - 1:LANES], (rows, LANES))
    pos = lax.broadcasted_iota(jnp.int32, (rows, LANES), 0) & (chunks - 1)
    run = tot
    s = 1
    while s < chunks:
        run = run + jnp.where(pos >= s, pltpu.roll(run, s, 0), 0.0)
        s *= 2
    bias = -LOG2E * (w + (run - tot))
    hi = bias.astype(BF16).astype(F32)
    mid = (bias - hi).astype(BF16).astype(F32)
    lo = (bias - hi - mid).astype(BF16).astype(F32)
    o_ref[0] = hi
    o_ref[1] = mid
    o_ref[2] = lo


def _gate_bias(z, bias_rows, chunks):
    b, rows, _ = z.shape
    return pl.pallas_call(
        functools.partial(_gate_kernel, chunks=chunks),
        grid=(b,),
        in_specs=[pl.BlockSpec((None, rows, LANES), lambda i: (i, 0, 0)),
                  pl.BlockSpec((rows, 1), lambda i: (0, 0))],
        out_specs=pl.BlockSpec((None, 3, rows, LANES), lambda i: (i, 0, 0, 0)),
        out_shape=jax.ShapeDtypeStruct((b, 3, rows, LANES), F32),
        compiler_params=pltpu.CompilerParams(dimension_semantics=("parallel",)),
    )(z, bias_rows)


HEADS_W = GROUP_HEADS * TB


def _stack_heads(q_ref, qs_ref):
    q = q_ref[...].astype(F32)
    head_of_lane = lax.broadcasted_iota(jnp.int32, (1, GROUP_W), 1) >> 6
    for h in range(GROUP_HEADS):
        qs_ref[h * TB:(h + 1) * TB, 0:GROUP_W] = jnp.where(head_of_lane == h, q, 0.0).astype(BF16)


def _head(x, h):
    return x[:, h * TB:(h + 1) * TB]


def _tile_heads(x):
    return jnp.concatenate([x] * GROUP_HEADS, axis=1)


def _kblock(k_ref, j):
    return k_ref[pl.ds(pl.multiple_of(j * TB, TB), TB), :]


def _pair_scores(k_ref, qs_ref, j, extra_ref=None):
    kblk = _kblock(k_ref, j)
    outs = []
    for pr in range(GROUP_W // LANES):
        rows = slice(2 * pr * TB, 2 * (pr + 1) * TB)
        kp = kblk[:, pr * LANES:(pr + 1) * LANES]
        qp = qs_ref[rows, pr * LANES:(pr + 1) * LANES]
        if extra_ref is not None:
            kp = jnp.concatenate([kp, _kblock(extra_ref, j)], axis=1)
            qp = jnp.concatenate([qp, qs_ref[rows, GROUP_W:GROUP_W + LANES]], axis=1)
        outs.append(_dot_nt(kp, qp))
    return jnp.concatenate(outs, axis=1)


def _causal_mask(heads=GROUP_HEADS):
    krow = lax.broadcasted_iota(jnp.int32, (TB, heads * TB), 0)
    qcol = lax.broadcasted_iota(jnp.int32, (TB, heads * TB), 1) & (TB - 1)
    return krow <= qcol


def _consume(st, blk_max, vt, m, accs, weight=None, keep=None):
    m_new = jnp.maximum(m, blk_max)
    if keep is not None:
        m_new = jnp.where(keep > 0.5, m_new, m)
    m_safe = jnp.where(m_new == NEG_INF, 0.0, m_new)
    alpha = jnp.exp2(m - m_safe)
    p = jnp.exp2(st - m_safe)
    if weight is not None:
        p = p * weight
    pb = p.astype(BF16)
    new_accs = []
    for h in range(GROUP_HEADS):
        pv = jnp.dot(vt[h * VROWS:(h + 1) * VROWS, :], _head(pb, h), preferred_element_type=F32)
        upd = _head(alpha, h) * accs[h] + pv
        new_accs.append(upd if keep is None else jnp.where(_head(keep, h) > 0.5, upd, accs[h]))
    return m_new, new_accs


TRIP_SHIFT = 3
STEPS_PER_TRIP = 1 << TRIP_SHIFT


def _attn_scratch(qs_width=GROUP_W):
    return [
        pltpu.VMEM((HEADS_W, qs_width), BF16),
        pltpu.VMEM((2, TB, HEADS_W), F32),
        pltpu.VMEM((2, 1, HEADS_W), F32),
        pltpu.VMEM((1, HEADS_W), F32),
        pltpu.VMEM((GROUP_HEADS * VROWS, TB), F32),
    ]


def _attend_sequence(scores_of, vt_ref, o_ref, scratch, n_steps, block_of, step_args):
    _, st_ref, bm_ref, m_ref, acc_ref = scratch

    def issue(s, slot, first=False):
        st = scores_of(block_of(s))
        mask_fn = step_args(s, first).get("mask_fn")
        if mask_fn is not None:
            st = mask_fn(st)
        st_ref[slot] = st
        bm_ref[slot] = jnp.max(st, axis=0, keepdims=True)

    def consume(s, slot, first=False):
        args = step_args(s, first)
        accs = [acc_ref[h * VROWS:(h + 1) * VROWS, :] for h in range(GROUP_HEADS)]
        m, accs = _consume(st_ref[slot], bm_ref[slot], vt_ref[block_of(s)], m_ref[...], accs,
                           weight=args.get("weight"), keep=args.get("keep"))
        m_ref[...] = m
        for h in range(GROUP_HEADS):
            acc_ref[h * VROWS:(h + 1) * VROWS, :] = accs[h]

    m_ref[...] = jnp.full((1, HEADS_W), NEG_INF, F32)
    acc_ref[...] = jnp.zeros_like(acc_ref)
    issue(0, 0, True)
    issue(1, 1)
    consume(0, 0, True)

    def run(s, count):
        for u in range(count):
            issue(s + u + 1, u & 1)
            consume(s + u, (u + 1) & 1)

    def trip(tt, carry):
        run(1 + STEPS_PER_TRIP * tt, STEPS_PER_TRIP)
        return carry

    n_trips = n_steps >> TRIP_SHIFT
    lax.fori_loop(0, n_trips, trip, 0)

    for bit in range(TRIP_SHIFT - 1, 0, -1):
        size = 1 << bit

        @pl.when((n_steps & size) != 0)
        def _(size=size):
            run(1 + ((n_steps >> TRIP_SHIFT) << TRIP_SHIFT) + (n_steps & (STEPS_PER_TRIP - 2 * size)), size)

    @pl.when((n_steps & 1) == 1)
    def _():
        consume(n_steps, 1)

    ot = jnp.concatenate(
        [acc_ref[h * VROWS:h * VROWS + HEAD_DIM, :] / acc_ref[h * VROWS + HEAD_DIM:h * VROWS + HEAD_DIM + 1, :]
         for h in range(GROUP_HEADS)], axis=0)
    o_ref[...] = ot.T.astype(o_ref.dtype)


def _diagonal_first(i):
    return lambda s: jnp.where(s == 0, i, s - 1)


def _attn_specs(seq, colblks):
    nq = seq // TB
    specs = []
    for kind, cb in colblks:
        if kind == "q":
            specs.append(pl.BlockSpec((TB, GROUP_W), lambda b, i, cb=cb: (b * nq + i, cb)))
        else:
            specs.append(pl.BlockSpec((seq, GROUP_W), lambda b, i, cb=cb: (b, cb)))
    return specs


def _vt_spec(nkb, cb):
    return pl.BlockSpec((None, nkb, GROUP_HEADS * VROWS, TB), lambda b, i: (b, 0, cb - CB_VA, 0))


def _o_spec(seq):
    nq = seq // TB
    return pl.BlockSpec((TB, GROUP_W), lambda b, i: (b * nq + i, 0))


_ATTN_PARAMS = dict(
    compiler_params=pltpu.CompilerParams(
        dimension_semantics=("parallel", "arbitrary"), vmem_limit_bytes=VMEM_LIMIT))


BIAS_TERMS = 3


def _fox_kernel(q_ref, k_ref, cb_ref, vt_ref, o_ref, *scratch):
    i = pl.program_id(1)
    qs_ref = scratch[0]
    _stack_heads(q_ref, qs_ref)
    lane = lax.broadcasted_iota(jnp.int32, (TB, LANES), 1)
    for h in range(GROUP_HEADS):
        pick = jnp.logical_and(lane >= BIAS_TERMS * h, lane < BIAS_TERMS * (h + 1))
        qs_ref[h * TB:(h + 1) * TB, GROUP_W:GROUP_W + LANES] = jnp.where(pick, 1.0, 0.0).astype(BF16)

    def scores_of(j):
        return _pair_scores(k_ref, qs_ref, j, cb_ref)

    def step_args(s, diag):
        if diag:
            return dict(mask_fn=lambda st: jnp.where(_causal_mask(), st, NEG_INF))
        return {}

    _attend_sequence(scores_of, vt_ref, o_ref, scratch, i, _diagonal_first(i), step_args)


def _fox(qkv, cbias, vt, batch, seq):
    nkb = seq // TB
    return pl.pallas_call(
        _fox_kernel,
        grid=(batch, seq // TB),
        in_specs=_attn_specs(seq, (("q", CB_QA), ("k", CB_KA))) + [
            pl.BlockSpec((seq, LANES), lambda b, i: (b, 0)),
            _vt_spec(nkb, CB_VA),
        ],
        out_specs=_o_spec(seq),
        out_shape=jax.ShapeDtypeStruct((batch * seq, GROUP_W), BF16),
        scratch_shapes=_attn_scratch(GROUP_W + LANES),
        **_ATTN_PARAMS,
    )(qkv, qkv, cbias, vt)


HIGH_HALF = -(1 << 16)


def _key_to_float(key_u):
    key = key_u ^ jnp.int32(-2 ** 31)
    bits = jnp.where(key < 0, key ^ jnp.int32(0x7FFFFFFF), key)
    below_neg_inf = jnp.logical_and(key_u >= 0, key_u < jnp.int32(0x007FFFFF))
    return jnp.where(below_neg_inf, NEG_INF, lax.bitcast_convert_type(bits, F32))


def _dsa_kernel(q_ref, k_ref, vt_ref, qi_ref, k8_ref, w_ref, o_ref, strip_ref, strip_hi_ref, qm_ref, cnt_ref, *scratch,
                topk, idx_bits):
    i = pl.program_id(1)
    nblk = i + 1

    qi = qi_ref[...].astype(F32)
    low_lanes = lax.broadcasted_iota(jnp.int32, (1, LANES), 1) < IDX_DIM
    per_half = LANES // IDX_DIM
    for h in range(IDX_HEADS):
        half = qi[:, (h // per_half) * LANES:(h // per_half + 1) * LANES]
        if h % per_half:
            half = pltpu.roll(half, LANES - IDX_DIM * (h % per_half), 1)
        qm_ref[h] = jnp.where(low_lanes, half, 0.0).astype(BF16)

    wrow = w_ref[...]

    def score_block(j, diag):
        kblk = _kblock(k8_ref, j)[:, 0:LANES]
        sc = _tree_sum([wrow[h:h + 1, :] * jnp.maximum(_dot_nt(kblk, qm_ref[h]), 0.0)
                        for h in range(IDX_HEADS)])
        if diag:
            sc = jnp.where(_causal_mask(1), sc, NEG_INF)
        strip_ref[j] = sc
        strip_hi_ref[j] = lax.bitcast_convert_type(
            lax.bitcast_convert_type(sc, jnp.int32) & HIGH_HALF, F32).astype(BF16)

    def score_trip(t, carry):
        for u in range(4):
            score_block(4 * t + u, False)
        return carry

    lax.fori_loop(0, i >> 2, score_trip, 0)

    @pl.when((i & 2) != 0)
    def _():
        score_block(((i >> 2) << 2), False)
        score_block(((i >> 2) << 2) + 1, False)

    @pl.when((i & 1) == 1)
    def _():
        score_block(i - 1, False)

    score_block(i, True)

    def count(pred):
        def body(j, acc):
            hit = jnp.where(pred(j, strip_ref[j]), 1.0, 0.0)
            return acc + _tree_sum([hit[r:r + 8, :] for r in range(0, TB, 8)])
        acc = lax.fori_loop(0, nblk, body, jnp.zeros((8, TB), F32))
        return jnp.sum(acc, axis=0, keepdims=True)

    def count_ge(ref, t, rows):
        one, zero = jnp.ones((), ref.dtype), jnp.zeros((), ref.dtype)

        def hits(j):
            hit = jnp.where(ref[j] >= t, one, zero)
            return _tree_sum([hit[r:r + rows, :] for r in range(0, TB, rows)]).astype(F32)

        def add(j0, n):
            cnt_ref[0:rows, :] += _tree_sum([hits(j0 + u) for u in range(n)])

        def trip(p, carry):
            add(4 * p, 4)
            return carry

        cnt_ref[0:rows, :] = jnp.zeros((rows, TB), F32)
        lax.fori_loop(0, nblk >> 2, trip, 0)

        @pl.when((nblk & 2) != 0)
        def _():
            add((nblk >> 2) << 2, 2)

        @pl.when((nblk & 1) == 1)
        def _():
            add(nblk - 1, 1)

        return jnp.sum(cnt_ref[0:rows, :], axis=0, keepdims=True)

    def radix_pass(it, state, high):
        prefix, cnt_at_prefix = state
        cand = prefix | jnp.left_shift(jnp.int32(1), 31 - it)
        t = _key_to_float(cand)
        if high:
            t = lax.bitcast_convert_type(lax.bitcast_convert_type(t, jnp.int32) & HIGH_HALF, F32)
            cnt = count_ge(strip_hi_ref, t.astype(BF16), BF16_ROWS)
        else:
            cnt = count_ge(strip_ref, t, 8)
        take = cnt >= topk
        return jnp.where(take, cand, prefix), jnp.where(take, cnt, cnt_at_prefix)

    n_all = jnp.full((1, TB), 1.0, F32) * (nblk * TB).astype(F32)
    state = (jnp.zeros((1, TB), jnp.int32), n_all)
    state = lax.fori_loop(0, 16, lambda it, s: radix_pass(it, s, True), state)
    prefix, cnt_ge = lax.fori_loop(16, 32, lambda it, s: radix_pass(it, s, False), state)
    thr = _key_to_float(prefix)

    @pl.when(jnp.max(cnt_ge) > topk)
    def _():
        need = topk - count(lambda j, blk: blk > thr)
        krow = lax.broadcasted_iota(jnp.int32, (TB, TB), 0)

        def idx_body(it, lim):
            cand = lim | jnp.left_shift(jnp.int32(1), idx_bits - 1 - it)
            cnt = count(lambda j, blk: jnp.logical_and(blk == thr, krow + j * TB < cand))
            return jnp.where(cnt <= need, cand, lim)

        lim = lax.fori_loop(0, idx_bits, idx_body, jnp.zeros((1, TB), jnp.int32))

        def drop_body(j, carry):
            blk = strip_ref[j]
            drop = jnp.logical_and(blk == thr, krow + j * TB >= lim)
            strip_ref[j] = jnp.where(drop, NEG_INF, blk)
            return carry

        lax.fori_loop(0, nblk, drop_body, 0)

    qs_ref = scratch[0]
    _stack_heads(q_ref, qs_ref)

    block_of = _diagonal_first(i)

    def step_args(s, diag):
        def mask_fn(st):
            blk = strip_ref[block_of(s)]
            sel = blk >= thr
            if diag:
                sel = jnp.logical_and(sel, blk > NEG_INF)
            return jnp.concatenate(
                [jnp.where(sel, _head(st, h), NEG_INF) for h in range(GROUP_HEADS)], axis=1)
        return dict(mask_fn=mask_fn)

    _attend_sequence(lambda j: _pair_scores(k_ref, qs_ref, j), vt_ref, o_ref, scratch,
                     i, block_of, step_args)


def _dsa(qkv, vt, wrow, batch, seq):
    nkb = seq // TB
    topk = min(DSA_TOPK, seq // 4)
    kern = functools.partial(_dsa_kernel, topk=float(topk), idx_bits=int(seq).bit_length())
    return pl.pallas_call(
        kern,
        grid=(batch, seq // TB),
        in_specs=_attn_specs(seq, (("q", CB_QB), ("k", CB_KB))) + [_vt_spec(nkb, CB_VB)]
        + _attn_specs(seq, (("q", CB_QI), ("k", CB_KI8)))
        + [pl.BlockSpec((None, IDX_HEADS, TB), lambda b, i: (b, 0, i))],
        out_specs=_o_spec(seq),
        out_shape=jax.ShapeDtypeStruct((batch * seq, GROUP_W), BF16),
        scratch_shapes=[pltpu.VMEM((nkb, TB, TB), F32), pltpu.VMEM((nkb, TB, TB), BF16),
                        pltpu.VMEM((IDX_HEADS, TB, LANES), BF16),
                        pltpu.VMEM((BF16_ROWS, TB), F32)] + _attn_scratch(),
        **_ATTN_PARAMS,
    )(qkv, qkv, vt, qkv, qkv, wrow)


DIL_SPAN = max(w for w, _ in DIL_PATTERNS) // TB + 1


def _dilated_multiplicity():
    k = np.arange(TB)[:, None]
    q = np.arange(TB)[None, :]
    out = np.zeros((DIL_SPAN, TB, TB), np.float32)
    for delta in range(DIL_SPAN):
        d = q - k + TB * delta
        for w, r in DIL_PATTERNS:
            out[delta] += ((d >= 0) & (d % r == 0) & (d <= w)).astype(np.float32)
    return out


def _dil_kernel(q_ref, k_ref, vt_ref, mult_ref, o_ref, *scratch):
    i = pl.program_id(1)
    qs_ref = scratch[0]
    _stack_heads(q_ref, qs_ref)
    n = jnp.minimum(i, DIL_SPAN - 1)

    def step_args(s, first):
        mult = _tile_heads(mult_ref[jnp.minimum(s, DIL_SPAN - 1)])
        return dict(mask_fn=lambda st: jnp.where(mult > 0.0, st, NEG_INF), weight=mult)

    _attend_sequence(lambda j: _pair_scores(k_ref, qs_ref, j), vt_ref, o_ref, scratch,
                     n, lambda s: jnp.maximum(i - s, 0), step_args)


def _dilated(qkv, vt, mult, batch, seq):
    nkb = seq // TB
    return pl.pallas_call(
        _dil_kernel,
        grid=(batch, seq // TB),
        in_specs=_attn_specs(seq, (("q", CB_QC), ("k", CB_KC))) + [
            _vt_spec(nkb, CB_VC),
            pl.BlockSpec((DIL_SPAN, TB, TB), lambda b, i: (0, 0, 0)),
        ],
        out_specs=_o_spec(seq),
        out_shape=jax.ShapeDtypeStruct((batch * seq, GROUP_W), BF16),
        scratch_shapes=_attn_scratch(),
        **_ATTN_PARAMS,
    )(qkv, qkv, vt, mult)


def _moba_kernel(q_ref, k_ref, vt_ref, o_ref, kmean_ref, sel_ref, *scratch, nkb, nkb_pad, topn):
    i = pl.program_id(1)

    @pl.when(i == 0)
    def _():
        kmean_ref[...] = jnp.zeros_like(kmean_ref)
        for n in range(nkb):
            kb = k_ref[n * TB:(n + 1) * TB, :].astype(F32)
            kmean_ref[n:n + 1, :] = jnp.mean(kb, axis=0, keepdims=True)

    qs_ref = scratch[0]
    _stack_heads(q_ref, qs_ref)
    blk_id = lax.broadcasted_iota(jnp.int32, (nkb_pad, HEADS_W), 0)
    blk_f = blk_id.astype(F32)
    g = _dot_nt(kmean_ref[...].astype(BF16), qs_ref[...])
    g = jnp.where(blk_id < i, g, NEG_INF)
    sel = jnp.zeros_like(g)
    for _ in range(topn):
        mx = jnp.max(g, axis=0, keepdims=True)
        is_max = jnp.logical_and(g == mx, mx > NEG_INF)
        first = jnp.min(jnp.where(is_max, blk_f, float(nkb_pad)), axis=0, keepdims=True)
        pick = blk_f == first
        sel = jnp.where(pick, 1.0, sel)
        g = jnp.where(pick, NEG_INF, g)
    sel_ref[...] = sel

    block_of = _diagonal_first(i)

    def step_args(s, diag):
        if diag:
            return dict(mask_fn=lambda st: jnp.where(_causal_mask(), st, NEG_INF))
        return dict(keep=sel_ref[pl.ds(block_of(s), 1), :])

    _attend_sequence(lambda j: _pair_scores(k_ref, qs_ref, j), vt_ref, o_ref, scratch,
                     i, block_of, step_args)


def _moba(qkv, vt, batch, seq):
    nkb = seq // TB
    nkb_pad = -(-nkb // 8) * 8
    kern = functools.partial(_moba_kernel, nkb=nkb, nkb_pad=nkb_pad, topn=min(MOBA_TOPK, nkb))
    return pl.pallas_call(
        kern,
        grid=(batch, seq // TB),
        in_specs=_attn_specs(seq, (("q", CB_QD), ("k", CB_KD))) + [_vt_spec(nkb, CB_VD)],
        out_specs=_o_spec(seq),
        out_shape=jax.ShapeDtypeStruct((batch * seq, GROUP_W), BF16),
        scratch_shapes=[pltpu.VMEM((nkb_pad, GROUP_W), F32),
                        pltpu.VMEM((nkb_pad, HEADS_W), F32)] + _attn_scratch(),
        compiler_params=pltpu.CompilerParams(
            dimension_semantics=("arbitrary", "arbitrary"), vmem_limit_bytes=VMEM_LIMIT),
    )(qkv, qkv, vt)


def _outproj_kernel(oa_ref, ob_ref, oc_ref, od_ref, w_ref, x_ref, g_ref, b_ref, y_ref, *, alpha):
    acc = alpha * x_ref[...]
    for gi, o_ref in enumerate((oa_ref, ob_ref, oc_ref, od_ref)):
        acc = acc + jnp.dot(o_ref[...], w_ref[gi * GROUP_W:(gi + 1) * GROUP_W, :],
                            preferred_element_type=F32)
    y_ref[...] = _layer_norm(acc, g_ref[...], b_ref[...])


def _outproj(os4, w_o, x2d, g, b, alpha):
    m, d = x2d.shape
    tm = min(512, m)
    row = lambda i: (i, 0)
    fixed = lambda i: (0, 0)
    return pl.pallas_call(
        functools.partial(_outproj_kernel, alpha=alpha),
        grid=(m // tm,),
        in_specs=[pl.BlockSpec((tm, GROUP_W), row)] * 4 + [
            pl.BlockSpec((d, d), fixed), pl.BlockSpec((tm, d), row),
            pl.BlockSpec((1, d), fixed), pl.BlockSpec((1, d), fixed)],
        out_specs=pl.BlockSpec((tm, d), row),
        out_shape=jax.ShapeDtypeStruct((m, d), F32),
        compiler_params=pltpu.CompilerParams(
            dimension_semantics=("parallel",), vmem_limit_bytes=VMEM_LIMIT),
    )(*os4, w_o, x2d, g, b)


def _mlp_kernel(x_ref, wu_ref, wd_ref, g_ref, b_ref, y_ref, xb_ref, acc_ref, *, alpha):
    f = pl.program_id(1)

    @pl.when(f == 0)
    def _():
        x = x_ref[...]
        xb_ref[...] = x.astype(BF16)
        acc_ref[...] = alpha * x

    h = jnp.dot(xb_ref[...], wu_ref[...], preferred_element_type=F32)
    h = jnp.square(jnp.maximum(h, 0.0))
    acc_ref[...] += jnp.dot(h.astype(BF16), wd_ref[...], preferred_element_type=F32)

    @pl.when(f == pl.num_programs(1) - 1)
    def _():
        y_ref[...] = _layer_norm(acc_ref[...], g_ref[...], b_ref[...])


def _mlp(x2d, w_up, w_down, g, b, alpha):
    m, d = x2d.shape
    dff = w_up.shape[1]
    tm = min(1024, m)
    tf = min(1024, dff)
    return pl.pallas_call(
        functools.partial(_mlp_kernel, alpha=alpha),
        grid=(m // tm, dff // tf),
        in_specs=[
            pl.BlockSpec((tm, d), lambda i, f: (i, 0)),
            pl.BlockSpec((d, tf), lambda i, f: (0, f)),
            pl.BlockSpec((tf, d), lambda i, f: (f, 0)),
            pl.BlockSpec((1, d), lambda i, f: (0, 0)),
            pl.BlockSpec((1, d), lambda i, f: (0, 0)),
        ],
        out_specs=pl.BlockSpec((tm, d), lambda i, f: (i, 0)),
        out_shape=jax.ShapeDtypeStruct((m, d), F32),
        scratch_shapes=[pltpu.VMEM((tm, d), BF16), pltpu.VMEM((tm, d), F32)],
        compiler_params=pltpu.CompilerParams(
            dimension_semantics=("parallel", "arbitrary"), vmem_limit_bytes=VMEM_LIMIT),
    )(x2d, w_up, w_down, g, b)


def _column_plan():
    hq = GROUP_W
    sizes = (hq, hq, hq, GROUP_HEADS,
             hq, hq, hq, IDX_HEADS * IDX_DIM, IDX_DIM, IDX_HEADS,
             hq, hq, hq, hq, hq, hq)
    offs = np.concatenate([[0], np.cumsum(sizes)])
    (qa, ka, va, fa, qb, kb, vb, qi, ki, wi, qc, kc, vc, qd, kd, vd) = [
        np.arange(offs[n], offs[n + 1]) for n in range(len(sizes))]
    blocks = {CB_QA: qa, CB_KA: ka, CB_VA: va, CB_VB: vb, CB_VC: vc, CB_VD: vd,
              CB_QB: qb, CB_KB: kb, CB_QC: qc, CB_KC: kc, CB_QD: qd, CB_KD: kd,
              CB_QI: qi, CB_KI8: np.tile(ki, IDX_HEADS)}
    scale = np.ones(N_COLBLK * GROUP_W, np.float32)
    for cb in (CB_QA, CB_QB, CB_QC, CB_QD):
        scale[cb * GROUP_W:(cb + 1) * GROUP_W] = LOG2E * HEAD_DIM ** -0.5
    return np.concatenate([blocks[cb] for cb in range(N_COLBLK)]), scale, np.concatenate([fa, wi])


def _rope_table(seq, dim):
    half = dim // 2
    lane = np.arange(LANES)
    inv = ROPE_THETA ** (-(jnp.arange(half, dtype=F32)) / half)
    ang = jnp.arange(seq, dtype=F32)[:, None] * inv[None, :]
    idx = lane % half
    sign = np.where((lane % dim) < half, -1.0, 1.0).astype(np.float32)
    return jnp.concatenate([jnp.cos(ang)[:, idx], jnp.sin(ang)[:, idx] * sign[None, :]], axis=1)


def _vt_all(qkv, batch, seq):
    nkb = seq // TB
    heads = (CB_VD - CB_VA + 1) * GROUP_HEADS
    v = qkv[:, CB_VA * GROUP_W:(CB_VD + 1) * GROUP_W].reshape(batch, nkb, TB, heads, HEAD_DIM)
    pad = jnp.zeros((batch, nkb, TB, heads, VROWS - HEAD_DIM), v.dtype).at[..., 0].set(1)
    v = jnp.concatenate([v, pad], axis=-1).reshape(batch, nkb, TB, heads * VROWS)
    return v.transpose(0, 1, 3, 2)


def kernel(x, w_in, b_f, w_o, ln1_g, ln1_b, w_up, w_down, ln2_g, ln2_b):
    batch, seq, d = x.shape
    depth = w_in.shape[0]
    assert seq % TB == 0 and d == 4 * GROUP_W
    chunks = seq // LANES
    assert chunks & (chunks - 1) == 0
    alpha = (2.0 * depth) ** 0.25

    main_cols, main_scale, misc_cols = _column_plan()
    w_main = (w_in[:, :, main_cols] * main_scale).astype(BF16)
    w_misc = jnp.pad(w_in[:, :, misc_cols], ((0, 0), (0, 0), (0, LANES - len(misc_cols)))).astype(BF16)
    w_o16, w_up16, w_down16 = w_o.astype(BF16), w_up.astype(BF16), w_down.astype(BF16)
    tab64 = _rope_table(seq, HEAD_DIM)
    tab32 = _rope_table(seq, IDX_DIM)
    mult = jnp.asarray(_dilated_multiplicity())

    x2d = x.reshape(batch * seq, d)
    for l in range(depth):
        qkv, misc = _inproj(x2d, w_main[l], w_misc[l], tab64, tab32, seq)
        misc3 = misc.reshape(batch, seq, LANES)
        z = misc3[:, :, 0:GROUP_HEADS].transpose(0, 2, 1).reshape(batch, GROUP_HEADS * chunks, LANES)
        bias_rows = jnp.repeat(b_f[l].astype(F32), chunks)[:, None]
        gb = _gate_bias(z, bias_rows, chunks)
        gb = gb.reshape(batch, BIAS_TERMS, GROUP_HEADS, seq).transpose(0, 3, 2, 1)
        cbias = jnp.pad(gb.reshape(batch * seq, GROUP_HEADS * BIAS_TERMS),
                        ((0, 0), (0, LANES - GROUP_HEADS * BIAS_TERMS))).astype(BF16)
        wrow = misc3[:, :, GROUP_HEADS:GROUP_HEADS + IDX_HEADS].transpose(0, 2, 1)

        vt = _vt_all(qkv, batch, seq)
        oa = _fox(qkv, cbias, vt, batch, seq)
        ob = _dsa(qkv, vt, wrow, batch, seq)
        oc = _dilated(qkv, vt, mult, batch, seq)
        od = _moba(qkv, vt, batch, seq)

        x2d = _outproj((oa, ob, oc, od), w_o16[l], x2d, ln1_g[l][None, :], ln1_b[l][None, :], alpha)
        x2d = _mlp(x2d, w_up16[l], w_down16[l], ln2_g[l][None, :], ln2_b[l][None, :], alpha)
    return x2d.reshape(batch, seq, d)
```

```python
import functools
import math

import numpy as np
import jax
import jax.numpy as jnp
from jax import lax
from jax.experimental import pallas as pl
from jax.experimental.pallas import tpu as pltpu

F32 = jnp.float32
BF16 = jnp.bfloat16

HEAD_DIM = 64
GROUP_HEADS = 4
GROUP_W = GROUP_HEADS * HEAD_DIM
IDX_HEADS = 8
IDX_DIM = 32
DSA_TOPK = 256
DIL_PATTERNS = ((128, 1), (512, 4), (2048, 16))
MOBA_BLOCK = 256
MOBA_TOPK = 3
ROPE_THETA = 10000.0
LN_EPS = 1e-5
LOG2E = math.log2(math.e)
LANES = 128
BF16_ROWS = 16
TB = 256
VROWS = HEAD_DIM + BF16_ROWS
NEG_INF = float("-inf")
VMEM_LIMIT = 56 * 1024 * 1024

(CB_QA, CB_KA, CB_VA, CB_VB, CB_VC, CB_VD,
 CB_QB, CB_KB, CB_QC, CB_KC, CB_QD, CB_KD,
 CB_QI, CB_KI8) = range(14)
N_COLBLK = 14
PROJ_TN = 2 * GROUP_W
PLAIN_STEPS, ROPE64_STEPS = 3, 3


def _dot_nt(a, b):
    return lax.dot_general(a, b, (((1,), (1,)), ((), ())), preferred_element_type=F32)


def _layer_norm(y, g, b):
    mu = jnp.mean(y, axis=-1, keepdims=True)
    d = y - mu
    var = jnp.mean(d * d, axis=-1, keepdims=True)
    return d * lax.rsqrt(var + LN_EPS) * g + b


def _tree_sum(parts):
    while len(parts) > 1:
        parts = [parts[n] + parts[n + 1] for n in range(0, len(parts), 2)]
    return parts[0]


def _rope(acc, tab_ref, half):
    cos = tab_ref[:, 0:LANES]
    sin = tab_ref[:, LANES:2 * LANES]
    lane = lax.broadcasted_iota(jnp.int32, (1, LANES), 1)
    first = (lane & (2 * half - 1)) < half
    outs = []
    for c in range(acc.shape[1] // LANES):
        a = acc[:, c * LANES:(c + 1) * LANES]
        swapped = jnp.where(first, pltpu.roll(a, LANES - half, 1), pltpu.roll(a, half, 1))
        outs.append(a * cos + swapped * sin)
    return jnp.concatenate(outs, axis=1)


def _inproj_kernel(x_ref, w_ref, wm_ref, t64_ref, t32_ref, o_ref, misc_ref, xb_ref):
    j = pl.program_id(1)

    @pl.when(j == 0)
    def _():
        xb = x_ref[...].astype(BF16)
        xb_ref[...] = xb
        misc_ref[...] = jnp.dot(xb, wm_ref[...], preferred_element_type=F32)

    acc = jnp.dot(xb_ref[...], w_ref[...], preferred_element_type=F32)

    @pl.when(j < PLAIN_STEPS)
    def _():
        o_ref[...] = acc.astype(BF16)

    @pl.when(jnp.logical_and(j >= PLAIN_STEPS, j < PLAIN_STEPS + ROPE64_STEPS))
    def _():
        o_ref[...] = _rope(acc, t64_ref, HEAD_DIM // 2).astype(BF16)

    @pl.when(j == PLAIN_STEPS + ROPE64_STEPS)
    def _():
        o_ref[...] = _rope(acc, t32_ref, IDX_DIM // 2).astype(BF16)


def _inproj(x2d, w_main, w_misc, tab64, tab32, seq):
    m, d = x2d.shape
    tm = min(1024, seq)
    nt = seq // tm
    return pl.pallas_call(
        _inproj_kernel,
        grid=(m // tm, N_COLBLK * GROUP_W // PROJ_TN),
        in_specs=[
            pl.BlockSpec((tm, d), lambda i, j: (i, 0)),
            pl.BlockSpec((d, PROJ_TN), lambda i, j: (0, j)),
            pl.BlockSpec((d, LANES), lambda i, j: (0, 0)),
            pl.BlockSpec((tm, 2 * LANES), lambda i, j: (i % nt, 0)),
            pl.BlockSpec((tm, 2 * LANES), lambda i, j: (i % nt, 0)),
        ],
        out_specs=[
            pl.BlockSpec((tm, PROJ_TN), lambda i, j: (i, j)),
            pl.BlockSpec((tm, LANES), lambda i, j: (i, 0)),
        ],
        out_shape=[
            jax.ShapeDtypeStruct((m, N_COLBLK * GROUP_W), BF16),
            jax.ShapeDtypeStruct((m, LANES), F32),
        ],
        scratch_shapes=[pltpu.VMEM((tm, d), BF16)],
        compiler_params=pltpu.CompilerParams(
            dimension_semantics=("parallel", "arbitrary"), vmem_limit_bytes=VMEM_LIMIT),
    )(x2d, w_main, w_misc, tab64, tab32)


def _gate_kernel(z_ref, b_ref, o_ref, *, chunks):
    z = z_ref[...] + b_ref[...]
    w = jnp.minimum(z, 0.0) - jnp.log1p(jnp.exp(-jnp.abs(z)))
    rows = w.shape[0]
    lane = lax.broadcasted_iota(jnp.int32, (rows, LANES), 1)
    s = 1
    while s < LANES:
        w = w + jnp.where(lane >= s, pltpu.roll(w, s, 1), 0.0)
        s *= 2
    tot = jnp.broadcast_to(w[:, LANES - 1:LANES], (rows, LANES))
    pos = lax.broadcasted_iota(jnp.int32, (rows, LANES), 0) & (chunks - 1)
    run = tot
    s = 1
    while s < chunks:
        run = run + jnp.where(pos >= s, pltpu.roll(run, s, 0), 0.0)
        s *= 2
    bias = -LOG2E * (w + (run - tot))
    hi = bias.astype(BF16).astype(F32)
    mid = (bias - hi).astype(BF16).astype(F32)
    lo = (bias - hi - mid).astype(BF16).astype(F32)
    o_ref[0] = hi
    o_ref[1] = mid
    o_ref[2] = lo


def _gate_bias(z, bias_rows, chunks):
    b, rows, _ = z.shape
    return pl.pallas_call(
        functools.partial(_gate_kernel, chunks=chunks),
        grid=(b,),
        in_specs=[pl.BlockSpec((None, rows, LANES), lambda i: (i, 0, 0)),
                  pl.BlockSpec((rows, 1), lambda i: (0, 0))],
        out_specs=pl.BlockSpec((None, 3, rows, LANES), lambda i: (i, 0, 0, 0)),
        out_shape=jax.ShapeDtypeStruct((b, 3, rows, LANES), F32),
        compiler_params=pltpu.CompilerParams(dimension_semantics=("parallel",)),
    )(z, bias_rows)


HEADS_W = GROUP_HEADS * TB


def _stack_heads(q_ref, qs_ref):
    q = q_ref[...].astype(F32)
    head_of_lane = lax.broadcasted_iota(jnp.int32, (1, GROUP_W), 1) >> 6
    for h in range(GROUP_HEADS):
        qs_ref[h * TB:(h + 1) * TB, 0:GROUP_W] = jnp.where(head_of_lane == h, q, 0.0).astype(BF16)


def _head(x, h):
    return x[:, h * TB:(h + 1) * TB]


def _tile_heads(x):
    return jnp.concatenate([x] * GROUP_HEADS, axis=1)


def _kblock(k_ref, j):
    return k_ref[pl.ds(pl.multiple_of(j * TB, TB), TB), :]


def _pair_scores(k_ref, qs_ref, j, extra_ref=None):
    kblk = _kblock(k_ref, j)
    outs = []
    for pr in range(GROUP_W // LANES):
        rows = slice(2 * pr * TB, 2 * (pr + 1) * TB)
        kp = kblk[:, pr * LANES:(pr + 1) * LANES]
        qp = qs_ref[rows, pr * LANES:(pr + 1) * LANES]
        if extra_ref is not None:
            kp = jnp.concatenate([kp, _kblock(extra_ref, j)], axis=1)
            qp = jnp.concatenate([qp, qs_ref[rows, GROUP_W:GROUP_W + LANES]], axis=1)
        outs.append(_dot_nt(kp, qp))
    return jnp.concatenate(outs, axis=1)


def _causal_mask(heads=GROUP_HEADS):
    krow = lax.broadcasted_iota(jnp.int32, (TB, heads * TB), 0)
    qcol = lax.broadcasted_iota(jnp.int32, (TB, heads * TB), 1) & (TB - 1)
    return krow <= qcol


def _consume(st, blk_max, vt, m, accs, weight=None, keep=None):
    m_new = jnp.maximum(m, blk_max)
    if keep is not None:
        m_new = jnp.where(keep > 0.5, m_new, m)
    m_safe = jnp.where(m_new == NEG_INF, 0.0, m_new)
    alpha = jnp.exp2(m - m_safe)
    p = jnp.exp2(st - m_safe)
    if weight is not None:
        p = p * weight
    pb = p.astype(BF16)
    new_accs = []
    for h in range(GROUP_HEADS):
        pv = jnp.dot(vt[h * VROWS:(h + 1) * VROWS, :], _head(pb, h), preferred_element_type=F32)
        upd = _head(alpha, h) * accs[h] + pv
        new_accs.append(upd if keep is None else jnp.where(_head(keep, h) > 0.5, upd, accs[h]))
    return m_new, new_accs


TRIP_SHIFT = 4
STEPS_PER_TRIP = 1 << TRIP_SHIFT


def _attn_scratch(qs_width=GROUP_W):
    return [
        pltpu.VMEM((HEADS_W, qs_width), BF16),
        pltpu.VMEM((2, TB, HEADS_W), F32),
        pltpu.VMEM((2, 1, HEADS_W), F32),
        pltpu.VMEM((1, HEADS_W), F32),
        pltpu.VMEM((GROUP_HEADS * VROWS, TB), F32),
    ]


def _attend_sequence(scores_of, vt_ref, o_ref, scratch, n_steps, block_of, step_args):
    _, st_ref, bm_ref, m_ref, acc_ref = scratch

    def issue(s, slot, first=False):
        st = scores_of(block_of(s))
        mask_fn = step_args(s, first).get("mask_fn")
        if mask_fn is not None:
            st = mask_fn(st)
        st_ref[slot] = st
        bm_ref[slot] = jnp.max(st, axis=0, keepdims=True)

    def consume(s, slot, first=False):
        args = step_args(s, first)
        accs = [acc_ref[h * VROWS:(h + 1) * VROWS, :] for h in range(GROUP_HEADS)]
        m, accs = _consume(st_ref[slot], bm_ref[slot], vt_ref[block_of(s)], m_ref[...], accs,
                           weight=args.get("weight"), keep=args.get("keep"))
        m_ref[...] = m
        for h in range(GROUP_HEADS):
            acc_ref[h * VROWS:(h + 1) * VROWS, :] = accs[h]

    m_ref[...] = jnp.full((1, HEADS_W), NEG_INF, F32)
    acc_ref[...] = jnp.zeros_like(acc_ref)
    issue(0, 0, True)
    issue(1, 1)
    consume(0, 0, True)

    def run(s, count):
        for u in range(count):
            issue(s + u + 1, u & 1)
            consume(s + u, (u + 1) & 1)

    def trip(tt, carry):
        run(1 + STEPS_PER_TRIP * tt, STEPS_PER_TRIP)
        return carry

    n_trips = n_steps >> TRIP_SHIFT
    lax.fori_loop(0, n_trips, trip, 0)

    for bit in range(TRIP_SHIFT - 1, 0, -1):
        size = 1 << bit

        @pl.when((n_steps & size) != 0)
        def _(size=size):
            run(1 + ((n_steps >> TRIP_SHIFT) << TRIP_SHIFT) + (n_steps & (STEPS_PER_TRIP - 2 * size)), size)

    @pl.when((n_steps & 1) == 1)
    def _():
        consume(n_steps, 1)

    ot = jnp.concatenate(
        [acc_ref[h * VROWS:h * VROWS + HEAD_DIM, :] / acc_ref[h * VROWS + HEAD_DIM:h * VROWS + HEAD_DIM + 1, :]
         for h in range(GROUP_HEADS)], axis=0)
    o_ref[...] = ot.T.astype(o_ref.dtype)


def _diagonal_first(i):
    return lambda s: jnp.where(s == 0, i, s - 1)


def _attn_specs(seq, colblks):
    nq = seq // TB
    specs = []
    for kind, cb in colblks:
        if kind == "q":
            specs.append(pl.BlockSpec((TB, GROUP_W), lambda b, i, cb=cb: (b * nq + i, cb)))
        else:
            specs.append(pl.BlockSpec((seq, GROUP_W), lambda b, i, cb=cb: (b, cb)))
    return specs


def _vt_spec(nkb, cb):
    return pl.BlockSpec((None, nkb, GROUP_HEADS * VROWS, TB), lambda b, i: (b, 0, cb - CB_VA, 0))


def _o_spec(seq):
    nq = seq // TB
    return pl.BlockSpec((TB, GROUP_W), lambda b, i: (b * nq + i, 0))


_ATTN_PARAMS = dict(
    compiler_params=pltpu.CompilerParams(
        dimension_semantics=("parallel", "arbitrary"), vmem_limit_bytes=VMEM_LIMIT))


BIAS_TERMS = 3


def _fox_kernel(q_ref, k_ref, cb_ref, vt_ref, o_ref, *scratch):
    i = pl.program_id(1)
    qs_ref = scratch[0]
    _stack_heads(q_ref, qs_ref)
    lane = lax.broadcasted_iota(jnp.int32, (TB, LANES), 1)
    for h in range(GROUP_HEADS):
        pick = jnp.logical_and(lane >= BIAS_TERMS * h, lane < BIAS_TERMS * (h + 1))
        qs_ref[h * TB:(h + 1) * TB, GROUP_W:GROUP_W + LANES] = jnp.where(pick, 1.0, 0.0).astype(BF16)

    def scores_of(j):
        return _pair_scores(k_ref, qs_ref, j, cb_ref)

    def step_args(s, diag):
        if diag:
            return dict(mask_fn=lambda st: jnp.where(_causal_mask(), st, NEG_INF))
        return {}

    _attend_sequence(scores_of, vt_ref, o_ref, scratch, i, _diagonal_first(i), step_args)


def _fox(qkv, cbias, vt, batch, seq):
    nkb = seq // TB
    return pl.pallas_call(
        _fox_kernel,
        grid=(batch, seq // TB),
        in_specs=_attn_specs(seq, (("q", CB_QA), ("k", CB_KA))) + [
            pl.BlockSpec((seq, LANES), lambda b, i: (b, 0)),
            _vt_spec(nkb, CB_VA),
        ],
        out_specs=_o_spec(seq),
        out_shape=jax.ShapeDtypeStruct((batch * seq, GROUP_W), BF16),
        scratch_shapes=_attn_scratch(GROUP_W + LANES),
        **_ATTN_PARAMS,
    )(qkv, qkv, cbias, vt)


HIGH_HALF = -(1 << 16)


def _key_to_float(key_u):
    key = key_u ^ jnp.int32(-2 ** 31)
    bits = jnp.where(key < 0, key ^ jnp.int32(0x7FFFFFFF), key)
    below_neg_inf = jnp.logical_and(key_u >= 0, key_u < jnp.int32(0x007FFFFF))
    return jnp.where(below_neg_inf, NEG_INF, lax.bitcast_convert_type(bits, F32))


def _dsa_kernel(q_ref, k_ref, vt_ref, qi_ref, k8_ref, w_ref, o_ref, strip_ref, strip_hi_ref, qm_ref, cnt_ref, *scratch,
                topk, idx_bits):
    i = pl.program_id(1)
    nblk = i + 1

    qi = qi_ref[...].astype(F32)
    low_lanes = lax.broadcasted_iota(jnp.int32, (1, LANES), 1) < IDX_DIM
    per_half = LANES // IDX_DIM
    for h in range(IDX_HEADS):
        half = qi[:, (h // per_half) * LANES:(h // per_half + 1) * LANES]
        if h % per_half:
            half = pltpu.roll(half, LANES - IDX_DIM * (h % per_half), 1)
        qm_ref[h] = jnp.where(low_lanes, half, 0.0).astype(BF16)

    wrow = w_ref[...]

    def score_block(j, diag):
        kblk = _kblock(k8_ref, j)[:, 0:LANES]
        sc = _tree_sum([wrow[h:h + 1, :] * jnp.maximum(_dot_nt(kblk, qm_ref[h]), 0.0)
                        for h in range(IDX_HEADS)])
        if diag:
            sc = jnp.where(_causal_mask(1), sc, NEG_INF)
        strip_ref[j] = sc
        strip_hi_ref[j] = lax.bitcast_convert_type(
            lax.bitcast_convert_type(sc, jnp.int32) & HIGH_HALF, F32).astype(BF16)

    def score_trip(t, carry):
        for u in range(4):
            score_block(4 * t + u, False)
        return carry

    lax.fori_loop(0, i >> 2, score_trip, 0)

    @pl.when((i & 2) != 0)
    def _():
        score_block(((i >> 2) << 2), False)
        score_block(((i >> 2) << 2) + 1, False)

    @pl.when((i & 1) == 1)
    def _():
        score_block(i - 1, False)

    score_block(i, True)

    def count(pred):
        def body(j, acc):
            hit = jnp.where(pred(j, strip_ref[j]), 1.0, 0.0)
            return acc + _tree_sum([hit[r:r + 8, :] for r in range(0, TB, 8)])
        acc = lax.fori_loop(0, nblk, body, jnp.zeros((8, TB), F32))
        return jnp.sum(acc, axis=0, keepdims=True)

    def count_ge(ref, t, rows):
        one, zero = jnp.ones((), ref.dtype), jnp.zeros((), ref.dtype)

        def hits(j):
            hit = jnp.where(ref[j] >= t, one, zero)
            return _tree_sum([hit[r:r + rows, :] for r in range(0, TB, rows)]).astype(F32)

        def add(j0, n):
            cnt_ref[0:rows, :] += _tree_sum([hits(j0 + u) for u in range(n)])

        def trip(p, carry):
            add(4 * p, 4)
            return carry

        cnt_ref[0:rows, :] = jnp.zeros((rows, TB), F32)
        lax.fori_loop(0, nblk >> 2, trip, 0)

        @pl.when((nblk & 2) != 0)
        def _():
            add((nblk >> 2) << 2, 2)

        @pl.when((nblk & 1) == 1)
        def _():
            add(nblk - 1, 1)

        return jnp.sum(cnt_ref[0:rows, :], axis=0, keepdims=True)

    def radix_pass(it, state, high):
        prefix, cnt_at_prefix = state
        cand = prefix | jnp.left_shift(jnp.int32(1), 31 - it)
        t = _key_to_float(cand)
        if high:
            t = lax.bitcast_convert_type(lax.bitcast_convert_type(t, jnp.int32) & HIGH_HALF, F32)
            cnt = count_ge(strip_hi_ref, t.astype(BF16), BF16_ROWS)
        else:
            cnt = count_ge(strip_ref, t, 8)
        take = cnt >= topk
        return jnp.where(take, cand, prefix), jnp.where(take, cnt, cnt_at_prefix)

    n_all = jnp.full((1, TB), 1.0, F32) * (nblk * TB).astype(F32)
    state = (jnp.zeros((1, TB), jnp.int32), n_all)
    state = lax.fori_loop(0, 16, lambda it, s: radix_pass(it, s, True), state)
    prefix, cnt_ge = lax.fori_loop(16, 32, lambda it, s: radix_pass(it, s, False), state)
    thr = _key_to_float(prefix)

    @pl.when(jnp.max(cnt_ge) > topk)
    def _():
        need = topk - count(lambda j, blk: blk > thr)
        krow = lax.broadcasted_iota(jnp.int32, (TB, TB), 0)

        def idx_body(it, lim):
            cand = lim | jnp.left_shift(jnp.int32(1), idx_bits - 1 - it)
            cnt = count(lambda j, blk: jnp.logical_and(blk == thr, krow + j * TB < cand))
            return jnp.where(cnt <= need, cand, lim)

        lim = lax.fori_loop(0, idx_bits, idx_body, jnp.zeros((1, TB), jnp.int32))

        def drop_body(j, carry):
            blk = strip_ref[j]
            drop = jnp.logical_and(blk == thr, krow + j * TB >= lim)
            strip_ref[j] = jnp.where(drop, NEG_INF, blk)
            return carry

        lax.fori_loop(0, nblk, drop_body, 0)

    qs_ref = scratch[0]
    _stack_heads(q_ref, qs_ref)

    block_of = _diagonal_first(i)

    def step_args(s, diag):
        def mask_fn(st):
            blk = strip_ref[block_of(s)]
            sel = blk >= thr
            if diag:
                sel = jnp.logical_and(sel, blk > NEG_INF)
            return jnp.concatenate(
                [jnp.where(sel, _head(st, h), NEG_INF) for h in range(GROUP_HEADS)], axis=1)
        return dict(mask_fn=mask_fn)

    _attend_sequence(lambda j: _pair_scores(k_ref, qs_ref, j), vt_ref, o_ref, scratch,
                     i, block_of, step_args)


def _dsa(qkv, vt, wrow, batch, seq):
    nkb = seq // TB
    topk = min(DSA_TOPK, seq // 4)
    kern = functools.partial(_dsa_kernel, topk=float(topk), idx_bits=int(seq).bit_length())
    return pl.pallas_call(
        kern,
        grid=(batch, seq // TB),
        in_specs=_attn_specs(seq, (("q", CB_QB), ("k", CB_KB))) + [_vt_spec(nkb, CB_VB)]
        + _attn_specs(seq, (("q", CB_QI), ("k", CB_KI8)))
        + [pl.BlockSpec((None, IDX_HEADS, TB), lambda b, i: (b, 0, i))],
        out_specs=_o_spec(seq),
        out_shape=jax.ShapeDtypeStruct((batch * seq, GROUP_W), BF16),
        scratch_shapes=[pltpu.VMEM((nkb, TB, TB), F32), pltpu.VMEM((nkb, TB, TB), BF16),
                        pltpu.VMEM((IDX_HEADS, TB, LANES), BF16),
                        pltpu.VMEM((BF16_ROWS, TB), F32)] + _attn_scratch(),
        **_ATTN_PARAMS,
    )(qkv, qkv, vt, qkv, qkv, wrow)


DIL_SPAN = max(w for w, _ in DIL_PATTERNS) // TB + 1


def _dilated_multiplicity():
    k = np.arange(TB)[:, None]
    q = np.arange(TB)[None, :]
    out = np.zeros((DIL_SPAN, TB, TB), np.float32)
    for delta in range(DIL_SPAN):
        d = q - k + TB * delta
        for w, r in DIL_PATTERNS:
            out[delta] += ((d >= 0) & (d % r == 0) & (d <= w)).astype(np.float32)
    return out


def _dil_kernel(q_ref, k_ref, vt_ref, mult_ref, o_ref, *scratch):
    i = pl.program_id(1)
    qs_ref = scratch[0]
    _stack_heads(q_ref, qs_ref)
    n = jnp.minimum(i, DIL_SPAN - 1)

    def step_args(s, first):
        mult = _tile_heads(mult_ref[jnp.minimum(s, DIL_SPAN - 1)])
        return dict(mask_fn=lambda st: jnp.where(mult > 0.0, st, NEG_INF), weight=mult)

    _attend_sequence(lambda j: _pair_scores(k_ref, qs_ref, j), vt_ref, o_ref, scratch,
                     n, lambda s: jnp.maximum(i - s, 0), step_args)


def _dilated(qkv, vt, mult, batch, seq):
    nkb = seq // TB
    return pl.pallas_call(
        _dil_kernel,
        grid=(batch, seq // TB),
        in_specs=_attn_specs(seq, (("q", CB_QC), ("k", CB_KC))) + [
            _vt_spec(nkb, CB_VC),
            pl.BlockSpec((DIL_SPAN, TB, TB), lambda b, i: (0, 0, 0)),
        ],
        out_specs=_o_spec(seq),
        out_shape=jax.ShapeDtypeStruct((batch * seq, GROUP_W), BF16),
        scratch_shapes=_attn_scratch(),
        **_ATTN_PARAMS,
    )(qkv, qkv, vt, mult)


def _moba_kernel(q_ref, k_ref, vt_ref, o_ref, kmean_ref, sel_ref, *scratch, nkb, nkb_pad, topn):
    i = pl.program_id(1)

    @pl.when(i == 0)
    def _():
        kmean_ref[...] = jnp.zeros_like(kmean_ref)
        for n in range(nkb):
            kb = k_ref[n * TB:(n + 1) * TB, :].astype(F32)
            kmean_ref[n:n + 1, :] = jnp.mean(kb, axis=0, keepdims=True)

    qs_ref = scratch[0]
    _stack_heads(q_ref, qs_ref)
    blk_id = lax.broadcasted_iota(jnp.int32, (nkb_pad, HEADS_W), 0)
    blk_f = blk_id.astype(F32)
    g = _dot_nt(kmean_ref[...].astype(BF16), qs_ref[...])
    g = jnp.where(blk_id < i, g, NEG_INF)
    sel = jnp.zeros_like(g)
    for _ in range(topn):
        mx = jnp.max(g, axis=0, keepdims=True)
        is_max = jnp.logical_and(g == mx, mx > NEG_INF)
        first = jnp.min(jnp.where(is_max, blk_f, float(nkb_pad)), axis=0, keepdims=True)
        pick = blk_f == first
        sel = jnp.where(pick, 1.0, sel)
        g = jnp.where(pick, NEG_INF, g)
    sel_ref[...] = sel

    block_of = _diagonal_first(i)

    def step_args(s, diag):
        if diag:
            return dict(mask_fn=lambda st: jnp.where(_causal_mask(), st, NEG_INF))
        return dict(keep=sel_ref[pl.ds(block_of(s), 1), :])

    _attend_sequence(lambda j: _pair_scores(k_ref, qs_ref, j), vt_ref, o_ref, scratch,
                     i, block_of, step_args)


def _moba(qkv, vt, batch, seq):
    nkb = seq // TB
    nkb_pad = -(-nkb // 8) * 8
    kern = functools.partial(_moba_kernel, nkb=nkb, nkb_pad=nkb_pad, topn=min(MOBA_TOPK, nkb))
    return pl.pallas_call(
        kern,
        grid=(batch, seq // TB),
        in_specs=_attn_specs(seq, (("q", CB_QD), ("k", CB_KD))) + [_vt_spec(nkb, CB_VD)],
        out_specs=_o_spec(seq),
        out_shape=jax.ShapeDtypeStruct((batch * seq, GROUP_W), BF16),
        scratch_shapes=[pltpu.VMEM((nkb_pad, GROUP_W), F32),
                        pltpu.VMEM((nkb_pad, HEADS_W), F32)] + _attn_scratch(),
        compiler_params=pltpu.CompilerParams(
            dimension_semantics=("arbitrary", "arbitrary"), vmem_limit_bytes=VMEM_LIMIT),
    )(qkv, qkv, vt)


def _outproj_kernel(oa_ref, ob_ref, oc_ref, od_ref, w_ref, x_ref, g_ref, b_ref, y_ref, *, alpha):
    acc = alpha * x_ref[...]
    for gi, o_ref in enumerate((oa_ref, ob_ref, oc_ref, od_ref)):
        acc = acc + jnp.dot(o_ref[...], w_ref[gi * GROUP_W:(gi + 1) * GROUP_W, :],
                            preferred_element_type=F32)
    y_ref[...] = _layer_norm(acc, g_ref[...], b_ref[...])


def _outproj(os4, w_o, x2d, g, b, alpha):
    m, d = x2d.shape
    tm = min(512, m)
    row = lambda i: (i, 0)
    fixed = lambda i: (0, 0)
    return pl.pallas_call(
        functools.partial(_outproj_kernel, alpha=alpha),
        grid=(m // tm,),
        in_specs=[pl.BlockSpec((tm, GROUP_W), row)] * 4 + [
            pl.BlockSpec((d, d), fixed), pl.BlockSpec((tm, d), row),
            pl.BlockSpec((1, d), fixed), pl.BlockSpec((1, d), fixed)],
        out_specs=pl.BlockSpec((tm, d), row),
        out_shape=jax.ShapeDtypeStruct((m, d), F32),
        compiler_params=pltpu.CompilerParams(
            dimension_semantics=("parallel",), vmem_limit_bytes=VMEM_LIMIT),
    )(*os4, w_o, x2d, g, b)


def _mlp_kernel(x_ref, wu_ref, wd_ref, g_ref, b_ref, y_ref, xb_ref, acc_ref, *, alpha):
    f = pl.program_id(1)

    @pl.when(f == 0)
    def _():
        x = x_ref[...]
        xb_ref[...] = x.astype(BF16)
        acc_ref[...] = alpha * x

    h = jnp.dot(xb_ref[...], wu_ref[...], preferred_element_type=F32)
    h = jnp.square(jnp.maximum(h, 0.0))
    acc_ref[...] += jnp.dot(h.astype(BF16), wd_ref[...], preferred_element_type=F32)

    @pl.when(f == pl.num_programs(1) - 1)
    def _():
        y_ref[...] = _layer_norm(acc_ref[...], g_ref[...], b_ref[...])


def _mlp(x2d, w_up, w_down, g, b, alpha):
    m, d = x2d.shape
    dff = w_up.shape[1]
    tm = min(1024, m)
    tf = min(1024, dff)
    return pl.pallas_call(
        functools.partial(_mlp_kernel, alpha=alpha),
        grid=(m // tm, dff // tf),
        in_specs=[
            pl.BlockSpec((tm, d), lambda i, f: (i, 0)),
            pl.BlockSpec((d, tf), lambda i, f: (0, f)),
            pl.BlockSpec((tf, d), lambda i, f: (f, 0)),
            pl.BlockSpec((1, d), lambda i, f: (0, 0)),
            pl.BlockSpec((1, d), lambda i, f: (0, 0)),
        ],
        out_specs=pl.BlockSpec((tm, d), lambda i, f: (i, 0)),
        out_shape=jax.ShapeDtypeStruct((m, d), F32),
        scratch_shapes=[pltpu.VMEM((tm, d), BF16), pltpu.VMEM((tm, d), F32)],
        compiler_params=pltpu.CompilerParams(
            dimension_semantics=("parallel", "arbitrary"), vmem_limit_bytes=VMEM_LIMIT),
    )(x2d, w_up, w_down, g, b)


def _column_plan():
    hq = GROUP_W
    sizes = (hq, hq, hq, GROUP_HEADS,
             hq, hq, hq, IDX_HEADS * IDX_DIM, IDX_DIM, IDX_HEADS,
             hq, hq, hq, hq, hq, hq)
    offs = np.concatenate([[0], np.cumsum(sizes)])
    (qa, ka, va, fa, qb, kb, vb, qi, ki, wi, qc, kc, vc, qd, kd, vd) = [
        np.arange(offs[n], offs[n + 1]) for n in range(len(sizes))]
    blocks = {CB_QA: qa, CB_KA: ka, CB_VA: va, CB_VB: vb, CB_VC: vc, CB_VD: vd,
              CB_QB: qb, CB_KB: kb, CB_QC: qc, CB_KC: kc, CB_QD: qd, CB_KD: kd,
              CB_QI: qi, CB_KI8: np.tile(ki, IDX_HEADS)}
    scale = np.ones(N_COLBLK * GROUP_W, np.float32)
    for cb in (CB_QA, CB_QB, CB_QC, CB_QD):
        scale[cb * GROUP_W:(cb + 1) * GROUP_W] = LOG2E * HEAD_DIM ** -0.5
    return np.concatenate([blocks[cb] for cb in range(N_COLBLK)]), scale, np.concatenate([fa, wi])


def _rope_table(seq, dim):
    half = dim // 2
    lane = np.arange(LANES)
    inv = ROPE_THETA ** (-(jnp.arange(half, dtype=F32)) / half)
    ang = jnp.arange(seq, dtype=F32)[:, None] * inv[None, :]
    idx = lane % half
    sign = np.where((lane % dim) < half, -1.0, 1.0).astype(np.float32)
    return jnp.concatenate([jnp.cos(ang)[:, idx], jnp.sin(ang)[:, idx] * sign[None, :]], axis=1)


def _vt_all(qkv, batch, seq):
    nkb = seq // TB
    heads = (CB_VD - CB_VA + 1) * GROUP_HEADS
    v = qkv[:, CB_VA * GROUP_W:(CB_VD + 1) * GROUP_W].reshape(batch, nkb, TB, heads, HEAD_DIM)
    pad = jnp.zeros((batch, nkb, TB, heads, VROWS - HEAD_DIM), v.dtype).at[..., 0].set(1)
    v = jnp.concatenate([v, pad], axis=-1).reshape(batch, nkb, TB, heads * VROWS)
    return v.transpose(0, 1, 3, 2)


def kernel(x, w_in, b_f, w_o, ln1_g, ln1_b, w_up, w_down, ln2_g, ln2_b):
    batch, seq, d = x.shape
    depth = w_in.shape[0]
    assert seq % TB == 0 and d == 4 * GROUP_W
    chunks = seq // LANES
    assert chunks & (chunks - 1) == 0
    alpha = (2.0 * depth) ** 0.25

    main_cols, main_scale, misc_cols = _column_plan()
    w_main = (w_in[:, :, main_cols] * main_scale).astype(BF16)
    w_misc = jnp.pad(w_in[:, :, misc_cols], ((0, 0), (0, 0), (0, LANES - len(misc_cols)))).astype(BF16)
    w_o16, w_up16, w_down16 = w_o.astype(BF16), w_up.astype(BF16), w_down.astype(BF16)
    tab64 = _rope_table(seq, HEAD_DIM)
    tab32 = _rope_table(seq, IDX_DIM)
    mult = jnp.asarray(_dilated_multiplicity())

    x2d = x.reshape(batch * seq, d)
    for l in range(depth):
        qkv, misc = _inproj(x2d, w_main[l], w_misc[l], tab64, tab32, seq)
        misc3 = misc.reshape(batch, seq, LANES)
        z = misc3[:, :, 0:GROUP_HEADS].transpose(0, 2, 1).reshape(batch, GROUP_HEADS * chunks, LANES)
        bias_rows = jnp.repeat(b_f[l].astype(F32), chunks)[:, None]
        gb = _gate_bias(z, bias_rows, chunks)
        gb = gb.reshape(batch, BIAS_TERMS, GROUP_HEADS, seq).transpose(0, 3, 2, 1)
        cbias = jnp.pad(gb.reshape(batch * seq, GROUP_HEADS * BIAS_TERMS),
                        ((0, 0), (0, LANES - GROUP_HEADS * BIAS_TERMS))).astype(BF16)
        wrow = misc3[:, :, GROUP_HEADS:GROUP_HEADS + IDX_HEADS].transpose(0, 2, 1)

        vt = _vt_all(qkv, batch, seq)
        oa = _fox(qkv, cbias, vt, batch, seq)
        ob = _dsa(qkv, vt, wrow, batch, seq)
        oc = _dilated(qkv, vt, mult, batch, seq)
        od = _moba(qkv, vt, batch, seq)

        x2d = _outproj((oa, ob, oc, od), w_o16[l], x2d, ln1_g[l][None, :], ln1_b[l][None, :], alpha)
        x2d = _mlp(x2d, w_up16[l], w_down16[l], ln2_g[l][None, :], ln2_b[l][None, :], alpha)
    return x2d.reshape(batch, seq, d)
```

```python
import functools
import math

import numpy as np
import jax
import jax.numpy as jnp
from jax import lax
from jax.experimental import pallas as pl
from jax.experimental.pallas import tpu as pltpu

F32 = jnp.float32
BF16 = jnp.bfloat16

HEAD_DIM = 64
GROUP_HEADS = 4
GROUP_W = GROUP_HEADS * HEAD_DIM
IDX_HEADS = 8
IDX_DIM = 32
DSA_TOPK = 256
DIL_PATTERNS = ((128, 1), (512, 4), (2048, 16))
MOBA_BLOCK = 256
MOBA_TOPK = 3
ROPE_THETA = 10000.0
LN_EPS = 1e-5
LOG2E = math.log2(math.e)
LANES = 128
BF16_ROWS = 16
TB = 256
VROWS = HEAD_DIM + BF16_ROWS
NEG_INF = float("-inf")
VMEM_LIMIT = 56 * 1024 * 1024

(CB_QA, CB_KA, CB_VA, CB_VB, CB_VC, CB_VD,
 CB_QB, CB_KB, CB_QC, CB_KC, CB_QD, CB_KD,
 CB_QI, CB_KI8) = range(14)
N_COLBLK = 14
PROJ_TN = 2 * GROUP_W
PLAIN_STEPS, ROPE64_STEPS = 3, 3


def _dot_nt(a, b):
    return lax.dot_general(a, b, (((1,), (1,)), ((), ())), preferred_element_type=F32)


def _layer_norm(y, g, b):
    mu = jnp.mean(y, axis=-1, keepdims=True)
    d = y - mu
    var = jnp.mean(d * d, axis=-1, keepdims=True)
    return d * lax.rsqrt(var + LN_EPS) * g + b


def _tree_sum(parts):
    while len(parts) > 1:
        parts = [parts[n] + parts[n + 1] for n in range(0, len(parts), 2)]
    return parts[0]


def _rope(acc, tab_ref, half):
    cos = tab_ref[:, 0:LANES]
    sin = tab_ref[:, LANES:2 * LANES]
    lane = lax.broadcasted_iota(jnp.int32, (1, LANES), 1)
    first = (lane & (2 * half - 1)) < half
    outs = []
    for c in range(acc.shape[1] // LANES):
        a = acc[:, c * LANES:(c + 1) * LANES]
        swapped = jnp.where(first, pltpu.roll(a, LANES - half, 1), pltpu.roll(a, half, 1))
        outs.append(a * cos + swapped * sin)
    return jnp.concatenate(outs, axis=1)


def _inproj_kernel(x_ref, w_ref, wm_ref, t64_ref, t32_ref, o_ref, misc_ref, xb_ref):
    j = pl.program_id(1)

    @pl.when(j == 0)
    def _():
        xb = x_ref[...].astype(BF16)
        xb_ref[...] = xb
        misc_ref[...] = jnp.dot(xb, wm_ref[...], preferred_element_type=F32)

    acc = jnp.dot(xb_ref[...], w_ref[...], preferred_element_type=F32)

    @pl.when(j < PLAIN_STEPS)
    def _():
        o_ref[...] = acc.astype(BF16)

    @pl.when(jnp.logical_and(j >= PLAIN_STEPS, j < PLAIN_STEPS + ROPE64_STEPS))
    def _():
        o_ref[...] = _rope(acc, t64_ref, HEAD_DIM // 2).astype(BF16)

    @pl.when(j == PLAIN_STEPS + ROPE64_STEPS)
    def _():
        o_ref[...] = _rope(acc, t32_ref, IDX_DIM // 2).astype(BF16)


def _inproj(x2d, w_main, w_misc, tab64, tab32, seq):
    m, d = x2d.shape
    tm = min(1024, seq)
    nt = seq // tm
    return pl.pallas_call(
        _inproj_kernel,
        grid=(m // tm, N_COLBLK * GROUP_W // PROJ_TN),
        in_specs=[
            pl.BlockSpec((tm, d), lambda i, j: (i, 0)),
            pl.BlockSpec((d, PROJ_TN), lambda i, j: (0, j)),
            pl.BlockSpec((d, LANES), lambda i, j: (0, 0)),
            pl.BlockSpec((tm, 2 * LANES), lambda i, j: (i % nt, 0)),
            pl.BlockSpec((tm, 2 * LANES), lambda i, j: (i % nt, 0)),
        ],
        out_specs=[
            pl.BlockSpec((tm, PROJ_TN), lambda i, j: (i, j)),
            pl.BlockSpec((tm, LANES), lambda i, j: (i, 0)),
        ],
        out_shape=[
            jax.ShapeDtypeStruct((m, N_COLBLK * GROUP_W), BF16),
            jax.ShapeDtypeStruct((m, LANES), F32),
        ],
        scratch_shapes=[pltpu.VMEM((tm, d), BF16)],
        compiler_params=pltpu.CompilerParams(
            dimension_semantics=("parallel", "arbitrary"), vmem_limit_bytes=VMEM_LIMIT),
    )(x2d, w_main, w_misc, tab64, tab32)


def _gate_kernel(z_ref, b_ref, o_ref, *, chunks):
    z = z_ref[...] + b_ref[...]
    w = jnp.minimum(z, 0.0) - jnp.log1p(jnp.exp(-jnp.abs(z)))
    rows = w.shape[0]
    lane = lax.broadcasted_iota(jnp.int32, (rows, LANES), 1)
    s = 1
    while s < LANES:
        w = w + jnp.where(lane >= s, pltpu.roll(w, s, 1), 0.0)
        s *= 2
    tot = jnp.broadcast_to(w[:, LANES - 1:LANES], (rows, LANES))
    pos = lax.broadcasted_iota(jnp.int32, (rows, LANES), 0) & (chunks - 1)
    run = tot
    s = 1
    while s < chunks:
        run = run + jnp.where(pos >= s, pltpu.roll(run, s, 0), 0.0)
        s *= 2
    bias = -LOG2E * (w + (run - tot))
    hi = bias.astype(BF16).astype(F32)
    mid = (bias - hi).astype(BF16).astype(F32)
    lo = (bias - hi - mid).astype(BF16).astype(F32)
    o_ref[0] = hi
    o_ref[1] = mid
    o_ref[2] = lo


def _gate_bias(z, bias_rows, chunks):
    b, rows, _ = z.shape
    return pl.pallas_call(
        functools.partial(_gate_kernel, chunks=chunks),
        grid=(b,),
        in_specs=[pl.BlockSpec((None, rows, LANES), lambda i: (i, 0, 0)),
                  pl.BlockSpec((rows, 1), lambda i: (0, 0))],
        out_specs=pl.BlockSpec((None, 3, rows, LANES), lambda i: (i, 0, 0, 0)),
        out_shape=jax.ShapeDtypeStruct((b, 3, rows, LANES), F32),
        compiler_params=pltpu.CompilerParams(dimension_semantics=("parallel",)),
    )(z, bias_rows)


HEADS_W = GROUP_HEADS * TB


def _stack_heads(q_ref, qs_ref):
    q = q_ref[...].astype(F32)
    head_of_lane = lax.broadcasted_iota(jnp.int32, (1, GROUP_W), 1) >> 6
    for h in range(GROUP_HEADS):
        qs_ref[h * TB:(h + 1) * TB, 0:GROUP_W] = jnp.where(head_of_lane == h, q, 0.0).astype(BF16)


def _head(x, h):
    return x[:, h * TB:(h + 1) * TB]


def _tile_heads(x):
    return jnp.concatenate([x] * GROUP_HEADS, axis=1)


def _kblock(k_ref, j):
    return k_ref[pl.ds(pl.multiple_of(j * TB, TB), TB), :]


def _pair_scores(k_ref, qs_ref, j, extra_ref=None):
    kblk = _kblock(k_ref, j)
    outs = []
    for pr in range(GROUP_W // LANES):
        rows = slice(2 * pr * TB, 2 * (pr + 1) * TB)
        kp = kblk[:, pr * LANES:(pr + 1) * LANES]
        qp = qs_ref[rows, pr * LANES:(pr + 1) * LANES]
        if extra_ref is not None:
            kp = jnp.concatenate([kp, _kblock(extra_ref, j)], axis=1)
            qp = jnp.concatenate([qp, qs_ref[rows, GROUP_W:GROUP_W + LANES]], axis=1)
        outs.append(_dot_nt(kp, qp))
    return jnp.concatenate(outs, axis=1)


def _causal_mask(heads=GROUP_HEADS):
    krow = lax.broadcasted_iota(jnp.int32, (TB, heads * TB), 0)
    qcol = lax.broadcasted_iota(jnp.int32, (TB, heads * TB), 1) & (TB - 1)
    return krow <= qcol


def _consume(st, blk_max, vt, m, accs, weight=None, keep=None):
    m_new = jnp.maximum(m, blk_max)
    if keep is not None:
        m_new = jnp.where(keep > 0.5, m_new, m)
    m_safe = jnp.where(m_new == NEG_INF, 0.0, m_new)
    alpha = jnp.exp2(m - m_safe)
    p = jnp.exp2(st - m_safe)
    if weight is not None:
        p = p * weight
    pb = p.astype(BF16)
    new_accs = []
    for h in range(GROUP_HEADS):
        pv = jnp.dot(vt[h * VROWS:(h + 1) * VROWS, :], _head(pb, h), preferred_element_type=F32)
        upd = _head(alpha, h) * accs[h] + pv
        new_accs.append(upd if keep is None else jnp.where(_head(keep, h) > 0.5, upd, accs[h]))
    return m_new, new_accs


TRIP_SHIFT = 3
STEPS_PER_TRIP = 1 << TRIP_SHIFT


def _attn_scratch(qs_width=GROUP_W):
    return [
        pltpu.VMEM((HEADS_W, qs_width), BF16),
        pltpu.VMEM((2, TB, HEADS_W), F32),
        pltpu.VMEM((2, 1, HEADS_W), F32),
        pltpu.VMEM((1, HEADS_W), F32),
        pltpu.VMEM((GROUP_HEADS * VROWS, TB), F32),
    ]


def _attend_sequence(scores_of, vt_ref, o_ref, scratch, n_steps, block_of, step_args):
    _, st_ref, bm_ref, m_ref, acc_ref = scratch

    def issue(s, slot, first=False):
        st = scores_of(block_of(s))
        mask_fn = step_args(s, first).get("mask_fn")
        if mask_fn is not None:
            st = mask_fn(st)
        st_ref[slot] = st
        bm_ref[slot] = jnp.max(st, axis=0, keepdims=True)

    def consume(s, slot, first=False):
        args = step_args(s, first)
        accs = [acc_ref[h * VROWS:(h + 1) * VROWS, :] for h in range(GROUP_HEADS)]
        m, accs = _consume(st_ref[slot], bm_ref[slot], vt_ref[block_of(s)], m_ref[...], accs,
                           weight=args.get("weight"), keep=args.get("keep"))
        m_ref[...] = m
        for h in range(GROUP_HEADS):
            acc_ref[h * VROWS:(h + 1) * VROWS, :] = accs[h]

    m_ref[...] = jnp.full((1, HEADS_W), NEG_INF, F32)
    acc_ref[...] = jnp.zeros_like(acc_ref)
    issue(0, 0, True)
    issue(1, 1)
    consume(0, 0, True)

    def run(s, count):
        for u in range(count):
            issue(s + u + 1, u & 1)
            consume(s + u, (u + 1) & 1)

    def trip(tt, carry):
        run(1 + STEPS_PER_TRIP * tt, STEPS_PER_TRIP)
        return carry

    n_trips = n_steps >> TRIP_SHIFT
    lax.fori_loop(0, n_trips, trip, 0)

    for bit in range(TRIP_SHIFT - 1, 0, -1):
        size = 1 << bit

        @pl.when((n_steps & size) != 0)
        def _(size=size):
            run(1 + ((n_steps >> TRIP_SHIFT) << TRIP_SHIFT) + (n_steps & (STEPS_PER_TRIP - 2 * size)), size)

    @pl.when((n_steps & 1) == 1)
    def _():
        consume(n_steps, 1)

    ot = jnp.concatenate(
        [acc_ref[h * VROWS:h * VROWS + HEAD_DIM, :] / acc_ref[h * VROWS + HEAD_DIM:h * VROWS + HEAD_DIM + 1, :]
         for h in range(GROUP_HEADS)], axis=0)
    o_ref[...] = ot.T.astype(o_ref.dtype)


def _diagonal_first(i):
    return lambda s: jnp.where(s == 0, i, s - 1)


def _attn_specs(seq, colblks):
    nq = seq // TB
    specs = []
    for kind, cb in colblks:
        if kind == "q":
            specs.append(pl.BlockSpec((TB, GROUP_W), lambda b, i, cb=cb: (b * nq + i, cb)))
        else:
            specs.append(pl.BlockSpec((seq, GROUP_W), lambda b, i, cb=cb: (b, cb)))
    return specs


def _vt_spec(nkb, cb):
    return pl.BlockSpec((None, nkb, GROUP_HEADS * VROWS, TB), lambda b, i: (b, 0, cb - CB_VA, 0))


def _o_spec(seq):
    nq = seq // TB
    return pl.BlockSpec((TB, GROUP_W), lambda b, i: (b * nq + i, 0))


_ATTN_PARAMS = dict(
    compiler_params=pltpu.CompilerParams(
        dimension_semantics=("parallel", "arbitrary"), vmem_limit_bytes=VMEM_LIMIT))


BIAS_TERMS = 3


def _fox_kernel(q_ref, k_ref, cb_ref, vt_ref, o_ref, *scratch):
    i = pl.program_id(1)
    qs_ref = scratch[0]
    _stack_heads(q_ref, qs_ref)
    lane = lax.broadcasted_iota(jnp.int32, (TB, LANES), 1)
    for h in range(GROUP_HEADS):
        pick = jnp.logical_and(lane >= BIAS_TERMS * h, lane < BIAS_TERMS * (h + 1))
        qs_ref[h * TB:(h + 1) * TB, GROUP_W:GROUP_W + LANES] = jnp.where(pick, 1.0, 0.0).astype(BF16)

    def scores_of(j):
        return _pair_scores(k_ref, qs_ref, j, cb_ref)

    def step_args(s, diag):
        if diag:
            return dict(mask_fn=lambda st: jnp.where(_causal_mask(), st, NEG_INF))
        return {}

    _attend_sequence(scores_of, vt_ref, o_ref, scratch, i, _diagonal_first(i), step_args)


def _fox(qkv, cbias, vt, batch, seq):
    nkb = seq // TB
    return pl.pallas_call(
        _fox_kernel,
        grid=(batch, seq // TB),
        in_specs=_attn_specs(seq, (("q", CB_QA), ("k", CB_KA))) + [
            pl.BlockSpec((seq, LANES), lambda b, i: (b, 0)),
            _vt_spec(nkb, CB_VA),
        ],
        out_specs=_o_spec(seq),
        out_shape=jax.ShapeDtypeStruct((batch * seq, GROUP_W), BF16),
        scratch_shapes=_attn_scratch(GROUP_W + LANES),
        **_ATTN_PARAMS,
    )(qkv, qkv, cbias, vt)


HIGH_HALF = -(1 << 16)


def _key_to_float(key_u):
    key = key_u ^ jnp.int32(-2 ** 31)
    bits = jnp.where(key < 0, key ^ jnp.int32(0x7FFFFFFF), key)
    below_neg_inf = jnp.logical_and(key_u >= 0, key_u < jnp.int32(0x007FFFFF))
    return jnp.where(below_neg_inf, NEG_INF, lax.bitcast_convert_type(bits, F32))


def _dsa_kernel(q_ref, k_ref, vt_ref, qi_ref, k8_ref, w_ref, o_ref, strip_ref, strip_hi_ref, qm_ref, cnt_ref, *scratch,
                topk, idx_bits):
    i = pl.program_id(1)
    nblk = i + 1

    qi = qi_ref[...].astype(F32)
    low_lanes = lax.broadcasted_iota(jnp.int32, (1, LANES), 1) < IDX_DIM
    per_half = LANES // IDX_DIM
    for h in range(IDX_HEADS):
        half = qi[:, (h // per_half) * LANES:(h // per_half + 1) * LANES]
        if h % per_half:
            half = pltpu.roll(half, LANES - IDX_DIM * (h % per_half), 1)
        qm_ref[h] = jnp.where(low_lanes, half, 0.0).astype(BF16)

    wrow = w_ref[...]

    def score_block(j, diag):
        kblk = _kblock(k8_ref, j)[:, 0:LANES]
        sc = _tree_sum([wrow[h:h + 1, :] * jnp.maximum(_dot_nt(kblk, qm_ref[h]), 0.0)
                        for h in range(IDX_HEADS)])
        if diag:
            sc = jnp.where(_causal_mask(1), sc, NEG_INF)
        strip_ref[j] = sc
        strip_hi_ref[j] = lax.bitcast_convert_type(
            lax.bitcast_convert_type(sc, jnp.int32) & HIGH_HALF, F32).astype(BF16)

    def score_trip(t, carry):
        for u in range(8):
            score_block(8 * t + u, False)
        return carry

    lax.fori_loop(0, i >> 3, score_trip, 0)

    @pl.when((i & 4) != 0)
    def _():
        for u in range(4):
            score_block(((i >> 3) << 3) + u, False)

    @pl.when((i & 2) != 0)
    def _():
        score_block(((i >> 2) << 2), False)
        score_block(((i >> 2) << 2) + 1, False)

    @pl.when((i & 1) == 1)
    def _():
        score_block(i - 1, False)

    score_block(i, True)

    def count(pred):
        def body(j, acc):
            hit = jnp.where(pred(j, strip_ref[j]), 1.0, 0.0)
            return acc + _tree_sum([hit[r:r + 8, :] for r in range(0, TB, 8)])
        acc = lax.fori_loop(0, nblk, body, jnp.zeros((8, TB), F32))
        return jnp.sum(acc, axis=0, keepdims=True)

    def count_ge(ref, t, rows):
        one, zero = jnp.ones((), ref.dtype), jnp.zeros((), ref.dtype)

        def hits(j):
            hit = jnp.where(ref[j] >= t, one, zero)
            return _tree_sum([hit[r:r + rows, :] for r in range(0, TB, rows)]).astype(F32)

        def add(j0, n):
            cnt_ref[0:rows, :] += _tree_sum([hits(j0 + u) for u in range(n)])

        def trip(p, carry):
            add(4 * p, 4)
            return carry

        cnt_ref[0:rows, :] = jnp.zeros((rows, TB), F32)
        lax.fori_loop(0, nblk >> 2, trip, 0)

        @pl.when((nblk & 2) != 0)
        def _():
            add((nblk >> 2) << 2, 2)

        @pl.when((nblk & 1) == 1)
        def _():
            add(nblk - 1, 1)

        return jnp.sum(cnt_ref[0:rows, :], axis=0, keepdims=True)

    def radix_pass(it, state, high):
        prefix, cnt_at_prefix = state
        cand = prefix | jnp.left_shift(jnp.int32(1), 31 - it)
        t = _key_to_float(cand)
        if high:
            t = lax.bitcast_convert_type(lax.bitcast_convert_type(t, jnp.int32) & HIGH_HALF, F32)
            cnt = count_ge(strip_hi_ref, t.astype(BF16), BF16_ROWS)
        else:
            cnt = count_ge(strip_ref, t, 8)
        take = cnt >= topk
        return jnp.where(take, cand, prefix), jnp.where(take, cnt, cnt_at_prefix)

    n_all = jnp.full((1, TB), 1.0, F32) * (nblk * TB).astype(F32)
    state = (jnp.zeros((1, TB), jnp.int32), n_all)
    state = lax.fori_loop(0, 16, lambda it, s: radix_pass(it, s, True), state)
    prefix, cnt_ge = lax.fori_loop(16, 32, lambda it, s: radix_pass(it, s, False), state)
    thr = _key_to_float(prefix)

    @pl.when(jnp.max(cnt_ge) > topk)
    def _():
        need = topk - count(lambda j, blk: blk > thr)
        krow = lax.broadcasted_iota(jnp.int32, (TB, TB), 0)

        def idx_body(it, lim):
            cand = lim | jnp.left_shift(jnp.int32(1), idx_bits - 1 - it)
            cnt = count(lambda j, blk: jnp.logical_and(blk == thr, krow + j * TB < cand))
            return jnp.where(cnt <= need, cand, lim)

        lim = lax.fori_loop(0, idx_bits, idx_body, jnp.zeros((1, TB), jnp.int32))

        def drop_body(j, carry):
            blk = strip_ref[j]
            drop = jnp.logical_and(blk == thr, krow + j * TB >= lim)
            strip_ref[j] = jnp.where(drop, NEG_INF, blk)
            return carry

        lax.fori_loop(0, nblk, drop_body, 0)

    qs_ref = scratch[0]
    _stack_heads(q_ref, qs_ref)

    block_of = _diagonal_first(i)

    def step_args(s, diag):
        def mask_fn(st):
            blk = strip_ref[block_of(s)]
            sel = blk >= thr
            if diag:
                sel = jnp.logical_and(sel, blk > NEG_INF)
            return jnp.concatenate(
                [jnp.where(sel, _head(st, h), NEG_INF) for h in range(GROUP_HEADS)], axis=1)
        return dict(mask_fn=mask_fn)

    _attend_sequence(lambda j: _pair_scores(k_ref, qs_ref, j), vt_ref, o_ref, scratch,
                     i, block_of, step_args)


def _dsa(qkv, vt, wrow, batch, seq):
    nkb = seq // TB
    topk = min(DSA_TOPK, seq // 4)
    kern = functools.partial(_dsa_kernel, topk=float(topk), idx_bits=int(seq).bit_length())
    return pl.pallas_call(
        kern,
        grid=(batch, seq // TB),
        in_specs=_attn_specs(seq, (("q", CB_QB), ("k", CB_KB))) + [_vt_spec(nkb, CB_VB)]
        + _attn_specs(seq, (("q", CB_QI), ("k", CB_KI8)))
        + [pl.BlockSpec((None, IDX_HEADS, TB), lambda b, i: (b, 0, i))],
        out_specs=_o_spec(seq),
        out_shape=jax.ShapeDtypeStruct((batch * seq, GROUP_W), BF16),
        scratch_shapes=[pltpu.VMEM((nkb, TB, TB), F32), pltpu.VMEM((nkb, TB, TB), BF16),
                        pltpu.VMEM((IDX_HEADS, TB, LANES), BF16),
                        pltpu.VMEM((BF16_ROWS, TB), F32)] + _attn_scratch(),
        **_ATTN_PARAMS,
    )(qkv, qkv, vt, qkv, qkv, wrow)


DIL_SPAN = max(w for w, _ in DIL_PATTERNS) // TB + 1


def _dilated_multiplicity():
    k = np.arange(TB)[:, None]
    q = np.arange(TB)[None, :]
    out = np.zeros((DIL_SPAN, TB, TB), np.float32)
    for delta in range(DIL_SPAN):
        d = q - k + TB * delta
        for w, r in DIL_PATTERNS:
            out[delta] += ((d >= 0) & (d % r == 0) & (d <= w)).astype(np.float32)
    return out


def _dil_kernel(q_ref, k_ref, vt_ref, mult_ref, o_ref, *scratch):
    i = pl.program_id(1)
    qs_ref = scratch[0]
    _stack_heads(q_ref, qs_ref)
    n = jnp.minimum(i, DIL_SPAN - 1)

    def step_args(s, first):
        mult = _tile_heads(mult_ref[jnp.minimum(s, DIL_SPAN - 1)])
        return dict(mask_fn=lambda st: jnp.where(mult > 0.0, st, NEG_INF), weight=mult)

    _attend_sequence(lambda j: _pair_scores(k_ref, qs_ref, j), vt_ref, o_ref, scratch,
                     n, lambda s: jnp.maximum(i - s, 0), step_args)


def _dilated(qkv, vt, mult, batch, seq):
    nkb = seq // TB
    return pl.pallas_call(
        _dil_kernel,
        grid=(batch, seq // TB),
        in_specs=_attn_specs(seq, (("q", CB_QC), ("k", CB_KC))) + [
            _vt_spec(nkb, CB_VC),
            pl.BlockSpec((DIL_SPAN, TB, TB), lambda b, i: (0, 0, 0)),
        ],
        out_specs=_o_spec(seq),
        out_shape=jax.ShapeDtypeStruct((batch * seq, GROUP_W), BF16),
        scratch_shapes=_attn_scratch(),
        **_ATTN_PARAMS,
    )(qkv, qkv, vt, mult)


def _moba_kernel(q_ref, k_ref, vt_ref, o_ref, kmean_ref, sel_ref, *scratch, nkb, nkb_pad, topn):
    i = pl.program_id(1)

    @pl.when(i == 0)
    def _():
        kmean_ref[...] = jnp.zeros_like(kmean_ref)
        for n in range(nkb):
            kb = k_ref[n * TB:(n + 1) * TB, :].astype(F32)
            kmean_ref[n:n + 1, :] = jnp.mean(kb, axis=0, keepdims=True)

    qs_ref = scratch[0]
    _stack_heads(q_ref, qs_ref)
    blk_id = lax.broadcasted_iota(jnp.int32, (nkb_pad, HEADS_W), 0)
    blk_f = blk_id.astype(F32)
    g = _dot_nt(kmean_ref[...].astype(BF16), qs_ref[...])
    g = jnp.where(blk_id < i, g, NEG_INF)
    sel = jnp.zeros_like(g)
    for _ in range(topn):
        mx = jnp.max(g, axis=0, keepdims=True)
        is_max = jnp.logical_and(g == mx, mx > NEG_INF)
        first = jnp.min(jnp.where(is_max, blk_f, float(nkb_pad)), axis=0, keepdims=True)
        pick = blk_f == first
        sel = jnp.where(pick, 1.0, sel)
        g = jnp.where(pick, NEG_INF, g)
    sel_ref[...] = sel

    block_of = _diagonal_first(i)

    def step_args(s, diag):
        if diag:
            return dict(mask_fn=lambda st: jnp.where(_causal_mask(), st, NEG_INF))
        return dict(keep=sel_ref[pl.ds(block_of(s), 1), :])

    _attend_sequence(lambda j: _pair_scores(k_ref, qs_ref, j), vt_ref, o_ref, scratch,
                     i, block_of, step_args)


def _moba(qkv, vt, batch, seq):
    nkb = seq // TB
    nkb_pad = -(-nkb // 8) * 8
    kern = functools.partial(_moba_kernel, nkb=nkb, nkb_pad=nkb_pad, topn=min(MOBA_TOPK, nkb))
    return pl.pallas_call(
        kern,
        grid=(batch, seq // TB),
        in_specs=_attn_specs(seq, (("q", CB_QD), ("k", CB_KD))) + [_vt_spec(nkb, CB_VD)],
        out_specs=_o_spec(seq),
        out_shape=jax.ShapeDtypeStruct((batch * seq, GROUP_W), BF16),
        scratch_shapes=[pltpu.VMEM((nkb_pad, GROUP_W), F32),
                        pltpu.VMEM((nkb_pad, HEADS_W), F32)] + _attn_scratch(),
        compiler_params=pltpu.CompilerParams(
            dimension_semantics=("arbitrary", "arbitrary"), vmem_limit_bytes=VMEM_LIMIT),
    )(qkv, qkv, vt)


def _outproj_kernel(oa_ref, ob_ref, oc_ref, od_ref, w_ref, x_ref, g_ref, b_ref, y_ref, *, alpha):
    acc = alpha * x_ref[...]
    for gi, o_ref in enumerate((oa_ref, ob_ref, oc_ref, od_ref)):
        acc = acc + jnp.dot(o_ref[...], w_ref[gi * GROUP_W:(gi + 1) * GROUP_W, :],
                            preferred_element_type=F32)
    y_ref[...] = _layer_norm(acc, g_ref[...], b_ref[...])


def _outproj(os4, w_o, x2d, g, b, alpha):
    m, d = x2d.shape
    tm = min(1024, m)
    row = lambda i: (i, 0)
    fixed = lambda i: (0, 0)
    return pl.pallas_call(
        functools.partial(_outproj_kernel, alpha=alpha),
        grid=(m // tm,),
        in_specs=[pl.BlockSpec((tm, GROUP_W), row)] * 4 + [
            pl.BlockSpec((d, d), fixed), pl.BlockSpec((tm, d), row),
            pl.BlockSpec((1, d), fixed), pl.BlockSpec((1, d), fixed)],
        out_specs=pl.BlockSpec((tm, d), row),
        out_shape=jax.ShapeDtypeStruct((m, d), F32),
        compiler_params=pltpu.CompilerParams(
            dimension_semantics=("parallel",), vmem_limit_bytes=VMEM_LIMIT),
    )(*os4, w_o, x2d, g, b)


def _mlp_kernel(x_ref, wu_ref, wd_ref, g_ref, b_ref, y_ref, xb_ref, acc_ref, *, alpha):
    f = pl.program_id(1)

    @pl.when(f == 0)
    def _():
        x = x_ref[...]
        xb_ref[...] = x.astype(BF16)
        acc_ref[...] = alpha * x

    h = jnp.dot(xb_ref[...], wu_ref[...], preferred_element_type=F32)
    h = jnp.square(jnp.maximum(h, 0.0))
    acc_ref[...] += jnp.dot(h.astype(BF16), wd_ref[...], preferred_element_type=F32)

    @pl.when(f == pl.num_programs(1) - 1)
    def _():
        y_ref[...] = _layer_norm(acc_ref[...], g_ref[...], b_ref[...])


def _mlp(x2d, w_up, w_down, g, b, alpha):
    m, d = x2d.shape
    dff = w_up.shape[1]
    tm = min(1024, m)
    tf = min(1024, dff)
    return pl.pallas_call(
        functools.partial(_mlp_kernel, alpha=alpha),
        grid=(m // tm, dff // tf),
        in_specs=[
            pl.BlockSpec((tm, d), lambda i, f: (i, 0)),
            pl.BlockSpec((d, tf), lambda i, f: (0, f)),
            pl.BlockSpec((tf, d), lambda i, f: (f, 0)),
            pl.BlockSpec((1, d), lambda i, f: (0, 0)),
            pl.BlockSpec((1, d), lambda i, f: (0, 0)),
        ],
        out_specs=pl.BlockSpec((tm, d), lambda i, f: (i, 0)),
        out_shape=jax.ShapeDtypeStruct((m, d), F32),
        scratch_shapes=[pltpu.VMEM((tm, d), BF16), pltpu.VMEM((tm, d), F32)],
        compiler_params=pltpu.CompilerParams(
            dimension_semantics=("parallel", "arbitrary"), vmem_limit_bytes=VMEM_LIMIT),
    )(x2d, w_up, w_down, g, b)


def _column_plan():
    hq = GROUP_W
    sizes = (hq, hq, hq, GROUP_HEADS,
             hq, hq, hq, IDX_HEADS * IDX_DIM, IDX_DIM, IDX_HEADS,
             hq, hq, hq, hq, hq, hq)
    offs = np.concatenate([[0], np.cumsum(sizes)])
    (qa, ka, va, fa, qb, kb, vb, qi, ki, wi, qc, kc, vc, qd, kd, vd) = [
        np.arange(offs[n], offs[n + 1]) for n in range(len(sizes))]
    blocks = {CB_QA: qa, CB_KA: ka, CB_VA: va, CB_VB: vb, CB_VC: vc, CB_VD: vd,
              CB_QB: qb, CB_KB: kb, CB_QC: qc, CB_KC: kc, CB_QD: qd, CB_KD: kd,
              CB_QI: qi, CB_KI8: np.tile(ki, IDX_HEADS)}
    scale = np.ones(N_COLBLK * GROUP_W, np.float32)
    for cb in (CB_QA, CB_QB, CB_QC, CB_QD):
        scale[cb * GROUP_W:(cb + 1) * GROUP_W] = LOG2E * HEAD_DIM ** -0.5
    return np.concatenate([blocks[cb] for cb in range(N_COLBLK)]), scale, np.concatenate([fa, wi])


def _rope_table(seq, dim):
    half = dim // 2
    lane = np.arange(LANES)
    inv = ROPE_THETA ** (-(jnp.arange(half, dtype=F32)) / half)
    ang = jnp.arange(seq, dtype=F32)[:, None] * inv[None, :]
    idx = lane % half
    sign = np.where((lane % dim) < half, -1.0, 1.0).astype(np.float32)
    return jnp.concatenate([jnp.cos(ang)[:, idx], jnp.sin(ang)[:, idx] * sign[None, :]], axis=1)


def _vt_all(qkv, batch, seq):
    nkb = seq // TB
    heads = (CB_VD - CB_VA + 1) * GROUP_HEADS
    v = qkv[:, CB_VA * GROUP_W:(CB_VD + 1) * GROUP_W].reshape(batch, nkb, TB, heads, HEAD_DIM)
    pad = jnp.zeros((batch, nkb, TB, heads, VROWS - HEAD_DIM), v.dtype).at[..., 0].set(1)
    v = jnp.concatenate([v, pad], axis=-1).reshape(batch, nkb, TB, heads * VROWS)
    return v.transpose(0, 1, 3, 2)


def kernel(x, w_in, b_f, w_o, ln1_g, ln1_b, w_up, w_down, ln2_g, ln2_b):
    batch, seq, d = x.shape
    depth = w_in.shape[0]
    assert seq % TB == 0 and d == 4 * GROUP_W
    chunks = seq // LANES
    assert chunks & (chunks - 1) == 0
    alpha = (2.0 * depth) ** 0.25

    main_cols, main_scale, misc_cols = _column_plan()
    w_main = (w_in[:, :, main_cols] * main_scale).astype(BF16)
    w_misc = jnp.pad(w_in[:, :, misc_cols], ((0, 0), (0, 0), (0, LANES - len(misc_cols)))).astype(BF16)
    w_o16, w_up16, w_down16 = w_o.astype(BF16), w_up.astype(BF16), w_down.astype(BF16)
    tab64 = _rope_table(seq, HEAD_DIM)
    tab32 = _rope_table(seq, IDX_DIM)
    mult = jnp.asarray(_dilated_multiplicity())

    x2d = x.reshape(batch * seq, d)
    for l in range(depth):
        qkv, misc = _inproj(x2d, w_main[l], w_misc[l], tab64, tab32, seq)
        misc3 = misc.reshape(batch, seq, LANES)
        z = misc3[:, :, 0:GROUP_HEADS].transpose(0, 2, 1).reshape(batch, GROUP_HEADS * chunks, LANES)
        bias_rows = jnp.repeat(b_f[l].astype(F32), chunks)[:, None]
        gb = _gate_bias(z, bias_rows, chunks)
        gb = gb.reshape(batch, BIAS_TERMS, GROUP_HEADS, seq).transpose(0, 3, 2, 1)
        cbias = jnp.pad(gb.reshape(batch * seq, GROUP_HEADS * BIAS_TERMS),
                        ((0, 0), (0, LANES - GROUP_HEADS * BIAS_TERMS))).astype(BF16)
        wrow = misc3[:, :, GROUP_HEADS:GROUP_HEADS + IDX_HEADS].transpose(0, 2, 1)

        vt = _vt_all(qkv, batch, seq)
        oa = _fox(qkv, cbias, vt, batch, seq)
        ob = _dsa(qkv, vt, wrow, batch, seq)
        oc = _dilated(qkv, vt, mult, batch, seq)
        od = _moba(qkv, vt, batch, seq)

        x2d = _outproj((oa, ob, oc, od), w_o16[l], x2d, ln1_g[l][None, :], ln1_b[l][None, :], alpha)
        x2d = _mlp(x2d, w_up16[l], w_down16[l], ln2_g[l][None, :], ln2_b[l][None, :], alpha)
    return x2d.reshape(batch, seq, d)
```

```python
import functools
import math

import numpy as np
import jax
import jax.numpy as jnp
from jax import lax
from jax.experimental import pallas as pl
from jax.experimental.pallas import tpu as pltpu

F32 = jnp.float32
BF16 = jnp.bfloat16

HEAD_DIM = 64
GROUP_HEADS = 4
GROUP_W = GROUP_HEADS * HEAD_DIM
IDX_HEADS = 8
IDX_DIM = 32
DSA_TOPK = 256
DIL_PATTERNS = ((128, 1), (512, 4), (2048, 16))
MOBA_BLOCK = 256
MOBA_TOPK = 3
ROPE_THETA = 10000.0
LN_EPS = 1e-5
LOG2E = math.log2(math.e)
LANES = 128
BF16_ROWS = 16
TB = 256
VROWS = HEAD_DIM + BF16_ROWS
NEG_INF = float("-inf")
VMEM_LIMIT = 56 * 1024 * 1024

(CB_QA, CB_KA, CB_VA, CB_VB, CB_VC, CB_VD,
 CB_QB, CB_KB, CB_QC, CB_KC, CB_QD, CB_KD,
 CB_QI, CB_KI8) = range(14)
N_COLBLK = 14
PROJ_TN = 2 * GROUP_W
PLAIN_STEPS, ROPE64_STEPS = 3, 3


def _dot_nt(a, b):
    return lax.dot_general(a, b, (((1,), (1,)), ((), ())), preferred_element_type=F32)


def _layer_norm(y, g, b):
    mu = jnp.mean(y, axis=-1, keepdims=True)
    d = y - mu
    var = jnp.mean(d * d, axis=-1, keepdims=True)
    return d * lax.rsqrt(var + LN_EPS) * g + b


def _tree_sum(parts):
    while len(parts) > 1:
        parts = [parts[n] + parts[n + 1] for n in range(0, len(parts), 2)]
    return parts[0]


def _rope(acc, tab_ref, half):
    cos = tab_ref[:, 0:LANES]
    sin = tab_ref[:, LANES:2 * LANES]
    lane = lax.broadcasted_iota(jnp.int32, (1, LANES), 1)
    first = (lane & (2 * half - 1)) < half
    outs = []
    for c in range(acc.shape[1] // LANES):
        a = acc[:, c * LANES:(c + 1) * LANES]
        swapped = jnp.where(first, pltpu.roll(a, LANES - half, 1), pltpu.roll(a, half, 1))
        outs.append(a * cos + swapped * sin)
    return jnp.concatenate(outs, axis=1)


def _inproj_kernel(x_ref, w_ref, wm_ref, t64_ref, t32_ref, o_ref, misc_ref, xb_ref):
    j = pl.program_id(1)

    @pl.when(j == 0)
    def _():
        xb = x_ref[...].astype(BF16)
        xb_ref[...] = xb
        misc_ref[...] = jnp.dot(xb, wm_ref[...], preferred_element_type=F32)

    acc = jnp.dot(xb_ref[...], w_ref[...], preferred_element_type=F32)

    @pl.when(j < PLAIN_STEPS)
    def _():
        o_ref[...] = acc.astype(BF16)

    @pl.when(jnp.logical_and(j >= PLAIN_STEPS, j < PLAIN_STEPS + ROPE64_STEPS))
    def _():
        o_ref[...] = _rope(acc, t64_ref, HEAD_DIM // 2).astype(BF16)

    @pl.when(j == PLAIN_STEPS + ROPE64_STEPS)
    def _():
        o_ref[...] = _rope(acc, t32_ref, IDX_DIM // 2).astype(BF16)


def _inproj(x2d, w_main, w_misc, tab64, tab32, seq):
    m, d = x2d.shape
    tm = min(1024, seq)
    nt = seq // tm
    return pl.pallas_call(
        _inproj_kernel,
        grid=(m // tm, N_COLBLK * GROUP_W // PROJ_TN),
        in_specs=[
            pl.BlockSpec((tm, d), lambda i, j: (i, 0)),
            pl.BlockSpec((d, PROJ_TN), lambda i, j: (0, j)),
            pl.BlockSpec((d, LANES), lambda i, j: (0, 0)),
            pl.BlockSpec((tm, 2 * LANES), lambda i, j: (i % nt, 0)),
            pl.BlockSpec((tm, 2 * LANES), lambda i, j: (i % nt, 0)),
        ],
        out_specs=[
            pl.BlockSpec((tm, PROJ_TN), lambda i, j: (i, j)),
            pl.BlockSpec((tm, LANES), lambda i, j: (i, 0)),
        ],
        out_shape=[
            jax.ShapeDtypeStruct((m, N_COLBLK * GROUP_W), BF16),
            jax.ShapeDtypeStruct((m, LANES), F32),
        ],
        scratch_shapes=[pltpu.VMEM((tm, d), BF16)],
        compiler_params=pltpu.CompilerParams(
            dimension_semantics=("parallel", "arbitrary"), vmem_limit_bytes=VMEM_LIMIT),
    )(x2d, w_main, w_misc, tab64, tab32)


def _gate_kernel(z_ref, b_ref, o_ref, *, chunks):
    z = z_ref[...] + b_ref[...]
    w = jnp.minimum(z, 0.0) - jnp.log1p(jnp.exp(-jnp.abs(z)))
    rows = w.shape[0]
    lane = lax.broadcasted_iota(jnp.int32, (rows, LANES), 1)
    s = 1
    while s < LANES:
        w = w + jnp.where(lane >= s, pltpu.roll(w, s, 1), 0.0)
        s *= 2
    tot = jnp.broadcast_to(w[:, LANES - 1:LANES], (rows, LANES))
    pos = lax.broadcasted_iota(jnp.int32, (rows, LANES), 0) & (chunks - 1)
    run = tot
    s = 1
    while s < chunks:
        run = run + jnp.where(pos >= s, pltpu.roll(run, s, 0), 0.0)
        s *= 2
    bias = -LOG2E * (w + (run - tot))
    hi = bias.astype(BF16).astype(F32)
    mid = (bias - hi).astype(BF16).astype(F32)
    lo = (bias - hi - mid).astype(BF16).astype(F32)
    o_ref[0] = hi
    o_ref[1] = mid
    o_ref[2] = lo


def _gate_bias(z, bias_rows, chunks):
    b, rows, _ = z.shape
    return pl.pallas_call(
        functools.partial(_gate_kernel, chunks=chunks),
        grid=(b,),
        in_specs=[pl.BlockSpec((None, rows, LANES), lambda i: (i, 0, 0)),
                  pl.BlockSpec((rows, 1), lambda i: (0, 0))],
        out_specs=pl.BlockSpec((None, 3, rows, LANES), lambda i: (i, 0, 0, 0)),
        out_shape=jax.ShapeDtypeStruct((b, 3, rows, LANES), F32),
        compiler_params=pltpu.CompilerParams(dimension_semantics=("parallel",)),
    )(z, bias_rows)


HEADS_W = GROUP_HEADS * TB


def _stack_heads(q_ref, qs_ref):
    q = q_ref[...].astype(F32)
    head_of_lane = lax.broadcasted_iota(jnp.int32, (1, GROUP_W), 1) >> 6
    for h in range(GROUP_HEADS):
        qs_ref[h * TB:(h + 1) * TB, 0:GROUP_W] = jnp.where(head_of_lane == h, q, 0.0).astype(BF16)


def _head(x, h):
    return x[:, h * TB:(h + 1) * TB]


def _tile_heads(x):
    return jnp.concatenate([x] * GROUP_HEADS, axis=1)


def _kblock(k_ref, j):
    return k_ref[pl.ds(pl.multiple_of(j * TB, TB), TB), :]


def _pair_scores(k_ref, qs_ref, j, extra_ref=None):
    kblk = _kblock(k_ref, j)
    outs = []
    for pr in range(GROUP_W // LANES):
        rows = slice(2 * pr * TB, 2 * (pr + 1) * TB)
        kp = kblk[:, pr * LANES:(pr + 1) * LANES]
        qp = qs_ref[rows, pr * LANES:(pr + 1) * LANES]
        if extra_ref is not None:
            kp = jnp.concatenate([kp, _kblock(extra_ref, j)], axis=1)
            qp = jnp.concatenate([qp, qs_ref[rows, GROUP_W:GROUP_W + LANES]], axis=1)
        outs.append(_dot_nt(kp, qp))
    return jnp.concatenate(outs, axis=1)


def _causal_mask(heads=GROUP_HEADS):
    krow = lax.broadcasted_iota(jnp.int32, (TB, heads * TB), 0)
    qcol = lax.broadcasted_iota(jnp.int32, (TB, heads * TB), 1) & (TB - 1)
    return krow <= qcol


def _consume(st, blk_max, vt, m, accs, weight=None, keep=None):
    m_new = jnp.maximum(m, blk_max)
    if keep is not None:
        m_new = jnp.where(keep > 0.5, m_new, m)
    m_safe = jnp.where(m_new == NEG_INF, 0.0, m_new)
    alpha = jnp.exp2(m - m_safe)
    p = jnp.exp2(st - m_safe)
    if weight is not None:
        p = p * weight
    pb = p.astype(BF16)
    new_accs = []
    for h in range(GROUP_HEADS):
        pv = jnp.dot(vt[h * VROWS:(h + 1) * VROWS, :], _head(pb, h), preferred_element_type=F32)
        upd = _head(alpha, h) * accs[h] + pv
        new_accs.append(upd if keep is None else jnp.where(_head(keep, h) > 0.5, upd, accs[h]))
    return m_new, new_accs


TRIP_SHIFT = 3
STEPS_PER_TRIP = 1 << TRIP_SHIFT


def _attn_scratch(qs_width=GROUP_W):
    return [
        pltpu.VMEM((HEADS_W, qs_width), BF16),
        pltpu.VMEM((2, TB, HEADS_W), F32),
        pltpu.VMEM((2, 1, HEADS_W), F32),
        pltpu.VMEM((1, HEADS_W), F32),
        pltpu.VMEM((GROUP_HEADS * VROWS, TB), F32),
    ]


def _attend_sequence(scores_of, vt_ref, o_ref, scratch, n_steps, block_of, step_args):
    _, st_ref, bm_ref, m_ref, acc_ref = scratch

    def issue(s, slot, first=False):
        st = scores_of(block_of(s))
        mask_fn = step_args(s, first).get("mask_fn")
        if mask_fn is not None:
            st = mask_fn(st)
        st_ref[slot] = st
        bm_ref[slot] = jnp.max(st, axis=0, keepdims=True)

    def consume(s, slot, first=False):
        args = step_args(s, first)
        accs = [acc_ref[h * VROWS:(h + 1) * VROWS, :] for h in range(GROUP_HEADS)]
        m, accs = _consume(st_ref[slot], bm_ref[slot], vt_ref[block_of(s)], m_ref[...], accs,
                           weight=args.get("weight"), keep=args.get("keep"))
        m_ref[...] = m
        for h in range(GROUP_HEADS):
            acc_ref[h * VROWS:(h + 1) * VROWS, :] = accs[h]

    m_ref[...] = jnp.full((1, HEADS_W), NEG_INF, F32)
    acc_ref[...] = jnp.zeros_like(acc_ref)
    issue(0, 0, True)
    issue(1, 1)
    consume(0, 0, True)

    def run(s, count):
        for u in range(count):
            issue(s + u + 1, u & 1)
            consume(s + u, (u + 1) & 1)

    def trip(tt, carry):
        run(1 + STEPS_PER_TRIP * tt, STEPS_PER_TRIP)
        return carry

    n_trips = n_steps >> TRIP_SHIFT
    lax.fori_loop(0, n_trips, trip, 0)

    for bit in range(TRIP_SHIFT - 1, 0, -1):
        size = 1 << bit

        @pl.when((n_steps & size) != 0)
        def _(size=size):
            run(1 + ((n_steps >> TRIP_SHIFT) << TRIP_SHIFT) + (n_steps & (STEPS_PER_TRIP - 2 * size)), size)

    @pl.when((n_steps & 1) == 1)
    def _():
        consume(n_steps, 1)

    ot = jnp.concatenate(
        [acc_ref[h * VROWS:h * VROWS + HEAD_DIM, :] / acc_ref[h * VROWS + HEAD_DIM:h * VROWS + HEAD_DIM + 1, :]
         for h in range(GROUP_HEADS)], axis=0)
    o_ref[...] = ot.T.astype(o_ref.dtype)


def _diagonal_first(i):
    return lambda s: jnp.where(s == 0, i, s - 1)


def _attn_specs(seq, colblks):
    nq = seq // TB
    specs = []
    for kind, cb in colblks:
        if kind == "q":
            specs.append(pl.BlockSpec((TB, GROUP_W), lambda b, i, cb=cb: (b * nq + i, cb)))
        else:
            specs.append(pl.BlockSpec((seq, GROUP_W), lambda b, i, cb=cb: (b, cb)))
    return specs


def _vt_spec(nkb, cb):
    return pl.BlockSpec((None, nkb, GROUP_HEADS * VROWS, TB), lambda b, i: (b, 0, cb - CB_VA, 0))


def _o_spec(seq):
    nq = seq // TB
    return pl.BlockSpec((TB, GROUP_W), lambda b, i: (b * nq + i, 0))


_ATTN_PARAMS = dict(
    compiler_params=pltpu.CompilerParams(
        dimension_semantics=("parallel", "arbitrary"), vmem_limit_bytes=VMEM_LIMIT))


BIAS_TERMS = 3


def _fox_kernel(q_ref, k_ref, cb_ref, vt_ref, o_ref, *scratch):
    i = pl.program_id(1)
    qs_ref = scratch[0]
    _stack_heads(q_ref, qs_ref)
    lane = lax.broadcasted_iota(jnp.int32, (TB, LANES), 1)
    for h in range(GROUP_HEADS):
        pick = jnp.logical_and(lane >= BIAS_TERMS * h, lane < BIAS_TERMS * (h + 1))
        qs_ref[h * TB:(h + 1) * TB, GROUP_W:GROUP_W + LANES] = jnp.where(pick, 1.0, 0.0).astype(BF16)

    def scores_of(j):
        return _pair_scores(k_ref, qs_ref, j, cb_ref)

    def step_args(s, diag):
        if diag:
            return dict(mask_fn=lambda st: jnp.where(_causal_mask(), st, NEG_INF))
        return {}

    _attend_sequence(scores_of, vt_ref, o_ref, scratch, i, _diagonal_first(i), step_args)


def _fox(qkv, cbias, vt, batch, seq):
    nkb = seq // TB
    return pl.pallas_call(
        _fox_kernel,
        grid=(batch, seq // TB),
        in_specs=_attn_specs(seq, (("q", CB_QA), ("k", CB_KA))) + [
            pl.BlockSpec((seq, LANES), lambda b, i: (b, 0)),
            _vt_spec(nkb, CB_VA),
        ],
        out_specs=_o_spec(seq),
        out_shape=jax.ShapeDtypeStruct((batch * seq, GROUP_W), BF16),
        scratch_shapes=_attn_scratch(GROUP_W + LANES),
        **_ATTN_PARAMS,
    )(qkv, qkv, cbias, vt)


HIGH_HALF = -(1 << 16)


def _key_to_float(key_u):
    key = key_u ^ jnp.int32(-2 ** 31)
    bits = jnp.where(key < 0, key ^ jnp.int32(0x7FFFFFFF), key)
    below_neg_inf = jnp.logical_and(key_u >= 0, key_u < jnp.int32(0x007FFFFF))
    return jnp.where(below_neg_inf, NEG_INF, lax.bitcast_convert_type(bits, F32))


def _dsa_kernel(q_ref, k_ref, vt_ref, qi_ref, k8_ref, w_ref, o_ref, strip_ref, strip_hi_ref, qm_ref, cnt_ref, *scratch,
                topk, idx_bits):
    i = pl.program_id(1)
    nblk = i + 1

    qi = qi_ref[...].astype(F32)
    low_lanes = lax.broadcasted_iota(jnp.int32, (1, LANES), 1) < IDX_DIM
    per_half = LANES // IDX_DIM
    for h in range(IDX_HEADS):
        half = qi[:, (h // per_half) * LANES:(h // per_half + 1) * LANES]
        if h % per_half:
            half = pltpu.roll(half, LANES - IDX_DIM * (h % per_half), 1)
        qm_ref[h] = jnp.where(low_lanes, half, 0.0).astype(BF16)

    wrow = w_ref[...]

    def score_block(j, diag):
        kblk = _kblock(k8_ref, j)[:, 0:LANES]
        sc = _tree_sum([wrow[h:h + 1, :] * jnp.maximum(_dot_nt(kblk, qm_ref[h]), 0.0)
                        for h in range(IDX_HEADS)])
        if diag:
            sc = jnp.where(_causal_mask(1), sc, NEG_INF)
        strip_ref[j] = sc
        strip_hi_ref[j] = lax.bitcast_convert_type(
            lax.bitcast_convert_type(sc, jnp.int32) & HIGH_HALF, F32).astype(BF16)

    def score_trip(t, carry):
        for u in range(8):
            score_block(8 * t + u, False)
        return carry

    lax.fori_loop(0, i >> 3, score_trip, 0)

    @pl.when((i & 4) != 0)
    def _():
        for u in range(4):
            score_block(((i >> 3) << 3) + u, False)

    @pl.when((i & 2) != 0)
    def _():
        score_block(((i >> 2) << 2), False)
        score_block(((i >> 2) << 2) + 1, False)

    @pl.when((i & 1) == 1)
    def _():
        score_block(i - 1, False)

    score_block(i, True)

    def count(pred):
        def body(j, acc):
            hit = jnp.where(pred(j, strip_ref[j]), 1.0, 0.0)
            return acc + _tree_sum([hit[r:r + 8, :] for r in range(0, TB, 8)])
        acc = lax.fori_loop(0, nblk, body, jnp.zeros((8, TB), F32))
        return jnp.sum(acc, axis=0, keepdims=True)

    def count_ge(ref, t, rows):
        one, zero = jnp.ones((), ref.dtype), jnp.zeros((), ref.dtype)

        def hits(j):
            hit = jnp.where(ref[j] >= t, one, zero)
            return _tree_sum([hit[r:r + rows, :] for r in range(0, TB, rows)]).astype(F32)

        def add(j0, n):
            cnt_ref[0:rows, :] += _tree_sum([hits(j0 + u) for u in range(n)])

        def trip(p, carry):
            add(4 * p, 4)
            return carry

        cnt_ref[0:rows, :] = jnp.zeros((rows, TB), F32)
        lax.fori_loop(0, nblk >> 2, trip, 0)

        @pl.when((nblk & 2) != 0)
        def _():
            add((nblk >> 2) << 2, 2)

        @pl.when((nblk & 1) == 1)
        def _():
            add(nblk - 1, 1)

        return jnp.sum(cnt_ref[0:rows, :], axis=0, keepdims=True)

    def radix_pass(it, state, high):
        prefix, cnt_at_prefix = state
        cand = prefix | jnp.left_shift(jnp.int32(1), 31 - it)
        t = _key_to_float(cand)
        if high:
            t = lax.bitcast_convert_type(lax.bitcast_convert_type(t, jnp.int32) & HIGH_HALF, F32)
            cnt = count_ge(strip_hi_ref, t.astype(BF16), BF16_ROWS)
        else:
            cnt = count_ge(strip_ref, t, 8)
        take = cnt >= topk
        return jnp.where(take, cand, prefix), jnp.where(take, cnt, cnt_at_prefix)

    n_all = jnp.full((1, TB), 1.0, F32) * (nblk * TB).astype(F32)
    state = (jnp.zeros((1, TB), jnp.int32), n_all)
    state = lax.fori_loop(0, 16, lambda it, s: radix_pass(it, s, True), state)
    prefix, cnt_ge = lax.fori_loop(16, 32, lambda it, s: radix_pass(it, s, False), state)
    thr = _key_to_float(prefix)

    @pl.when(jnp.max(cnt_ge) > topk)
    def _():
        need = topk - count(lambda j, blk: blk > thr)
        krow = lax.broadcasted_iota(jnp.int32, (TB, TB), 0)

        def idx_body(it, lim):
            cand = lim | jnp.left_shift(jnp.int32(1), idx_bits - 1 - it)
            cnt = count(lambda j, blk: jnp.logical_and(blk == thr, krow + j * TB < cand))
            return jnp.where(cnt <= need, cand, lim)

        lim = lax.fori_loop(0, idx_bits, idx_body, jnp.zeros((1, TB), jnp.int32))

        def drop_body(j, carry):
            blk = strip_ref[j]
            drop = jnp.logical_and(blk == thr, krow + j * TB >= lim)
            strip_ref[j] = jnp.where(drop, NEG_INF, blk)
            return carry

        lax.fori_loop(0, nblk, drop_body, 0)

    qs_ref = scratch[0]
    _stack_heads(q_ref, qs_ref)

    block_of = _diagonal_first(i)

    def step_args(s, diag):
        def mask_fn(st):
            blk = strip_ref[block_of(s)]
            sel = blk >= thr
            if diag:
                sel = jnp.logical_and(sel, blk > NEG_INF)
            return jnp.concatenate(
                [jnp.where(sel, _head(st, h), NEG_INF) for h in range(GROUP_HEADS)], axis=1)
        return dict(mask_fn=mask_fn)

    _attend_sequence(lambda j: _pair_scores(k_ref, qs_ref, j), vt_ref, o_ref, scratch,
                     i, block_of, step_args)


def _dsa(qkv, vt, wrow, batch, seq):
    nkb = seq // TB
    topk = min(DSA_TOPK, seq // 4)
    kern = functools.partial(_dsa_kernel, topk=float(topk), idx_bits=int(seq).bit_length())
    return pl.pallas_call(
        kern,
        grid=(batch, seq // TB),
        in_specs=_attn_specs(seq, (("q", CB_QB), ("k", CB_KB))) + [_vt_spec(nkb, CB_VB)]
        + _attn_specs(seq, (("q", CB_QI), ("k", CB_KI8)))
        + [pl.BlockSpec((None, IDX_HEADS, TB), lambda b, i: (b, 0, i))],
        out_specs=_o_spec(seq),
        out_shape=jax.ShapeDtypeStruct((batch * seq, GROUP_W), BF16),
        scratch_shapes=[pltpu.VMEM((nkb, TB, TB), F32), pltpu.VMEM((nkb, TB, TB), BF16),
                        pltpu.VMEM((IDX_HEADS, TB, LANES), BF16),
                        pltpu.VMEM((BF16_ROWS, TB), F32)] + _attn_scratch(),
        **_ATTN_PARAMS,
    )(qkv, qkv, vt, qkv, qkv, wrow)


DIL_SPAN = max(w for w, _ in DIL_PATTERNS) // TB + 1


def _dilated_multiplicity():
    k = np.arange(TB)[:, None]
    q = np.arange(TB)[None, :]
    out = np.zeros((DIL_SPAN, TB, TB), np.float32)
    for delta in range(DIL_SPAN):
        d = q - k + TB * delta
        for w, r in DIL_PATTERNS:
            out[delta] += ((d >= 0) & (d % r == 0) & (d <= w)).astype(np.float32)
    return out


def _dil_kernel(q_ref, k_ref, vt_ref, mult_ref, o_ref, *scratch):
    i = pl.program_id(1)
    qs_ref = scratch[0]
    _stack_heads(q_ref, qs_ref)
    n = jnp.minimum(i, DIL_SPAN - 1)

    def step_args(s, first):
        mult = _tile_heads(mult_ref[jnp.minimum(s, DIL_SPAN - 1)])
        return dict(mask_fn=lambda st: jnp.where(mult > 0.0, st, NEG_INF), weight=mult)

    _attend_sequence(lambda j: _pair_scores(k_ref, qs_ref, j), vt_ref, o_ref, scratch,
                     n, lambda s: jnp.maximum(i - s, 0), step_args)


def _dilated(qkv, vt, mult, batch, seq):
    nkb = seq // TB
    return pl.pallas_call(
        _dil_kernel,
        grid=(batch, seq // TB),
        in_specs=_attn_specs(seq, (("q", CB_QC), ("k", CB_KC))) + [
            _vt_spec(nkb, CB_VC),
            pl.BlockSpec((DIL_SPAN, TB, TB), lambda b, i: (0, 0, 0)),
        ],
        out_specs=_o_spec(seq),
        out_shape=jax.ShapeDtypeStruct((batch * seq, GROUP_W), BF16),
        scratch_shapes=_attn_scratch(),
        **_ATTN_PARAMS,
    )(qkv, qkv, vt, mult)


def _moba_kernel(q_ref, k_ref, vt_ref, o_ref, kmean_ref, sel_ref, *scratch, nkb, nkb_pad, topn):
    i = pl.program_id(1)

    @pl.when(i == 0)
    def _():
        kmean_ref[...] = jnp.zeros_like(kmean_ref)
        for n in range(nkb):
            kb = k_ref[n * TB:(n + 1) * TB, :].astype(F32)
            kmean_ref[n:n + 1, :] = jnp.mean(kb, axis=0, keepdims=True)

    qs_ref = scratch[0]
    _stack_heads(q_ref, qs_ref)
    blk_id = lax.broadcasted_iota(jnp.int32, (nkb_pad, HEADS_W), 0)
    blk_f = blk_id.astype(F32)
    g = _dot_nt(kmean_ref[...].astype(BF16), qs_ref[...])
    g = jnp.where(blk_id < i, g, NEG_INF)
    sel = jnp.zeros_like(g)
    for _ in range(topn):
        mx = jnp.max(g, axis=0, keepdims=True)
        is_max = jnp.logical_and(g == mx, mx > NEG_INF)
        first = jnp.min(jnp.where(is_max, blk_f, float(nkb_pad)), axis=0, keepdims=True)
        pick = blk_f == first
        sel = jnp.where(pick, 1.0, sel)
        g = jnp.where(pick, NEG_INF, g)
    sel_ref[...] = sel

    block_of = _diagonal_first(i)

    def step_args(s, diag):
        if diag:
            return dict(mask_fn=lambda st: jnp.where(_causal_mask(), st, NEG_INF))
        return dict(keep=sel_ref[pl.ds(block_of(s), 1), :])

    _attend_sequence(lambda j: _pair_scores(k_ref, qs_ref, j), vt_ref, o_ref, scratch,
                     i, block_of, step_args)


def _moba(qkv, vt, batch, seq):
    nkb = seq // TB
    nkb_pad = -(-nkb // 8) * 8
    kern = functools.partial(_moba_kernel, nkb=nkb, nkb_pad=nkb_pad, topn=min(MOBA_TOPK, nkb))
    return pl.pallas_call(
        kern,
        grid=(batch, seq // TB),
        in_specs=_attn_specs(seq, (("q", CB_QD), ("k", CB_KD))) + [_vt_spec(nkb, CB_VD)],
        out_specs=_o_spec(seq),
        out_shape=jax.ShapeDtypeStruct((batch * seq, GROUP_W), BF16),
        scratch_shapes=[pltpu.VMEM((nkb_pad, GROUP_W), F32),
                        pltpu.VMEM((nkb_pad, HEADS_W), F32)] + _attn_scratch(),
        compiler_params=pltpu.CompilerParams(
            dimension_semantics=("arbitrary", "arbitrary"), vmem_limit_bytes=VMEM_LIMIT),
    )(qkv, qkv, vt)


def _block_tail_kernel(oa_ref, ob_ref, oc_ref, od_ref, wo_ref, x_ref, g1_ref, b1_ref,
                       wu_ref, wd_ref, g2_ref, b2_ref, y_ref, xb_ref, acc_ref, *, alpha):
    f = pl.program_id(1)

    @pl.when(f == 0)
    def _():
        acc = alpha * x_ref[...]
        for gi, o_ref in enumerate((oa_ref, ob_ref, oc_ref, od_ref)):
            acc = acc + jnp.dot(o_ref[...], wo_ref[gi * GROUP_W:(gi + 1) * GROUP_W, :],
                                preferred_element_type=F32)
        x1 = _layer_norm(acc, g1_ref[...], b1_ref[...])
        xb_ref[...] = x1.astype(BF16)
        acc_ref[...] = alpha * x1

    h = jnp.dot(xb_ref[...], wu_ref[...], preferred_element_type=F32)
    h = jnp.square(jnp.maximum(h, 0.0))
    acc_ref[...] += jnp.dot(h.astype(BF16), wd_ref[...], preferred_element_type=F32)

    @pl.when(f == pl.num_programs(1) - 1)
    def _():
        y_ref[...] = _layer_norm(acc_ref[...], g2_ref[...], b2_ref[...])


def _block_tail(os4, w_o, x2d, g1, b1, w_up, w_down, g2, b2, alpha):
    m, d = x2d.shape
    dff = w_up.shape[1]
    tm = min(1024, m)
    tf = min(1024, dff)
    row = lambda i, f: (i, 0)
    fixed = lambda i, f: (0, 0)
    vec = pl.BlockSpec((1, d), fixed)
    return pl.pallas_call(
        functools.partial(_block_tail_kernel, alpha=alpha),
        grid=(m // tm, dff // tf),
        in_specs=[pl.BlockSpec((tm, GROUP_W), row)] * 4 + [
            pl.BlockSpec((d, d), fixed), pl.BlockSpec((tm, d), row), vec, vec,
            pl.BlockSpec((d, tf), lambda i, f: (0, f)),
            pl.BlockSpec((tf, d), lambda i, f: (f, 0)), vec, vec],
        out_specs=pl.BlockSpec((tm, d), row),
        out_shape=jax.ShapeDtypeStruct((m, d), F32),
        scratch_shapes=[pltpu.VMEM((tm, d), BF16), pltpu.VMEM((tm, d), F32)],
        compiler_params=pltpu.CompilerParams(
            dimension_semantics=("parallel", "arbitrary"), vmem_limit_bytes=VMEM_LIMIT),
    )(*os4, w_o, x2d, g1, b1, w_up, w_down, g2, b2)


def _column_plan():
    hq = GROUP_W
    sizes = (hq, hq, hq, GROUP_HEADS,
             hq, hq, hq, IDX_HEADS * IDX_DIM, IDX_DIM, IDX_HEADS,
             hq, hq, hq, hq, hq, hq)
    offs = np.concatenate([[0], np.cumsum(sizes)])
    (qa, ka, va, fa, qb, kb, vb, qi, ki, wi, qc, kc, vc, qd, kd, vd) = [
        np.arange(offs[n], offs[n + 1]) for n in range(len(sizes))]
    blocks = {CB_QA: qa, CB_KA: ka, CB_VA: va, CB_VB: vb, CB_VC: vc, CB_VD: vd,
              CB_QB: qb, CB_KB: kb, CB_QC: qc, CB_KC: kc, CB_QD: qd, CB_KD: kd,
              CB_QI: qi, CB_KI8: np.tile(ki, IDX_HEADS)}
    scale = np.ones(N_COLBLK * GROUP_W, np.float32)
    for cb in (CB_QA, CB_QB, CB_QC, CB_QD):
        scale[cb * GROUP_W:(cb + 1) * GROUP_W] = LOG2E * HEAD_DIM ** -0.5
    return np.concatenate([blocks[cb] for cb in range(N_COLBLK)]), scale, np.concatenate([fa, wi])


def _rope_table(seq, dim):
    half = dim // 2
    lane = np.arange(LANES)
    inv = ROPE_THETA ** (-(jnp.arange(half, dtype=F32)) / half)
    ang = jnp.arange(seq, dtype=F32)[:, None] * inv[None, :]
    idx = lane % half
    sign = np.where((lane % dim) < half, -1.0, 1.0).astype(np.float32)
    return jnp.concatenate([jnp.cos(ang)[:, idx], jnp.sin(ang)[:, idx] * sign[None, :]], axis=1)


def _vt_all(qkv, batch, seq):
    nkb = seq // TB
    heads = (CB_VD - CB_VA + 1) * GROUP_HEADS
    v = qkv[:, CB_VA * GROUP_W:(CB_VD + 1) * GROUP_W].reshape(batch, nkb, TB, heads, HEAD_DIM)
    pad = jnp.zeros((batch, nkb, TB, heads, VROWS - HEAD_DIM), v.dtype).at[..., 0].set(1)
    v = jnp.concatenate([v, pad], axis=-1).reshape(batch, nkb, TB, heads * VROWS)
    return v.transpose(0, 1, 3, 2)


def kernel(x, w_in, b_f, w_o, ln1_g, ln1_b, w_up, w_down, ln2_g, ln2_b):
    batch, seq, d = x.shape
    depth = w_in.shape[0]
    assert seq % TB == 0 and d == 4 * GROUP_W
    chunks = seq // LANES
    assert chunks & (chunks - 1) == 0
    alpha = (2.0 * depth) ** 0.25

    main_cols, main_scale, misc_cols = _column_plan()
    w_main = (w_in[:, :, main_cols] * main_scale).astype(BF16)
    w_misc = jnp.pad(w_in[:, :, misc_cols], ((0, 0), (0, 0), (0, LANES - len(misc_cols)))).astype(BF16)
    w_o16, w_up16, w_down16 = w_o.astype(BF16), w_up.astype(BF16), w_down.astype(BF16)
    tab64 = _rope_table(seq, HEAD_DIM)
    tab32 = _rope_table(seq, IDX_DIM)
    mult = jnp.asarray(_dilated_multiplicity())

    x2d = x.reshape(batch * seq, d)
    for l in range(depth):
        qkv, misc = _inproj(x2d, w_main[l], w_misc[l], tab64, tab32, seq)
        misc3 = misc.reshape(batch, seq, LANES)
        z = misc3[:, :, 0:GROUP_HEADS].transpose(0, 2, 1).reshape(batch, GROUP_HEADS * chunks, LANES)
        bias_rows = jnp.repeat(b_f[l].astype(F32), chunks)[:, None]
        gb = _gate_bias(z, bias_rows, chunks)
        gb = gb.reshape(batch, BIAS_TERMS, GROUP_HEADS, seq).transpose(0, 3, 2, 1)
        cbias = jnp.pad(gb.reshape(batch * seq, GROUP_HEADS * BIAS_TERMS),
                        ((0, 0), (0, LANES - GROUP_HEADS * BIAS_TERMS))).astype(BF16)
        wrow = misc3[:, :, GROUP_HEADS:GROUP_HEADS + IDX_HEADS].transpose(0, 2, 1)

        vt = _vt_all(qkv, batch, seq)
        oa = _fox(qkv, cbias, vt, batch, seq)
        ob = _dsa(qkv, vt, wrow, batch, seq)
        oc = _dilated(qkv, vt, mult, batch, seq)
        od = _moba(qkv, vt, batch, seq)

        x2d = _block_tail((oa, ob, oc, od), w_o16[l], x2d, ln1_g[l][None, :], ln1_b[l][None, :],
                          w_up16[l], w_down16[l], ln2_g[l][None, :], ln2_b[l][None, :], alpha)
    return x2d.reshape(batch, seq, d)
```
